```python
import jax, jax.numpy as jnp
from jax import lax
import numpy as np

D_MODEL = 2048
BATCH = 2
SEQ = 8192
DEPTH = 2

HEAD_DIM = 128
MIX_WIDTH = D_MODEL
N_HEADS_A = MIX_WIDTH // (2 * HEAD_DIM)
N_HEADS_B = MIX_WIDTH // (2 * HEAD_DIM)
N_HEADS_C = MIX_WIDTH // HEAD_DIM
N_KV_C = N_HEADS_C // 4
GRID_W = 64
NA_ROW_WIN = 8
NA_COL_WIN = 16
DILATED_BRANCHES = ((128, 1), (512, 4), (2048, 16))
C_RADIUS = 128
BAND_BLOCK = 128
D_FF = 4 * D_MODEL
NORM_EPS = 1e-6
NEG_INF = -1e30

kernel_name = "hybrid_natten_dilated_swa_encoder"


def rms_norm(x, g):
    xf = x.astype(jnp.float32)
    y = xf * lax.rsqrt(jnp.mean(xf * xf, axis=-1, keepdims=True) + NORM_EPS)
    return (y * g.astype(jnp.float32)).astype(x.dtype)


def split_heads(x, n_heads):
    b, t, _ = x.shape
    return x.reshape(b, t, n_heads, HEAD_DIM).transpose(0, 2, 1, 3)


def merge_heads(x):
    b, h, t, d = x.shape
    return x.transpose(0, 2, 1, 3).reshape(b, t, h * d)


def alibi_slopes(n_heads):
    return 2.0 ** (-8.0 * jnp.arange(1, n_heads + 1, dtype=jnp.float32) / n_heads)


def neighbourhood_attention(q, k, v, rpb):
    b, h, t, hd = q.shape
    rows = t // GRID_W
    kr = min(NA_ROW_WIN, rows)
    qg = q.reshape(b, h, rows, GRID_W, hd)
    kg = k.reshape(b, h, rows, GRID_W, hd)
    vg = v.reshape(b, h, rows, GRID_W, hd)
    col = jnp.arange(GRID_W)
    col_start = jnp.clip(col - NA_COL_WIN // 2, 0, GRID_W - NA_COL_WIN)
    col_idx = col_start[:, None] + jnp.arange(NA_COL_WIN)[None, :]
    dc = col_idx - col[:, None] + (NA_COL_WIN - 1)
    rpb_c = rpb.astype(jnp.float32)[:, :, dc]
    scale = HEAD_DIM ** -0.5

    def row_fn(r):
        r0 = jnp.clip(r - kr // 2, 0, rows - kr)
        q_r = lax.dynamic_index_in_dim(qg, r, axis=2, keepdims=False)
        k_w = lax.dynamic_slice_in_dim(kg, r0, kr, axis=2)[:, :, :, col_idx]
        v_w = lax.dynamic_slice_in_dim(vg, r0, kr, axis=2)[:, :, :, col_idx]
        s = jnp.einsum('bhcd,bhrckd->bhcrk', q_r, k_w).astype(jnp.float32) * scale
        dr = r0 + jnp.arange(kr) - r + (NA_ROW_WIN - 1)
        bias = jnp.take(rpb_c, dr, axis=1).transpose(0, 2, 1, 3)
        s = s + bias[None]
        p = jax.nn.softmax(s.reshape(b, h, GRID_W, kr * NA_COL_WIN), axis=-1).reshape(s.shape)
        return jnp.einsum('bhcrk,bhrckd->bhcd', p.astype(v.dtype), v_w)

    out = lax.map(row_fn, jnp.arange(rows))
    return out.transpose(1, 2, 0, 3, 4).reshape(b, h, t, hd)


def banded_attention(q, k, v, radius, dist_scale, slopes, sink=None, with_lse=False):
    n, hq, seq_len, hd = q.shape
    hkv = k.shape[1]
    grp = hq // hkv
    block = min(BAND_BLOCK, seq_len)
    n_blk = -(-seq_len // block)
    lp = n_blk * block
    span = block + 2 * radius
    qp = jnp.pad(q, ((0, 0), (0, 0), (0, lp - seq_len), (0, 0))).reshape(n, hkv, grp, lp, hd)
    pad_k = ((0, 0), (0, 0), (radius, radius + lp - seq_len), (0, 0))
    kp = jnp.pad(k, pad_k)
    vp = jnp.pad(v, pad_k)
    dist = jnp.abs(jnp.arange(block)[:, None] - jnp.arange(span)[None, :] + radius)
    band = dist <= radius
    bias = (-slopes[:, None, None] * (dist * dist_scale).astype(jnp.float32)[None]).reshape(hkv, grp, block, span)
    scale = hd ** -0.5
    if sink is not None:
        sk = sink.astype(jnp.float32).reshape(1, hkv, grp, 1)

    def block_fn(bi):
        s0 = bi * block
        qb = lax.dynamic_slice_in_dim(qp, s0, block, axis=3)
        kb = lax.dynamic_slice_in_dim(kp, s0, span, axis=2)
        vb = lax.dynamic_slice_in_dim(vp, s0, span, axis=2)
        kpos = s0 - radius + jnp.arange(span)
        valid = band & ((kpos >= 0) & (kpos < seq_len))[None, :]
        s = jnp.einsum('nkgqd,nksd->nkgqs', qb, kb).astype(jnp.float32) * scale + bias
        s = jnp.where(valid, s, NEG_INF)
        m = jnp.max(s, axis=-1)
        if sink is not None:
            m = jnp.maximum(m, sk)
        e = jnp.exp(s - m[..., None])
        den = jnp.sum(e, axis=-1)
        if sink is not None:
            den = den + jnp.exp(sk - m)
        o = jnp.einsum('nkgqs,nksd->nkgqd', (e / den[..., None]).astype(v.dtype), vb)
        if with_lse:
            return o, m + jnp.log(den)
        return o

    res = lax.map(block_fn, jnp.arange(n_blk))
    if with_lse:
        o, lse = res
        lse = lse.transpose(1, 2, 3, 0, 4).reshape(n, hq, lp)[:, :, :seq_len]
    else:
        o = res
    o = o.transpose(1, 2, 3, 0, 4, 5).reshape(n, hq, lp, hd)[:, :, :seq_len]
    if with_lse:
        return o, lse
    return o


def dilated_attention(q, k, v):
    b, h, t, hd = q.shape
    slopes = alibi_slopes(h)
    outs, lses = [], []
    for window, dil in DILATED_BRANCHES:
        sub = t // dil

        def to_res(z):
            return z.reshape(b, h, sub, dil, hd).transpose(0, 3, 1, 2, 4).reshape(b * dil, h, sub, hd)

        o, lse = banded_attention(to_res(q), to_res(k), to_res(v), window // (2 * dil), dil, slopes,
                                  with_lse=True)
        outs.append(o.reshape(b, dil, h, sub, hd).transpose(0, 2, 3, 1, 4).reshape(b, h, t, hd))
        lses.append(lse.reshape(b, dil, h, sub).transpose(0, 2, 3, 1).reshape(b, h, t))
    w = jax.nn.softmax(jnp.stack(lses, axis=0), axis=0)
    return jnp.sum(w[..., None].astype(q.dtype) * jnp.stack(outs, axis=0), axis=0)


def even_mixer(h, w_in, rpb, w_out):
    wa = N_HEADS_A * HEAD_DIM
    wb = N_HEADS_B * HEAD_DIM
    proj = h @ w_in
    qa, ka, va, qb, kb, vb = jnp.split(proj, [wa, 2 * wa, 3 * wa, 3 * wa + wb, 3 * wa + 2 * wb], axis=-1)
    oa = neighbourhood_attention(split_heads(qa, N_HEADS_A), split_heads(ka, N_HEADS_A),
                                 split_heads(va, N_HEADS_A), rpb)
    ob = dilated_attention(split_heads(qb, N_HEADS_B), split_heads(kb, N_HEADS_B),
                           split_heads(vb, N_HEADS_B))
    o = jnp.concatenate([merge_heads(oa), merge_heads(ob)], axis=-1)
    return o @ w_out


def odd_mixer(h, w_qkv, sink, w_out):
    wq = N_HEADS_C * HEAD_DIM
    wkv = N_KV_C * HEAD_DIM
    proj = h @ w_qkv
    q, k, v = jnp.split(proj, [wq, wq + wkv], axis=-1)
    o = banded_attention(split_heads(q, N_HEADS_C), split_heads(k, N_KV_C), split_heads(v, N_KV_C),
                         C_RADIUS, 1, alibi_slopes(N_HEADS_C), sink=sink)
    return merge_heads(o) @ w_out


def sq_relu_mlp(h, w1, w2):
    a = jax.nn.relu(h @ w1)
    return (a * a) @ w2


def setup_inputs(seed: int = 0) -> dict:
    key = jax.random.key(seed)
    ks = jax.random.split(key, 13)
    n_even = (DEPTH + 1) // 2
    n_odd = DEPTH // 2
    d = D_MODEL
    w_even_in = 3 * (N_HEADS_A + N_HEADS_B) * HEAD_DIM
    w_odd_in = (N_HEADS_C + 2 * N_KV_C) * HEAD_DIM
    f32 = jnp.float32
    return {
        'x': jax.random.normal(ks[0], (BATCH, SEQ, d), f32),
        'attn_norm': 1.0 + 0.02 * jax.random.normal(ks[1], (DEPTH, d), f32),
        'mlp_norm': 1.0 + 0.02 * jax.random.normal(ks[2], (DEPTH, d), f32),
        'w_mlp_in': jax.random.normal(ks[3], (DEPTH, d, D_FF), f32) * d ** -0.5,
        'w_mlp_out': jax.random.normal(ks[4], (DEPTH, D_FF, d), f32) * D_FF ** -0.5,
        'even_w_in': jax.random.normal(ks[5], (n_even, d, w_even_in), f32) * d ** -0.5,
        'even_rpb': 0.5 * jax.random.normal(ks[6], (n_even, N_HEADS_A, 2 * NA_ROW_WIN - 1, 2 * NA_COL_WIN - 1), f32),
        'even_w_out': jax.random.normal(ks[7], (n_even, MIX_WIDTH, d), f32) * MIX_WIDTH ** -0.5,
        'odd_w_qkv': jax.random.normal(ks[8], (n_odd, d, w_odd_in), f32) * d ** -0.5,
        'odd_sink': jax.random.normal(ks[9], (n_odd, N_HEADS_C), f32),
        'odd_w_out': jax.random.normal(ks[10], (n_odd, MIX_WIDTH, d), f32) * MIX_WIDTH ** -0.5,
        'final_norm': 1.0 + 0.02 * jax.random.normal(ks[11], (d,), f32),
    }


def reference(x, attn_norm, mlp_norm, w_mlp_in, w_mlp_out, even_w_in, even_rpb, even_w_out,
              odd_w_qkv, odd_sink, odd_w_out, final_norm):
    for i in range(DEPTH):
        h = rms_norm(x, attn_norm[i])
        j = i // 2
        if i % 2 == 0:
            x = x + even_mixer(h, even_w_in[j], even_rpb[j], even_w_out[j])
        else:
            x = x + odd_mixer(h, odd_w_qkv[j], odd_sink[j], odd_w_out[j])
        x = x + sq_relu_mlp(rms_norm(x, mlp_norm[i]), w_mlp_in[i], w_mlp_out[i])
    return rms_norm(x, final_norm)
```

```python
import functools

import jax
import jax.numpy as jnp
from jax import lax
from jax.experimental import pallas as pl
from jax.experimental.pallas import tpu as pltpu

HEAD_DIM = 128
GRID_W = 64
NA_ROW_WIN = 8
NA_COL_WIN = 16
DILATED_BRANCHES = ((128, 1), (512, 4), (2048, 16))
C_RADIUS = 128
NORM_EPS = 1e-6
NEG_INF = -1e30

V7X_VMEM_LIMIT_CAP = 56 * 1024 * 1024
BF16 = jnp.bfloat16
F32 = jnp.float32


def _vmem_limit(block_bytes, scratch_bytes, temp_bytes):
    need = 2 * block_bytes + scratch_bytes + temp_bytes + (2 << 20)
    return int(min(max(need, 16 << 20), V7X_VMEM_LIMIT_CAP))


def _nbytes(shape, dtype):
    n = 1
    for s in shape:
        n *= s
    return n * jnp.dtype(dtype).itemsize


def _rms_norm_f32(x, g):
    y = x * lax.rsqrt(jnp.mean(x * x, axis=-1, keepdims=True) + NORM_EPS)
    return y * g


def _norm_matmul_kernel(x_ref, g_ref, w_ref, o_ref, h_ref):
    @pl.when(pl.program_id(1) == 0)
    def _():
        h_ref[...] = _rms_norm_f32(x_ref[...], g_ref[...]).astype(h_ref.dtype)

    o_ref[...] = jnp.dot(h_ref[...], w_ref[...], preferred_element_type=F32).astype(o_ref.dtype)


def _norm_matmul(x, g, w, out_dtype, *, tm=512, tn=1024, name):
    m, d = x.shape
    n = w.shape[1]
    assert m % tm == 0 and n % tn == 0
    blocks = _nbytes((tm, d), F32) + _nbytes((d, tn), BF16) + _nbytes((tm, tn), out_dtype)
    return pl.pallas_call(
        _norm_matmul_kernel,
        grid=(m // tm, n // tn),
        in_specs=[
            pl.BlockSpec((tm, d), lambda i, j: (i, 0)),
            pl.BlockSpec((1, d), lambda i, j: (0, 0)),
            pl.BlockSpec((d, tn), lambda i, j: (0, j)),
        ],
        out_specs=pl.BlockSpec((tm, tn), lambda i, j: (i, j)),
        out_shape=jax.ShapeDtypeStruct((m, n), out_dtype),
        scratch_shapes=[pltpu.VMEM((tm, d), BF16)],
        compiler_params=pltpu.CompilerParams(
            dimension_semantics=("parallel", "arbitrary"),
            vmem_limit_bytes=_vmem_limit(blocks, _nbytes((tm, d), BF16), _nbytes((tm, d), F32) + _nbytes((tm, tn), F32)),
        ),
        name=name,
    )(x, g.reshape(1, d), w)


def _proj_residual_kernel(a_ref, b_ref, wa_ref, wb_ref, x_ref, o_ref):
    acc = jnp.dot(a_ref[...], wa_ref[...], preferred_element_type=F32)
    acc = acc + jnp.dot(b_ref[...], wb_ref[...], preferred_element_type=F32)
    o_ref[...] = x_ref[...] + acc


def _proj_residual(a, a_col, b, b_col, w, x, *, tm=512, tn=1024, name):
    m, d = x.shape
    kh = w.shape[0] // 2
    assert m % tm == 0 and d % tn == 0
    blocks = 2 * _nbytes((tm, kh), BF16) + 2 * _nbytes((kh, tn), BF16) + 2 * _nbytes((tm, tn), F32)
    return pl.pallas_call(
        _proj_residual_kernel,
        grid=(m // tm, d // tn),
        in_specs=[
            pl.BlockSpec((tm, kh), lambda i, j: (i, a_col)),
            pl.BlockSpec((tm, kh), lambda i, j: (i, b_col)),
            pl.BlockSpec((kh, tn), lambda i, j: (0, j)),
            pl.BlockSpec((kh, tn), lambda i, j: (1, j)),
            pl.BlockSpec((tm, tn), lambda i, j: (i, j)),
        ],
        out_specs=pl.BlockSpec((tm, tn), lambda i, j: (i, j)),
        out_shape=jax.ShapeDtypeStruct((m, d), F32),
        compiler_params=pltpu.CompilerParams(
            dimension_semantics=("parallel", "parallel"),
            vmem_limit_bytes=_vmem_limit(blocks, 0, 2 * _nbytes((tm, tn), F32)),
        ),
        name=name,
    )(a, b, w, w, x)


def _mlp_kernel(x_ref, g_ref, w1_ref, w2_ref, gf_ref, o_ref, h_ref, *, final_norm):
    f = pl.program_id(1)

    @pl.when(f == 0)
    def _():
        x = x_ref[...]
        h_ref[...] = _rms_norm_f32(x, g_ref[...]).astype(h_ref.dtype)
        o_ref[...] = x

    a = jnp.maximum(jnp.dot(h_ref[...], w1_ref[...], preferred_element_type=F32), 0.0)
    o_ref[...] += jnp.dot((a * a).astype(BF16), w2_ref[...], preferred_element_type=F32)

    if final_norm:
        @pl.when(f == pl.num_programs(1) - 1)
        def _():
            o_ref[...] = _rms_norm_f32(o_ref[...], gf_ref[...])


def _mlp(x, g, w1, w2, g_final, *, final_norm, tm=512, tf=1024, name):
    m, d = x.shape
    ff = w1.shape[1]
    assert m % tm == 0 and ff % tf == 0
    blocks = 2 * _nbytes((tm, d), F32) + _nbytes((d, tf), BF16) + _nbytes((tf, d), BF16)
    temps = _nbytes((tm, tf), F32) + _nbytes((tm, tf), BF16) + 2 * _nbytes((tm, d), F32)
    return pl.pallas_call(
        functools.partial(_mlp_kernel, final_norm=final_norm),
        grid=(m // tm, ff // tf),
        in_specs=[
            pl.BlockSpec((tm, d), lambda i, f: (i, 0)),
            pl.BlockSpec((1, d), lambda i, f: (0, 0)),
            pl.BlockSpec((d, tf), lambda i, f: (0, f)),
            pl.BlockSpec((tf, d), lambda i, f: (f, 0)),
            pl.BlockSpec((1, d), lambda i, f: (0, 0)),
        ],
        out_specs=pl.BlockSpec((tm, d), lambda i, f: (i, 0)),
        out_shape=jax.ShapeDtypeStruct((m, d), F32),
        scratch_shapes=[pltpu.VMEM((tm, d), BF16)],
        compiler_params=pltpu.CompilerParams(
            dimension_semantics=("parallel", "arbitrary"),
            vmem_limit_bytes=_vmem_limit(blocks, _nbytes((tm, d), BF16), temps),
        ),
        name=name,
    )(x, g.reshape(1, d), w1, w2, g_final.reshape(1, d))


def _na_bias_table(rpb):
    n_heads = rpb.shape[0]
    delta = jnp.arange(NA_ROW_WIN)[:, None]
    key_row = jnp.arange(NA_ROW_WIN)[None, :]
    dr = key_row - delta + (NA_ROW_WIN - 1)
    col = jnp.arange(GRID_W)
    col_start = jnp.clip(col - NA_COL_WIN // 2, 0, GRID_W - NA_COL_WIN)
    kc = jnp.arange(GRID_W)[None, :]
    valid = (kc >= col_start[:, None]) & (kc < col_start[:, None] + NA_COL_WIN)
    dc = jnp.clip(kc - col[:, None] + (NA_COL_WIN - 1), 0, 2 * NA_COL_WIN - 2)
    tbl = rpb.astype(F32)[:, dr[:, :, None, None], dc[None, None, :, :]]
    tbl = jnp.where(valid[None, None, None], tbl, NEG_INF)
    tbl = tbl.transpose(0, 1, 3, 2, 4)
    return tbl.reshape(n_heads, NA_ROW_WIN, GRID_W, NA_ROW_WIN * GRID_W)


def _na_kernel(q_ref, k_ref, v_ref, bias_ref, o_ref, *, rows):
    span = NA_ROW_WIN * GRID_W

    def row_body(r, carry):
        r0 = jnp.clip(r - NA_ROW_WIN // 2, 0, rows - NA_ROW_WIN)
        q_start = pl.multiple_of(r * GRID_W, GRID_W)
        k_start = pl.multiple_of(r0 * GRID_W, GRID_W)
        q = q_ref[pl.ds(q_start, GRID_W), :]
        k = k_ref[pl.ds(k_start, span), :]
        v = v_ref[pl.ds(k_start, span), :]
        s = lax.dot_general(q, k, (((1,), (1,)), ((), ())), preferred_element_type=F32)
        s = s + bias_ref[r - r0]
        m = jnp.max(s, axis=-1, keepdims=True)
        e = jnp.exp(s - m)
        den = jnp.sum(e, axis=-1, keepdims=True)
        o = jnp.dot(e.astype(BF16), v, preferred_element_type=F32) / den
        o_ref[pl.ds(q_start, GRID_W), :] = o.astype(o_ref.dtype)
        return carry

    lax.fori_loop(0, rows, row_body, 0)


def _na_attention(proj, bias_tbl, n_heads, *, name):
    b, t, _ = proj.shape
    rows = t // GRID_W
    assert rows >= NA_ROW_WIN
    span = NA_ROW_WIN * GRID_W
    head_blk = lambda off: pl.BlockSpec((None, t, HEAD_DIM), lambda bi, h: (bi, 0, off + h))
    blocks = 4 * _nbytes((t, HEAD_DIM), BF16) + _nbytes((NA_ROW_WIN, GRID_W, span), F32)
    return pl.pallas_call(
        functools.partial(_na_kernel, rows=rows),
        grid=(b, n_heads),
        in_specs=[
            head_blk(0), head_blk(n_heads), head_blk(2 * n_heads),
            pl.BlockSpec((None, NA_ROW_WIN, GRID_W, span), lambda bi, h: (h, 0, 0, 0)),
        ],
        out_specs=pl.BlockSpec((None, t, HEAD_DIM), lambda bi, h: (bi, 0, h)),
        out_shape=jax.ShapeDtypeStruct((b, t, n_heads * HEAD_DIM), BF16),
        compiler_params=pltpu.CompilerParams(
            dimension_semantics=("parallel", "parallel"),
            vmem_limit_bytes=_vmem_limit(blocks, 0, 8 << 20),
        ),
        name=name,
    )(proj, proj, proj, bias_tbl)


DIL_SUPER = 2048
DIL_QBLK = 128


def _dilated_kernel(slope_ref, q_ref, k_ref, v_ref, o_ref, acc_ref, m_ref, l_ref, *, seq_len):
    h = pl.program_id(1)
    sb = pl.program_id(2)
    slope = slope_ref[h]
    ii = lax.broadcasted_iota(jnp.int32, (DIL_QBLK, 1), 0)

    for branch, (window, dil) in enumerate(DILATED_BRANCHES):
        radius = window // (2 * dil)
        span = DIL_QBLK + 2 * radius
        sub = seq_len // dil
        blocks_per_res = DIL_SUPER // dil // DIL_QBLK
        jj = lax.broadcasted_iota(jnp.int32, (1, span), 1)
        rel0 = ii - jj

        def block_body(idx, carry, dil=dil, radius=radius, span=span, sub=sub,
                       blocks_per_res=blocks_per_res, rel0=rel0, branch=branch):
            res = idx // blocks_per_res
            c = idx % blocks_per_res
            q_row = res + dil * DIL_QBLK * c
            u0 = sb * (DIL_SUPER // dil) + DIL_QBLK * c
            ws = jnp.clip(u0 - radius, 0, sub - span)
            k_row = res + dil * ws
            if dil == 1:
                q_idx = pl.ds(q_row, DIL_QBLK)
                k_idx = pl.ds(k_row, span)
            else:
                q_idx = pl.ds(q_row, DIL_QBLK, stride=dil)
                k_idx = pl.ds(k_row, span, stride=dil)
            q = q_ref[q_idx, :].astype(BF16)
            k = k_ref[k_idx, :].astype(BF16)
            v = v_ref[k_idx, :].astype(BF16)
            s = lax.dot_general(q, k, (((1,), (1,)), ((), ())), preferred_element_type=F32)
            dist = jnp.abs(rel0 + (u0 - ws))
            s = jnp.where(dist <= radius, s - (slope * dil) * dist.astype(F32), NEG_INF)
            m_cur = jnp.max(s, axis=-1, keepdims=True)
            if branch == 0:
                m_new = jnp.broadcast_to(m_cur, (DIL_QBLK, HEAD_DIM))
                p = jnp.exp(s - m_cur)
                l_new = jnp.broadcast_to(jnp.sum(p, axis=-1, keepdims=True), (DIL_QBLK, HEAD_DIM))
                acc_new = jnp.dot(p.astype(BF16), v, preferred_element_type=F32)
            else:
                m_old = m_ref[q_idx, :]
                m_new = jnp.maximum(m_old, m_cur)
                alpha = jnp.exp(m_old - m_new)
                p = jnp.exp(s - m_new[:, :1])
                l_new = alpha * l_ref[q_idx, :] + jnp.sum(p, axis=-1, keepdims=True)
                acc_new = alpha * acc_ref[q_idx, :] + jnp.dot(p.astype(BF16), v, preferred_element_type=F32)
            m_ref[q_idx, :] = m_new
            l_ref[q_idx, :] = l_new
            acc_ref[q_idx, :] = acc_new
            return carry

        lax.fori_loop(0, DIL_SUPER // DIL_QBLK, block_body, 0)

    o_ref[...] = (acc_ref[...] / l_ref[...]).astype(o_ref.dtype)


def _dilated_attention(proj, n_heads, *, name):
    b, t, _ = proj.shape
    assert t % DIL_SUPER == 0
    for window, dil in DILATED_BRANCHES:
        assert DIL_SUPER % (dil * DIL_QBLK) == 0 and t // dil >= DIL_QBLK + window // dil
    slopes = 2.0 ** (-8.0 * jnp.arange(1, n_heads + 1, dtype=F32) / n_heads)
    kv_blk = lambda off: pl.BlockSpec((None, t, HEAD_DIM), lambda bi, h, s: (bi, 0, off + h))
    blocks = (2 * _nbytes((t, HEAD_DIM), F32) + _nbytes((DIL_SUPER, HEAD_DIM), F32)
              + _nbytes((DIL_SUPER, HEAD_DIM), BF16))
    scratch = 3 * _nbytes((DIL_SUPER, HEAD_DIM), F32)
    return pl.pallas_call(
        functools.partial(_dilated_kernel, seq_len=t),
        grid=(b, n_heads, t // DIL_SUPER),
        in_specs=[
            pl.BlockSpec(memory_space=pltpu.SMEM),
            pl.BlockSpec((None, DIL_SUPER, HEAD_DIM), lambda bi, h, s: (bi, s, h)),
            kv_blk(n_heads), kv_blk(2 * n_heads),
        ],
        out_specs=pl.BlockSpec((None, DIL_SUPER, HEAD_DIM), lambda bi, h, s: (bi, s, h)),
        out_shape=jax.ShapeDtypeStruct((b, t, n_heads * HEAD_DIM), BF16),
        scratch_shapes=[pltpu.VMEM((DIL_SUPER, HEAD_DIM), F32)] * 3,
        compiler_params=pltpu.CompilerParams(
            dimension_semantics=("parallel", "parallel", "arbitrary"),
            vmem_limit_bytes=_vmem_limit(blocks, scratch, 8 << 20),
        ),
        name=name,
    )(slopes, proj, proj, proj)


SWA_QBLK = 256


def _swa_kernel(slope_ref, sink_ref, q_ref, k_ref, v_ref, o_ref, *, seq_len, group):
    kvh = pl.program_id(1)
    s0 = pl.program_id(2) * SWA_QBLK
    span = SWA_QBLK + 2 * C_RADIUS
    ks = pl.multiple_of(jnp.clip(s0 - C_RADIUS, 0, seq_len - span), C_RADIUS)
    k = k_ref[pl.ds(ks, span), :]
    v = v_ref[pl.ds(ks, span), :]
    rel = (lax.broadcasted_iota(jnp.int32, (SWA_QBLK, 1), 0)
           - lax.broadcasted_iota(jnp.int32, (1, span), 1)) + (s0 - ks)
    dist = jnp.abs(rel)
    in_band = dist <= C_RADIUS
    dist_f = dist.astype(F32)
    for g in range(group):
        head = kvh * group + g
        q = q_ref[:, g * HEAD_DIM:(g + 1) * HEAD_DIM]
        s = lax.dot_general(q, k, (((1,), (1,)), ((), ())), preferred_element_type=F32)
        s = jnp.where(in_band, s - slope_ref[head] * dist_f, NEG_INF)
        sink = sink_ref[head]
        m = jnp.maximum(jnp.max(s, axis=-1, keepdims=True), sink)
        e = jnp.exp(s - m)
        den = jnp.sum(e, axis=-1, keepdims=True) + jnp.exp(sink - m)
        o = jnp.dot(e.astype(BF16), v, preferred_element_type=F32) / den
        o_ref[:, g * HEAD_DIM:(g + 1) * HEAD_DIM] = o.astype(o_ref.dtype)


def _swa_attention(proj, sink, n_heads, n_kv, *, name):
    b, t, _ = proj.shape
    group = n_heads // n_kv
    span = SWA_QBLK + 2 * C_RADIUS
    assert t % SWA_QBLK == 0 and t >= span
    slopes = 2.0 ** (-8.0 * jnp.arange(1, n_heads + 1, dtype=F32) / n_heads)
    kv_blk = lambda off: pl.BlockSpec((None, t, HEAD_DIM), lambda bi, kh, s: (bi, 0, off + kh))
    q_blk = pl.BlockSpec((None, SWA_QBLK, group * HEAD_DIM), lambda bi, kh, s: (bi, s, kh))
    blocks = 2 * _nbytes((t, HEAD_DIM), BF16) + 2 * _nbytes((SWA_QBLK, group * HEAD_DIM), BF16)
    return pl.pallas_call(
        functools.partial(_swa_kernel, seq_len=t, group=group),
        grid=(b, n_kv, t // SWA_QBLK),
        in_specs=[
            pl.BlockSpec(memory_space=pltpu.SMEM),
            pl.BlockSpec(memory_space=pltpu.SMEM),
            q_blk, kv_blk(n_heads), kv_blk(n_heads + n_kv),
        ],
        out_specs=q_blk,
        out_shape=jax.ShapeDtypeStruct((b, t, n_heads * HEAD_DIM), BF16),
        compiler_params=pltpu.CompilerParams(
            dimension_semantics=("parallel", "parallel", "arbitrary"),
            vmem_limit_bytes=_vmem_limit(blocks, 0, 8 << 20),
        ),
        name=name,
    )(slopes, sink.astype(F32), proj, proj, proj)


def _scaled_bf16(w, n_q_cols):
    col = jnp.arange(w.shape[1])
    scale = jnp.where(col < n_q_cols, HEAD_DIM ** -0.5, 1.0).astype(F32)
    return (w * scale[None, :]).astype(BF16)


def kernel(x, attn_norm, mlp_norm, w_mlp_in, w_mlp_out, even_w_in, even_rpb, even_w_out,
           odd_w_qkv, odd_sink, odd_w_out, final_norm):
    b, t, d = x.shape
    depth = attn_norm.shape[0]
    n_heads_a = even_rpb.shape[1]
    wa = n_heads_a * HEAD_DIM
    n_heads_b = (even_w_in.shape[2] - 3 * wa) // (3 * HEAD_DIM)
    n_heads_c = odd_sink.shape[1]
    n_kv_c = (odd_w_qkv.shape[2] // HEAD_DIM - n_heads_c) // 2
    xf = x.reshape(b * t, d)

    for i in range(depth):
        j = i // 2
        if i % 2 == 0:
            w_in = even_w_in[j]
            proj_a = _norm_matmul(xf, attn_norm[i], _scaled_bf16(w_in[:, :3 * wa], wa), BF16, name=f"l{i}_proj_a")
            proj_b = _norm_matmul(xf, attn_norm[i], _scaled_bf16(w_in[:, 3 * wa:], n_heads_b * HEAD_DIM), F32,
                                  name=f"l{i}_proj_b")
            o_a = _na_attention(proj_a.reshape(b, t, -1), _na_bias_table(even_rpb[j]), n_heads_a, name=f"l{i}_na")
            o_b = _dilated_attention(proj_b.reshape(b, t, -1), n_heads_b, name=f"l{i}_dilated")
            xf = _proj_residual(o_a.reshape(b * t, -1), 0, o_b.reshape(b * t, -1), 0,
                                even_w_out[j].astype(BF16), xf, name=f"l{i}_out")
        else:
            proj_c = _norm_matmul(xf, attn_norm[i], _scaled_bf16(odd_w_qkv[j], n_heads_c * HEAD_DIM), BF16,
                                  name=f"l{i}_proj_c")
            o_c = _swa_attention(proj_c.reshape(b, t, -1), odd_sink[j], n_heads_c, n_kv_c, name=f"l{i}_swa")
            o_c = o_c.reshape(b * t, -1)
            xf = _proj_residual(o_c, 0, o_c, 1, odd_w_out[j].astype(BF16), xf, name=f"l{i}_out")
        last = i == depth - 1
        xf = _mlp(xf, mlp_norm[i], w_mlp_in[i].astype(BF16), w_mlp_out[i].astype(BF16), final_norm,
                  final_norm=last, name=f"l{i}_mlp")
    if depth == 0:
        raise ValueError("depth must be positive")
    return xf.reshape(b, t, d)
```

```python
import functools
import math

import jax
import jax.numpy as jnp
from jax import lax
from jax.experimental import pallas as pl
from jax.experimental.pallas import tpu as pltpu

HEAD_DIM = 128
GRID_W = 64
NA_ROW_WIN = 8
NA_COL_WIN = 16
DILATED_BRANCHES = ((128, 1), (512, 4), (2048, 16))
C_RADIUS = 128
NORM_EPS = 1e-6
NEG_INF = -1e30
LOG2E = math.log2(math.e)

V7X_VMEM_LIMIT_CAP = 56 * 1024 * 1024
BF16 = jnp.bfloat16
F32 = jnp.float32


def _vmem_limit(block_bytes, scratch_bytes, temp_bytes):
    need = 2 * block_bytes + scratch_bytes + temp_bytes + (2 << 20)
    return int(min(max(need, 16 << 20), V7X_VMEM_LIMIT_CAP))


def _nbytes(shape, dtype):
    n = 1
    for s in shape:
        n *= s
    return n * jnp.dtype(dtype).itemsize


def _rms_norm_f32(x, g):
    y = x * lax.rsqrt(jnp.mean(x * x, axis=-1, keepdims=True) + NORM_EPS)
    return y * g


def _qk_scores(q, k):
    return lax.dot_general(q, k, (((1,), (1,)), ((), ())), preferred_element_type=F32)


def _pv_with_denominator(p, v):
    v_ext = jnp.concatenate([v, jnp.ones_like(v)], axis=1)
    pv = jnp.dot(p, v_ext, preferred_element_type=F32)
    return pv[:, :HEAD_DIM], pv[:, HEAD_DIM:]


def _norm_matmul_kernel(x_ref, g_ref, w_ref, o_ref, h_ref):
    @pl.when(pl.program_id(1) == 0)
    def _():
        h_ref[...] = _rms_norm_f32(x_ref[...], g_ref[...]).astype(h_ref.dtype)

    o_ref[...] = jnp.dot(h_ref[...], w_ref[...], preferred_element_type=F32).astype(o_ref.dtype)


def _norm_matmul(x, g, w, out_dtype, *, tm=512, tn=1024, name):
    m, d = x.shape
    n = w.shape[1]
    assert m % tm == 0 and n % tn == 0
    blocks = _nbytes((tm, d), F32) + _nbytes((d, tn), BF16) + _nbytes((tm, tn), out_dtype)
    return pl.pallas_call(
        _norm_matmul_kernel,
        grid=(m // tm, n // tn),
        in_specs=[
            pl.BlockSpec((tm, d), lambda i, j: (i, 0)),
            pl.BlockSpec((1, d), lambda i, j: (0, 0)),
            pl.BlockSpec((d, tn), lambda i, j: (0, j)),
        ],
        out_specs=pl.BlockSpec((tm, tn), lambda i, j: (i, j)),
        out_shape=jax.ShapeDtypeStruct((m, n), out_dtype),
        scratch_shapes=[pltpu.VMEM((tm, d), BF16)],
        compiler_params=pltpu.CompilerParams(
            dimension_semantics=("parallel", "arbitrary"),
            vmem_limit_bytes=_vmem_limit(blocks, _nbytes((tm, d), BF16), _nbytes((tm, d), F32) + _nbytes((tm, tn), F32)),
        ),
        name=name,
    )(x, g.reshape(1, d), w)


def _proj_residual_kernel(a_ref, b_ref, wa_ref, wb_ref, x_ref, o_ref):
    acc = jnp.dot(a_ref[...], wa_ref[...], preferred_element_type=F32)
    acc = acc + jnp.dot(b_ref[...], wb_ref[...], preferred_element_type=F32)
    o_ref[...] = x_ref[...] + acc


def _proj_residual(a, a_col, b, b_col, w, x, *, tm=512, tn=1024, name):
    m, d = x.shape
    kh = w.shape[0] // 2
    assert m % tm == 0 and d % tn == 0
    blocks = 2 * _nbytes((tm, kh), BF16) + 2 * _nbytes((kh, tn), BF16) + 2 * _nbytes((tm, tn), F32)
    return pl.pallas_call(
        _proj_residual_kernel,
        grid=(m // tm, d // tn),
        in_specs=[
            pl.BlockSpec((tm, kh), lambda i, j: (i, a_col)),
            pl.BlockSpec((tm, kh), lambda i, j: (i, b_col)),
            pl.BlockSpec((kh, tn), lambda i, j: (0, j)),
            pl.BlockSpec((kh, tn), lambda i, j: (1, j)),
            pl.BlockSpec((tm, tn), lambda i, j: (i, j)),
        ],
        out_specs=pl.BlockSpec((tm, tn), lambda i, j: (i, j)),
        out_shape=jax.ShapeDtypeStruct((m, d), F32),
        compiler_params=pltpu.CompilerParams(
            dimension_semantics=("parallel", "parallel"),
            vmem_limit_bytes=_vmem_limit(blocks, 0, 2 * _nbytes((tm, tn), F32)),
        ),
        name=name,
    )(a, b, w, w, x)


def _mlp_kernel(x_ref, g_ref, w1_ref, w2_ref, *rest, final_norm):
    gf_ref = rest[0] if final_norm else None
    o_ref, h_ref = rest[-2:]
    f = pl.program_id(1)

    @pl.when(f == 0)
    def _():
        x = x_ref[...]
        h_ref[...] = _rms_norm_f32(x, g_ref[...]).astype(h_ref.dtype)
        o_ref[...] = x

    a = jnp.maximum(jnp.dot(h_ref[...], w1_ref[...], preferred_element_type=F32), 0.0)
    o_ref[...] += jnp.dot((a * a).astype(BF16), w2_ref[...], preferred_element_type=F32)

    if final_norm:
        @pl.when(f == pl.num_programs(1) - 1)
        def _():
            o_ref[...] = _rms_norm_f32(o_ref[...], gf_ref[...])


def _mlp(x, g, w1, w2, g_final=None, *, tm=512, tf=1024, name):
    m, d = x.shape
    ff = w1.shape[1]
    assert m % tm == 0 and ff % tf == 0
    final_norm = g_final is not None
    blocks = 2 * _nbytes((tm, d), F32) + _nbytes((d, tf), BF16) + _nbytes((tf, d), BF16)
    temps = _nbytes((tm, tf), F32) + _nbytes((tm, tf), BF16) + 2 * _nbytes((tm, d), F32)
    gain_spec = pl.BlockSpec((1, d), lambda i, f: (0, 0))
    extra_specs, extra_args = ([gain_spec], [g_final.reshape(1, d)]) if final_norm else ([], [])
    return pl.pallas_call(
        functools.partial(_mlp_kernel, final_norm=final_norm),
        grid=(m // tm, ff // tf),
        in_specs=[
            pl.BlockSpec((tm, d), lambda i, f: (i, 0)),
            gain_spec,
            pl.BlockSpec((d, tf), lambda i, f: (0, f)),
            pl.BlockSpec((tf, d), lambda i, f: (f, 0)),
        ] + extra_specs,
        out_specs=pl.BlockSpec((tm, d), lambda i, f: (i, 0)),
        out_shape=jax.ShapeDtypeStruct((m, d), F32),
        scratch_shapes=[pltpu.VMEM((tm, d), BF16)],
        compiler_params=pltpu.CompilerParams(
            dimension_semantics=("parallel", "arbitrary"),
            vmem_limit_bytes=_vmem_limit(blocks, _nbytes((tm, d), BF16), temps),
        ),
        name=name,
    )(x, g.reshape(1, d), w1, w2, *extra_args)


NA_ROWS_PER_ITER = 4


def _na_bias_table(rpb):
    n_heads = rpb.shape[0]
    col = jnp.arange(GRID_W)
    col_start = jnp.clip(col - NA_COL_WIN // 2, 0, GRID_W - NA_COL_WIN)
    kc = jnp.arange(GRID_W)[None, :]
    valid = (kc >= col_start[:, None]) & (kc < col_start[:, None] + NA_COL_WIN)
    dc = kc - col[:, None] + (NA_COL_WIN - 1)
    onehot = (dc[:, :, None] == jnp.arange(2 * NA_COL_WIN - 1)[None, None, :]).astype(F32)
    by_col = jnp.einsum("hrd,ckd->hrck", rpb.astype(F32) * LOG2E, onehot, precision=lax.Precision.HIGHEST)
    by_col = jnp.where(valid[None, None], by_col, NEG_INF)
    per_delta = [by_col[:, NA_ROW_WIN - 1 - dl: 2 * NA_ROW_WIN - 1 - dl] for dl in range(NA_ROW_WIN)]
    tbl = jnp.stack(per_delta, axis=1)
    tbl = tbl.transpose(0, 1, 3, 2, 4)
    return tbl.reshape(n_heads, NA_ROW_WIN, GRID_W, NA_ROW_WIN * GRID_W)


def _na_kernel(q_ref, k_ref, v_ref, bias_ref, o_ref, *, rows):
    span = NA_ROW_WIN * GRID_W

    def one_row(r):
        r0 = jnp.clip(r - NA_ROW_WIN // 2, 0, rows - NA_ROW_WIN)
        q_start = pl.multiple_of(r * GRID_W, GRID_W)
        k_start = pl.multiple_of(r0 * GRID_W, GRID_W)
        q = q_ref[pl.ds(q_start, GRID_W), :]
        k = k_ref[pl.ds(k_start, span), :]
        v = v_ref[pl.ds(k_start, span), :]
        s = _qk_scores(q, k) + bias_ref[r - r0]
        m = jnp.max(s, axis=-1, keepdims=True)
        p = jnp.exp2(s - m).astype(BF16)
        num, den = _pv_with_denominator(p, v)
        o_ref[pl.ds(q_start, GRID_W), :] = (num / den).astype(o_ref.dtype)

    def body(it, carry):
        for u in range(NA_ROWS_PER_ITER):
            one_row(it * NA_ROWS_PER_ITER + u)
        return carry

    lax.fori_loop(0, rows // NA_ROWS_PER_ITER, body, 0)


def _na_attention(proj, bias_tbl, n_heads, *, name):
    b, t, _ = proj.shape
    rows = t // GRID_W
    assert rows >= NA_ROW_WIN and rows % NA_ROWS_PER_ITER == 0
    span = NA_ROW_WIN * GRID_W
    head_blk = lambda off: pl.BlockSpec((None, t, HEAD_DIM), lambda bi, h: (bi, 0, off + h))
    blocks = 4 * _nbytes((t, HEAD_DIM), BF16) + _nbytes((NA_ROW_WIN, GRID_W, span), F32)
    return pl.pallas_call(
        functools.partial(_na_kernel, rows=rows),
        grid=(b, n_heads),
        in_specs=[
            head_blk(0), head_blk(n_heads), head_blk(2 * n_heads),
            pl.BlockSpec((None, NA_ROW_WIN, GRID_W, span), lambda bi, h: (h, 0, 0, 0)),
        ],
        out_specs=pl.BlockSpec((None, t, HEAD_DIM), lambda bi, h: (bi, 0, h)),
        out_shape=jax.ShapeDtypeStruct((b, t, n_heads * HEAD_DIM), BF16),
        compiler_params=pltpu.CompilerParams(
            dimension_semantics=("parallel", "parallel"),
            vmem_limit_bytes=_vmem_limit(blocks, 0, 8 << 20),
        ),
        name=name,
    )(proj, proj, proj, bias_tbl)


DIL_SUPER = 2048
DIL_QBLK = 128
DIL_BLOCKS_PER_ITER = 4
DIL_MERGE_ROWS = 256


def _band_bias(n_q, span, offset, radius, slope2):
    rel = (lax.broadcasted_iota(jnp.int32, (n_q, 1), 0)
           - lax.broadcasted_iota(jnp.int32, (1, span), 1)) + offset
    dist = jnp.abs(rel)
    return jnp.where(dist <= radius, -slope2 * dist.astype(F32), NEG_INF)


def _dilated_kernel(slope_ref, q_ref, k_ref, v_ref, o_ref, bias_ref, acc_ref, m_ref, l_ref, *, seq_len):
    h = pl.program_id(1)
    sb = pl.program_id(2)

    @pl.when(sb == 0)
    def _():
        slope2 = slope_ref[h] * LOG2E
        for br, (window, dil) in enumerate(DILATED_BRANCHES):
            radius = window // (2 * dil)
            for var in range(3):
                bias_ref[3 * br + var] = _band_bias(DIL_QBLK, DIL_QBLK + 2 * radius, var * radius, radius, slope2 * dil)

    for br, (window, dil) in enumerate(DILATED_BRANCHES):
        radius = window // (2 * dil)
        span = DIL_QBLK + 2 * radius
        sub = seq_len // dil
        blocks_per_res = DIL_SUPER // dil // DIL_QBLK

        def one_block(idx, br=br, dil=dil, radius=radius, span=span, sub=sub, blocks_per_res=blocks_per_res):
            res = idx // blocks_per_res
            c = idx % blocks_per_res
            q_row = res + dil * DIL_QBLK * c
            u0 = sb * (DIL_SUPER // dil) + DIL_QBLK * c
            ws = jnp.clip(u0 - radius, 0, sub - span)
            k_row = res + dil * ws
            if dil == 1:
                q_idx = pl.ds(q_row, DIL_QBLK)
                k_idx = pl.ds(k_row, span)
            else:
                q_idx = pl.ds(q_row, DIL_QBLK, stride=dil)
                k_idx = pl.ds(k_row, span, stride=dil)
            q = q_ref[q_idx, :].astype(BF16)
            k = k_ref[k_idx, :].astype(BF16)
            v = v_ref[k_idx, :].astype(BF16)
            s = _qk_scores(q, k) + bias_ref[3 * br + (u0 - ws) // radius]
            m = jnp.max(s, axis=-1, keepdims=True)
            p = jnp.exp2(s - m).astype(BF16)
            num, den = _pv_with_denominator(p, v)
            acc_ref[br, q_idx, :] = num
            l_ref[br, q_idx, :] = den
            m_ref[br, q_idx, :] = jnp.broadcast_to(m, (DIL_QBLK, HEAD_DIM))

        def body(it, carry, one_block=one_block):
            for u in range(DIL_BLOCKS_PER_ITER):
                one_block(it * DIL_BLOCKS_PER_ITER + u)
            return carry

        lax.fori_loop(0, DIL_SUPER // DIL_QBLK // DIL_BLOCKS_PER_ITER, body, 0)

    n_br = len(DILATED_BRANCHES)

    def merge(it, carry):
        rows = pl.ds(pl.multiple_of(it * DIL_MERGE_ROWS, DIL_MERGE_ROWS), DIL_MERGE_ROWS)
        ms = [m_ref[br, rows, :] for br in range(n_br)]
        m_all = functools.reduce(jnp.maximum, ms)
        ws = [jnp.exp2(m - m_all) for m in ms]
        num = sum(w * acc_ref[br, rows, :] for br, w in enumerate(ws))
        den = sum(w * l_ref[br, rows, :] for br, w in enumerate(ws))
        o_ref[rows, :] = (num / den).astype(o_ref.dtype)
        return carry

    lax.fori_loop(0, DIL_SUPER // DIL_MERGE_ROWS, merge, 0)


def _dilated_attention(proj, n_heads, *, name):
    b, t, _ = proj.shape
    n_br = len(DILATED_BRANCHES)
    spans = {DIL_QBLK + window // dil for window, dil in DILATED_BRANCHES}
    assert len(spans) == 1, "bias scratch assumes one key span for all branches"
    span = spans.pop()
    assert t % DIL_SUPER == 0 and DIL_SUPER // DIL_QBLK % DIL_BLOCKS_PER_ITER == 0 and DIL_SUPER % DIL_MERGE_ROWS == 0
    for window, dil in DILATED_BRANCHES:
        assert DIL_SUPER % (dil * DIL_QBLK) == 0 and t // dil >= span and window // (2 * dil) <= DIL_QBLK
    slopes = 2.0 ** (-8.0 * jnp.arange(1, n_heads + 1, dtype=F32) / n_heads)
    kv_blk = lambda off: pl.BlockSpec((None, t, HEAD_DIM), lambda bi, h, s: (bi, 0, off + h))
    blocks = (2 * _nbytes((t, HEAD_DIM), F32) + _nbytes((DIL_SUPER, HEAD_DIM), F32)
              + _nbytes((DIL_SUPER, HEAD_DIM), BF16))
    stats = pltpu.VMEM((n_br, DIL_SUPER, HEAD_DIM), F32)
    scratch = 3 * _nbytes((n_br, DIL_SUPER, HEAD_DIM), F32) + _nbytes((3 * n_br, DIL_QBLK, span), F32)
    return pl.pallas_call(
        functools.partial(_dilated_kernel, seq_len=t),
        grid=(b, n_heads, t // DIL_SUPER),
        in_specs=[
            pl.BlockSpec(memory_space=pltpu.SMEM),
            pl.BlockSpec((None, DIL_SUPER, HEAD_DIM), lambda bi, h, s: (bi, s, h)),
            kv_blk(n_heads), kv_blk(2 * n_heads),
        ],
        out_specs=pl.BlockSpec((None, DIL_SUPER, HEAD_DIM), lambda bi, h, s: (bi, s, h)),
        out_shape=jax.ShapeDtypeStruct((b, t, n_heads * HEAD_DIM), BF16),
        scratch_shapes=[pltpu.VMEM((3 * n_br, DIL_QBLK, span), F32), stats, stats, stats],
        compiler_params=pltpu.CompilerParams(
            dimension_semantics=("parallel", "parallel", "arbitrary"),
            vmem_limit_bytes=_vmem_limit(blocks, scratch, 8 << 20),
        ),
        name=name,
    )(slopes, proj, proj, proj)


SWA_QBLK = 256


def _swa_kernel(slope_ref, sink_ref, q_ref, k_ref, v_ref, o_ref, bias_ref, *, seq_len, group):
    kvh = pl.program_id(1)
    s0 = pl.program_id(2) * SWA_QBLK
    span = SWA_QBLK + 2 * C_RADIUS

    @pl.when(pl.program_id(2) == 0)
    def _():
        for g in range(group):
            slope2 = slope_ref[kvh * group + g] * LOG2E
            for var in range(3):
                bias_ref[3 * g + var] = _band_bias(SWA_QBLK, span, var * C_RADIUS, C_RADIUS, slope2)

    ks = pl.multiple_of(jnp.clip(s0 - C_RADIUS, 0, seq_len - span), C_RADIUS)
    var = (s0 - ks) // C_RADIUS
    k = k_ref[pl.ds(ks, span), :]
    v = v_ref[pl.ds(ks, span), :]
    for g in range(group):
        q = q_ref[:, g * HEAD_DIM:(g + 1) * HEAD_DIM]
        s = _qk_scores(q, k) + bias_ref[3 * g + var]
        sink2 = sink_ref[kvh * group + g] * LOG2E
        m = jnp.maximum(jnp.max(s, axis=-1, keepdims=True), sink2)
        p = jnp.exp2(s - m).astype(BF16)
        num, den = _pv_with_denominator(p, v)
        o = num / (den + jnp.exp2(sink2 - m))
        o_ref[:, g * HEAD_DIM:(g + 1) * HEAD_DIM] = o.astype(o_ref.dtype)


def _swa_attention(proj, sink, n_heads, n_kv, *, name):
    b, t, _ = proj.shape
    group = n_heads // n_kv
    span = SWA_QBLK + 2 * C_RADIUS
    assert t % SWA_QBLK == 0 and t >= span and C_RADIUS <= SWA_QBLK and SWA_QBLK % C_RADIUS == 0
    slopes = 2.0 ** (-8.0 * jnp.arange(1, n_heads + 1, dtype=F32) / n_heads)
    kv_blk = lambda off: pl.BlockSpec((None, t, HEAD_DIM), lambda bi, kh, s: (bi, 0, off + kh))
    q_blk = pl.BlockSpec((None, SWA_QBLK, group * HEAD_DIM), lambda bi, kh, s: (bi, s, kh))
    blocks = 2 * _nbytes((t, HEAD_DIM), BF16) + 2 * _nbytes((SWA_QBLK, group * HEAD_DIM), BF16)
    bias_bytes = _nbytes((3 * group, SWA_QBLK, span), F32)
    return pl.pallas_call(
        functools.partial(_swa_kernel, seq_len=t, group=group),
        grid=(b, n_kv, t // SWA_QBLK),
        in_specs=[
            pl.BlockSpec(memory_space=pltpu.SMEM),
            pl.BlockSpec(memory_space=pltpu.SMEM),
            q_blk, kv_blk(n_heads), kv_blk(n_heads + n_kv),
        ],
        out_specs=q_blk,
        out_shape=jax.ShapeDtypeStruct((b, t, n_heads * HEAD_DIM), BF16),
        scratch_shapes=[pltpu.VMEM((3 * group, SWA_QBLK, span), F32)],
        compiler_params=pltpu.CompilerParams(
            dimension_semantics=("parallel", "parallel", "arbitrary"),
            vmem_limit_bytes=_vmem_limit(blocks, bias_bytes, 8 << 20),
        ),
        name=name,
    )(slopes, sink.astype(F32), proj, proj, proj)


def _scaled_bf16(w, n_q_cols):
    col = jnp.arange(w.shape[1])
    scale = jnp.where(col < n_q_cols, LOG2E * HEAD_DIM ** -0.5, 1.0).astype(F32)
    return (w * scale[None, :]).astype(BF16)


def kernel(x, attn_norm, mlp_norm, w_mlp_in, w_mlp_out, even_w_in, even_rpb, even_w_out,
           odd_w_qkv, odd_sink, odd_w_out, final_norm):
    b, t, d = x.shape
    depth = attn_norm.shape[0]
    n_heads_a = even_rpb.shape[1]
    wa = n_heads_a * HEAD_DIM
    n_heads_b = (even_w_in.shape[2] - 3 * wa) // (3 * HEAD_DIM)
    n_heads_c = odd_sink.shape[1]
    n_kv_c = (odd_w_qkv.shape[2] // HEAD_DIM - n_heads_c) // 2
    xf = x.reshape(b * t, d)

    for i in range(depth):
        j = i // 2
        if i % 2 == 0:
            w_in = even_w_in[j]
            proj_a = _norm_matmul(xf, attn_norm[i], _scaled_bf16(w_in[:, :3 * wa], wa), BF16, name=f"l{i}_proj_a")
            proj_b = _norm_matmul(xf, attn_norm[i], _scaled_bf16(w_in[:, 3 * wa:], n_heads_b * HEAD_DIM), F32,
                                  name=f"l{i}_proj_b")
            o_a = _na_attention(proj_a.reshape(b, t, -1), _na_bias_table(even_rpb[j]), n_heads_a, name=f"l{i}_na")
            o_b = _dilated_attention(proj_b.reshape(b, t, -1), n_heads_b, name=f"l{i}_dilated")
            xf = _proj_residual(o_a.reshape(b * t, -1), 0, o_b.reshape(b * t, -1), 0,
                                even_w_out[j].astype(BF16), xf, name=f"l{i}_out")
        else:
            proj_c = _norm_matmul(xf, attn_norm[i], _scaled_bf16(odd_w_qkv[j], n_heads_c * HEAD_DIM), BF16,
                                  name=f"l{i}_proj_c")
            o_c = _swa_attention(proj_c.reshape(b, t, -1), odd_sink[j], n_heads_c, n_kv_c, name=f"l{i}_swa")
            o_c = o_c.reshape(b * t, -1)
            xf = _proj_residual(o_c, 0, o_c, 1, odd_w_out[j].astype(BF16), xf, name=f"l{i}_out")
        g_final = final_norm if i == depth - 1 else None
        xf = _mlp(xf, mlp_norm[i], w_mlp_in[i].astype(BF16), w_mlp_out[i].astype(BF16), g_final, name=f"l{i}_mlp")
    return xf.reshape(b, t, d)
```

```python
import functools
import math

import jax
import jax.numpy as jnp
from jax import lax
from jax.experimental import pallas as pl
from jax.experimental.pallas import tpu as pltpu

HEAD_DIM = 128
GRID_W = 64
NA_ROW_WIN = 8
NA_COL_WIN = 16
DILATED_BRANCHES = ((128, 1), (512, 4), (2048, 16))
C_RADIUS = 128
NORM_EPS = 1e-6
NEG_INF = -1e30
LOG2E = math.log2(math.e)

V7X_VMEM_LIMIT_CAP = 56 * 1024 * 1024
BF16 = jnp.bfloat16
F32 = jnp.float32


def _vmem_limit(block_bytes, scratch_bytes, temp_bytes):
    need = 2 * block_bytes + scratch_bytes + temp_bytes + (2 << 20)
    return int(min(max(need, 16 << 20), V7X_VMEM_LIMIT_CAP))


def _nbytes(shape, dtype):
    n = 1
    for s in shape:
        n *= s
    return n * jnp.dtype(dtype).itemsize


def _rms_norm_f32(x, g):
    y = x * lax.rsqrt(jnp.mean(x * x, axis=-1, keepdims=True) + NORM_EPS)
    return y * g


def _qk_scores(q, k):
    return lax.dot_general(q, k, (((1,), (1,)), ((), ())), preferred_element_type=F32)


def _pv_with_denominator(p, v):
    v_ext = jnp.concatenate([v, jnp.ones_like(v)], axis=1)
    pv = jnp.dot(p, v_ext, preferred_element_type=F32)
    return pv[:, :HEAD_DIM], pv[:, HEAD_DIM:]


def _norm_matmul_kernel(x_ref, g_ref, w_ref, o_ref, h_ref):
    @pl.when(pl.program_id(1) == 0)
    def _():
        h_ref[...] = _rms_norm_f32(x_ref[...], g_ref[...]).astype(h_ref.dtype)

    o_ref[...] = jnp.dot(h_ref[...], w_ref[...], preferred_element_type=F32).astype(o_ref.dtype)


def _norm_matmul(x, g, w, out_dtype, *, tm=512, tn=1024, name):
    m, d = x.shape
    n = w.shape[1]
    assert m % tm == 0 and n % tn == 0
    blocks = _nbytes((tm, d), F32) + _nbytes((d, tn), BF16) + _nbytes((tm, tn), out_dtype)
    return pl.pallas_call(
        _norm_matmul_kernel,
        grid=(m // tm, n // tn),
        in_specs=[
            pl.BlockSpec((tm, d), lambda i, j: (i, 0)),
            pl.BlockSpec((1, d), lambda i, j: (0, 0)),
            pl.BlockSpec((d, tn), lambda i, j: (0, j)),
        ],
        out_specs=pl.BlockSpec((tm, tn), lambda i, j: (i, j)),
        out_shape=jax.ShapeDtypeStruct((m, n), out_dtype),
        scratch_shapes=[pltpu.VMEM((tm, d), BF16)],
        compiler_params=pltpu.CompilerParams(
            dimension_semantics=("parallel", "arbitrary"),
            vmem_limit_bytes=_vmem_limit(blocks, _nbytes((tm, d), BF16), _nbytes((tm, d), F32) + _nbytes((tm, tn), F32)),
        ),
        name=name,
    )(x, g.reshape(1, d), w)


def _proj_residual_kernel(a_ref, b_ref, wa_ref, wb_ref, x_ref, o_ref):
    acc = jnp.dot(a_ref[...], wa_ref[...], preferred_element_type=F32)
    acc = acc + jnp.dot(b_ref[...], wb_ref[...], preferred_element_type=F32)
    o_ref[...] = x_ref[...] + acc


def _proj_residual(a, a_col, b, b_col, w, x, *, tm=512, tn=1024, name):
    m, d = x.shape
    kh = w.shape[0] // 2
    assert m % tm == 0 and d % tn == 0
    blocks = 2 * _nbytes((tm, kh), BF16) + 2 * _nbytes((kh, tn), BF16) + 2 * _nbytes((tm, tn), F32)
    return pl.pallas_call(
        _proj_residual_kernel,
        grid=(m // tm, d // tn),
        in_specs=[
            pl.BlockSpec((tm, kh), lambda i, j: (i, a_col)),
            pl.BlockSpec((tm, kh), lambda i, j: (i, b_col)),
            pl.BlockSpec((kh, tn), lambda i, j: (0, j)),
            pl.BlockSpec((kh, tn), lambda i, j: (1, j)),
            pl.BlockSpec((tm, tn), lambda i, j: (i, j)),
        ],
        out_specs=pl.BlockSpec((tm, tn), lambda i, j: (i, j)),
        out_shape=jax.ShapeDtypeStruct((m, d), F32),
        compiler_params=pltpu.CompilerParams(
            dimension_semantics=("parallel", "parallel"),
            vmem_limit_bytes=_vmem_limit(blocks, 0, 2 * _nbytes((tm, tn), F32)),
        ),
        name=name,
    )(a, b, w, w, x)


def _mlp_kernel(x_ref, g_ref, w1_ref, w2_ref, *rest, final_norm):
    gf_ref = rest[0] if final_norm else None
    o_ref, h_ref = rest[-2:]
    f = pl.program_id(1)

    @pl.when(f == 0)
    def _():
        x = x_ref[...]
        h_ref[...] = _rms_norm_f32(x, g_ref[...]).astype(h_ref.dtype)
        o_ref[...] = x

    a = jnp.maximum(jnp.dot(h_ref[...], w1_ref[...], preferred_element_type=F32), 0.0)
    o_ref[...] += jnp.dot((a * a).astype(BF16), w2_ref[...], preferred_element_type=F32)

    if final_norm:
        @pl.when(f == pl.num_programs(1) - 1)
        def _():
            o_ref[...] = _rms_norm_f32(o_ref[...], gf_ref[...])


def _mlp(x, g, w1, w2, layer, g_final=None, *, tm=512, tf=1024, name):
    m, d = x.shape
    ff = w1.shape[2]
    assert m % tm == 0 and ff % tf == 0
    final_norm = g_final is not None
    blocks = 2 * _nbytes((tm, d), F32) + _nbytes((d, tf), BF16) + _nbytes((tf, d), BF16)
    temps = _nbytes((tm, tf), F32) + _nbytes((tm, tf), BF16) + 2 * _nbytes((tm, d), F32)
    gain_spec = pl.BlockSpec((1, d), lambda i, f: (0, 0))
    extra_specs, extra_args = ([gain_spec], [g_final.reshape(1, d)]) if final_norm else ([], [])
    return pl.pallas_call(
        functools.partial(_mlp_kernel, final_norm=final_norm),
        grid=(m // tm, ff // tf),
        in_specs=[
            pl.BlockSpec((tm, d), lambda i, f: (i, 0)),
            gain_spec,
            pl.BlockSpec((None, d, tf), lambda i, f: (layer, 0, f)),
            pl.BlockSpec((None, tf, d), lambda i, f: (layer, f, 0)),
        ] + extra_specs,
        out_specs=pl.BlockSpec((tm, d), lambda i, f: (i, 0)),
        out_shape=jax.ShapeDtypeStruct((m, d), F32),
        scratch_shapes=[pltpu.VMEM((tm, d), BF16)],
        compiler_params=pltpu.CompilerParams(
            dimension_semantics=("parallel", "arbitrary"),
            vmem_limit_bytes=_vmem_limit(blocks, _nbytes((tm, d), BF16), temps),
        ),
        name=name,
    )(x, g.reshape(1, d), w1, w2, *extra_args)


NA_ROWS_PER_ITER = 16


def _na_bias_table(rpb):
    n_heads = rpb.shape[0]
    col = jnp.arange(GRID_W)
    col_start = jnp.clip(col - NA_COL_WIN // 2, 0, GRID_W - NA_COL_WIN)
    kc = jnp.arange(GRID_W)[None, :]
    valid = (kc >= col_start[:, None]) & (kc < col_start[:, None] + NA_COL_WIN)
    dc = kc - col[:, None] + (NA_COL_WIN - 1)
    onehot = (dc[:, :, None] == jnp.arange(2 * NA_COL_WIN - 1)[None, None, :]).astype(F32)
    by_col = jnp.einsum("hrd,ckd->hrck", rpb.astype(F32) * LOG2E, onehot, precision=lax.Precision.HIGHEST)
    by_col = jnp.where(valid[None, None], by_col, NEG_INF)
    per_delta = [by_col[:, NA_ROW_WIN - 1 - dl: 2 * NA_ROW_WIN - 1 - dl] for dl in range(NA_ROW_WIN)]
    tbl = jnp.stack(per_delta, axis=1)
    tbl = tbl.transpose(0, 1, 3, 2, 4)
    return tbl.reshape(n_heads, NA_ROW_WIN, GRID_W, NA_ROW_WIN * GRID_W)


def _na_kernel(q_ref, k_ref, v_ref, bias_ref, o_ref, *, rows):
    span = NA_ROW_WIN * GRID_W

    def one_row(r):
        r0 = jnp.clip(r - NA_ROW_WIN // 2, 0, rows - NA_ROW_WIN)
        q_start = pl.multiple_of(r * GRID_W, GRID_W)
        k_start = pl.multiple_of(r0 * GRID_W, GRID_W)
        q = q_ref[pl.ds(q_start, GRID_W), :]
        k = k_ref[pl.ds(k_start, span), :]
        v = v_ref[pl.ds(k_start, span), :]
        s = _qk_scores(q, k) + bias_ref[r - r0]
        m = jnp.max(s, axis=-1, keepdims=True)
        p = jnp.exp2(s - m).astype(BF16)
        num, den = _pv_with_denominator(p, v)
        o_ref[pl.ds(q_start, GRID_W), :] = (num / den).astype(o_ref.dtype)

    def body(it, carry):
        for u in range(NA_ROWS_PER_ITER):
            one_row(it * NA_ROWS_PER_ITER + u)
        return carry

    lax.fori_loop(0, rows // NA_ROWS_PER_ITER, body, 0)


def _na_attention(proj, bias_tbl, n_heads, *, name):
    b, t, _ = proj.shape
    rows = t // GRID_W
    assert rows >= NA_ROW_WIN and rows % NA_ROWS_PER_ITER == 0
    span = NA_ROW_WIN * GRID_W
    head_blk = lambda off: pl.BlockSpec((None, t, HEAD_DIM), lambda bi, h: (bi, 0, off + h))
    blocks = 4 * _nbytes((t, HEAD_DIM), BF16) + _nbytes((NA_ROW_WIN, GRID_W, span), F32)
    return pl.pallas_call(
        functools.partial(_na_kernel, rows=rows),
        grid=(b, n_heads),
        in_specs=[
            head_blk(0), head_blk(n_heads), head_blk(2 * n_heads),
            pl.BlockSpec((None, NA_ROW_WIN, GRID_W, span), lambda bi, h: (h, 0, 0, 0)),
        ],
        out_specs=pl.BlockSpec((None, t, HEAD_DIM), lambda bi, h: (bi, 0, h)),
        out_shape=jax.ShapeDtypeStruct((b, t, n_heads * HEAD_DIM), BF16),
        compiler_params=pltpu.CompilerParams(
            dimension_semantics=("parallel", "parallel"),
            vmem_limit_bytes=_vmem_limit(blocks, 0, 8 << 20),
        ),
        name=name,
    )(proj, proj, proj, bias_tbl)


DIL_SUPER = 2048
DIL_QBLK = 128
DIL_BLOCKS_PER_ITER = 16
DIL_MERGE_ROWS = 256


def _band_bias(n_q, span, offset, radius, slope2):
    rel = (lax.broadcasted_iota(jnp.int32, (n_q, 1), 0)
           - lax.broadcasted_iota(jnp.int32, (1, span), 1)) + offset
    dist = jnp.abs(rel)
    return jnp.where(dist <= radius, -slope2 * dist.astype(F32), NEG_INF)


def _dilated_kernel(slope_ref, q_ref, k_ref, v_ref, o_ref, bias_ref, acc_ref, m_ref, l_ref, *, seq_len):
    h = pl.program_id(1)
    sb = pl.program_id(2)

    @pl.when(sb == 0)
    def _():
        slope2 = slope_ref[h] * LOG2E
        for br, (window, dil) in enumerate(DILATED_BRANCHES):
            radius = window // (2 * dil)
            for var in range(3):
                bias_ref[3 * br + var] = _band_bias(DIL_QBLK, DIL_QBLK + 2 * radius, var * radius, radius, slope2 * dil)

    for br, (window, dil) in enumerate(DILATED_BRANCHES):
        radius = window // (2 * dil)
        span = DIL_QBLK + 2 * radius
        sub = seq_len // dil
        blocks_per_res = DIL_SUPER // dil // DIL_QBLK

        def one_block(idx, br=br, dil=dil, radius=radius, span=span, sub=sub, blocks_per_res=blocks_per_res):
            res = idx // blocks_per_res
            c = idx % blocks_per_res
            q_row = res + dil * DIL_QBLK * c
            u0 = sb * (DIL_SUPER // dil) + DIL_QBLK * c
            ws = jnp.clip(u0 - radius, 0, sub - span)
            k_row = res + dil * ws
            if dil == 1:
                q_idx = pl.ds(q_row, DIL_QBLK)
                k_idx = pl.ds(k_row, span)
            else:
                q_idx = pl.ds(q_row, DIL_QBLK, stride=dil)
                k_idx = pl.ds(k_row, span, stride=dil)
            q = q_ref[q_idx, :].astype(BF16)
            k = k_ref[k_idx, :].astype(BF16)
            v = v_ref[k_idx, :].astype(BF16)
            s = _qk_scores(q, k) + bias_ref[3 * br + (u0 - ws) // radius]
            m = jnp.max(s, axis=-1, keepdims=True)
            p = jnp.exp2(s - m).astype(BF16)
            num, den = _pv_with_denominator(p, v)
            acc_ref[br, q_idx, :] = num
            l_ref[br, q_idx, :] = den
            m_ref[br, q_idx, :] = jnp.broadcast_to(m, (DIL_QBLK, HEAD_DIM))

        def body(it, carry, one_block=one_block):
            for u in range(DIL_BLOCKS_PER_ITER):
                one_block(it * DIL_BLOCKS_PER_ITER + u)
            return carry

        lax.fori_loop(0, DIL_SUPER // DIL_QBLK // DIL_BLOCKS_PER_ITER, body, 0)

    n_br = len(DILATED_BRANCHES)

    def merge(it, carry):
        rows = pl.ds(pl.multiple_of(it * DIL_MERGE_ROWS, DIL_MERGE_ROWS), DIL_MERGE_ROWS)
        ms = [m_ref[br, rows, :] for br in range(n_br)]
        m_all = functools.reduce(jnp.maximum, ms)
        ws = [jnp.exp2(m - m_all) for m in ms]
        num = sum(w * acc_ref[br, rows, :] for br, w in enumerate(ws))
        den = sum(w * l_ref[br, rows, :] for br, w in enumerate(ws))
        o_ref[rows, :] = (num / den).astype(o_ref.dtype)
        return carry

    lax.fori_loop(0, DIL_SUPER // DIL_MERGE_ROWS, merge, 0)


def _dilated_attention(proj, n_heads, *, name):
    b, t, _ = proj.shape
    n_br = len(DILATED_BRANCHES)
    spans = {DIL_QBLK + window // dil for window, dil in DILATED_BRANCHES}
    assert len(spans) == 1, "bias scratch assumes one key span for all branches"
    span = spans.pop()
    assert t % DIL_SUPER == 0 and DIL_SUPER // DIL_QBLK % DIL_BLOCKS_PER_ITER == 0 and DIL_SUPER % DIL_MERGE_ROWS == 0
    for window, dil in DILATED_BRANCHES:
        assert DIL_SUPER % (dil * DIL_QBLK) == 0 and t // dil >= span and window // (2 * dil) <= DIL_QBLK
    slopes = 2.0 ** (-8.0 * jnp.arange(1, n_heads + 1, dtype=F32) / n_heads)
    kv_blk = lambda off: pl.BlockSpec((None, t, HEAD_DIM), lambda bi, h, s: (bi, 0, off + h))
    blocks = (2 * _nbytes((t, HEAD_DIM), F32) + _nbytes((DIL_SUPER, HEAD_DIM), F32)
              + _nbytes((DIL_SUPER, HEAD_DIM), BF16))
    stats = pltpu.VMEM((n_br, DIL_SUPER, HEAD_DIM), F32)
    scratch = 3 * _nbytes((n_br, DIL_SUPER, HEAD_DIM), F32) + _nbytes((3 * n_br, DIL_QBLK, span), F32)
    return pl.pallas_call(
        functools.partial(_dilated_kernel, seq_len=t),
        grid=(b, n_heads, t // DIL_SUPER),
        in_specs=[
            pl.BlockSpec(memory_space=pltpu.SMEM),
            pl.BlockSpec((None, DIL_SUPER, HEAD_DIM), lambda bi, h, s: (bi, s, h)),
            kv_blk(n_heads), kv_blk(2 * n_heads),
        ],
        out_specs=pl.BlockSpec((None, DIL_SUPER, HEAD_DIM), lambda bi, h, s: (bi, s, h)),
        out_shape=jax.ShapeDtypeStruct((b, t, n_heads * HEAD_DIM), BF16),
        scratch_shapes=[pltpu.VMEM((3 * n_br, DIL_QBLK, span), F32), stats, stats, stats],
        compiler_params=pltpu.CompilerParams(
            dimension_semantics=("parallel", "parallel", "arbitrary"),
            vmem_limit_bytes=_vmem_limit(blocks, scratch, 8 << 20),
        ),
        name=name,
    )(slopes, proj, proj, proj)


SWA_QBLK = 256


def _swa_kernel(slope_ref, sink_ref, q_ref, k_ref, v_ref, o_ref, bias_ref, *, seq_len, group):
    kvh = pl.program_id(1)
    s0 = pl.program_id(2) * SWA_QBLK
    span = SWA_QBLK + 2 * C_RADIUS

    @pl.when(pl.program_id(2) == 0)
    def _():
        for g in range(group):
            slope2 = slope_ref[kvh * group + g] * LOG2E
            for var in range(3):
                bias_ref[3 * g + var] = _band_bias(SWA_QBLK, span, var * C_RADIUS, C_RADIUS, slope2)

    ks = pl.multiple_of(jnp.clip(s0 - C_RADIUS, 0, seq_len - span), C_RADIUS)
    var = (s0 - ks) // C_RADIUS
    k = k_ref[pl.ds(ks, span), :]
    v = v_ref[pl.ds(ks, span), :]
    for g in range(group):
        q = q_ref[:, g * HEAD_DIM:(g + 1) * HEAD_DIM]
        s = _qk_scores(q, k) + bias_ref[3 * g + var]
        sink2 = sink_ref[kvh * group + g] * LOG2E
        m = jnp.maximum(jnp.max(s, axis=-1, keepdims=True), sink2)
        p = jnp.exp2(s - m).astype(BF16)
        num, den = _pv_with_denominator(p, v)
        o = num / (den + jnp.exp2(sink2 - m))
        o_ref[:, g * HEAD_DIM:(g + 1) * HEAD_DIM] = o.astype(o_ref.dtype)


def _swa_attention(proj, sink, n_heads, n_kv, *, name):
    b, t, _ = proj.shape
    group = n_heads // n_kv
    span = SWA_QBLK + 2 * C_RADIUS
    assert t % SWA_QBLK == 0 and t >= span and C_RADIUS <= SWA_QBLK and SWA_QBLK % C_RADIUS == 0
    slopes = 2.0 ** (-8.0 * jnp.arange(1, n_heads + 1, dtype=F32) / n_heads)
    kv_blk = lambda off: pl.BlockSpec((None, t, HEAD_DIM), lambda bi, kh, s: (bi, 0, off + kh))
    q_blk = pl.BlockSpec((None, SWA_QBLK, group * HEAD_DIM), lambda bi, kh, s: (bi, s, kh))
    blocks = 2 * _nbytes((t, HEAD_DIM), BF16) + 2 * _nbytes((SWA_QBLK, group * HEAD_DIM), BF16)
    bias_bytes = _nbytes((3 * group, SWA_QBLK, span), F32)
    return pl.pallas_call(
        functools.partial(_swa_kernel, seq_len=t, group=group),
        grid=(b, n_kv, t // SWA_QBLK),
        in_specs=[
            pl.BlockSpec(memory_space=pltpu.SMEM),
            pl.BlockSpec(memory_space=pltpu.SMEM),
            q_blk, kv_blk(n_heads), kv_blk(n_heads + n_kv),
        ],
        out_specs=q_blk,
        out_shape=jax.ShapeDtypeStruct((b, t, n_heads * HEAD_DIM), BF16),
        scratch_shapes=[pltpu.VMEM((3 * group, SWA_QBLK, span), F32)],
        compiler_params=pltpu.CompilerParams(
            dimension_semantics=("parallel", "parallel", "arbitrary"),
            vmem_limit_bytes=_vmem_limit(blocks, bias_bytes, 8 << 20),
        ),
        name=name,
    )(slopes, sink.astype(F32), proj, proj, proj)


CAST_BLOCK_BYTES = 4 << 20


def _cast_kernel(w_ref, o_ref):
    o_ref[...] = w_ref[...].astype(o_ref.dtype)


def _to_bf16(w, *, name):
    lead = w.shape[:-2]
    w2 = w.reshape(-1, w.shape[-1])
    rows, cols = w2.shape
    rb = min(rows, max(16, CAST_BLOCK_BYTES // (4 * cols)))
    assert rows % rb == 0
    out = pl.pallas_call(
        _cast_kernel,
        grid=(rows // rb,),
        in_specs=[pl.BlockSpec((rb, cols), lambda i: (i, 0))],
        out_specs=pl.BlockSpec((rb, cols), lambda i: (i, 0)),
        out_shape=jax.ShapeDtypeStruct((rows, cols), BF16),
        compiler_params=pltpu.CompilerParams(
            dimension_semantics=("parallel",),
            vmem_limit_bytes=_vmem_limit(_nbytes((rb, cols), F32) + _nbytes((rb, cols), BF16), 0, 0),
        ),
        name=name,
    )(w2)
    return out.reshape(*lead, *w.shape[-2:])


def _scaled_bf16(w, n_q_cols):
    col = jnp.arange(w.shape[1])
    scale = jnp.where(col < n_q_cols, LOG2E * HEAD_DIM ** -0.5, 1.0).astype(F32)
    return (w * scale[None, :]).astype(BF16)


def kernel(x, attn_norm, mlp_norm, w_mlp_in, w_mlp_out, even_w_in, even_rpb, even_w_out,
           odd_w_qkv, odd_sink, odd_w_out, final_norm):
    b, t, d = x.shape
    depth = attn_norm.shape[0]
    n_heads_a = even_rpb.shape[1]
    wa = n_heads_a * HEAD_DIM
    n_heads_b = (even_w_in.shape[2] - 3 * wa) // (3 * HEAD_DIM)
    n_heads_c = odd_sink.shape[1]
    n_kv_c = (odd_w_qkv.shape[2] // HEAD_DIM - n_heads_c) // 2
    xf = x.reshape(b * t, d)
    w1_bf16 = _to_bf16(w_mlp_in, name="cast_mlp_in")
    w2_bf16 = _to_bf16(w_mlp_out, name="cast_mlp_out")

    for i in range(depth):
        j = i // 2
        if i % 2 == 0:
            w_in = even_w_in[j]
            proj_a = _norm_matmul(xf, attn_norm[i], _scaled_bf16(w_in[:, :3 * wa], wa), BF16, name=f"l{i}_proj_a")
            proj_b = _norm_matmul(xf, attn_norm[i], _scaled_bf16(w_in[:, 3 * wa:], n_heads_b * HEAD_DIM), F32,
                                  name=f"l{i}_proj_b")
            o_a = _na_attention(proj_a.reshape(b, t, -1), _na_bias_table(even_rpb[j]), n_heads_a, name=f"l{i}_na")
            o_b = _dilated_attention(proj_b.reshape(b, t, -1), n_heads_b, name=f"l{i}_dilated")
            xf = _proj_residual(o_a.reshape(b * t, -1), 0, o_b.reshape(b * t, -1), 0,
                                even_w_out[j].astype(BF16), xf, name=f"l{i}_out")
        else:
            proj_c = _norm_matmul(xf, attn_norm[i], _scaled_bf16(odd_w_qkv[j], n_heads_c * HEAD_DIM), BF16,
                                  name=f"l{i}_proj_c")
            o_c = _swa_attention(proj_c.reshape(b, t, -1), odd_sink[j], n_heads_c, n_kv_c, name=f"l{i}_swa")
            o_c = o_c.reshape(b * t, -1)
            xf = _proj_residual(o_c, 0, o_c, 1, odd_w_out[j].astype(BF16), xf, name=f"l{i}_out")
        g_final = final_norm if i == depth - 1 else None
        xf = _mlp(xf, mlp_norm[i], w1_bf16, w2_bf16, i, g_final, name=f"l{i}_mlp")
    return xf.reshape(b, t, d)
```

```python
import functools
import math

import jax
import jax.numpy as jnp
from jax import lax
from jax.experimental import pallas as pl
from jax.experimental.pallas import tpu as pltpu

HEAD_DIM = 128
GRID_W = 64
NA_ROW_WIN = 8
NA_COL_WIN = 16
DILATED_BRANCHES = ((128, 1), (512, 4), (2048, 16))
C_RADIUS = 128
NORM_EPS = 1e-6
NEG_INF = -1e30
LOG2E = math.log2(math.e)

V7X_VMEM_LIMIT_CAP = 56 * 1024 * 1024
BF16 = jnp.bfloat16
F32 = jnp.float32


def _vmem_limit(block_bytes, scratch_bytes, temp_bytes):
    need = 2 * block_bytes + scratch_bytes + temp_bytes + (2 << 20)
    return int(min(max(need, 16 << 20), V7X_VMEM_LIMIT_CAP))


def _nbytes(shape, dtype):
    n = 1
    for s in shape:
        n *= s
    return n * jnp.dtype(dtype).itemsize


def _rms_norm_f32(x, g):
    y = x * lax.rsqrt(jnp.mean(x * x, axis=-1, keepdims=True) + NORM_EPS)
    return y * g


def _qk_scores(q, k):
    return lax.dot_general(q, k, (((1,), (1,)), ((), ())), preferred_element_type=F32)


def _pv_with_denominator(p, v):
    v_ext = jnp.concatenate([v, jnp.ones_like(v)], axis=1)
    pv = jnp.dot(p, v_ext, preferred_element_type=F32)
    return pv[:, :HEAD_DIM], pv[:, HEAD_DIM:]


def _norm_matmul_kernel(x_ref, g_ref, w_ref, o_ref, h_ref):
    @pl.when(pl.program_id(1) == 0)
    def _():
        h_ref[...] = _rms_norm_f32(x_ref[...], g_ref[...]).astype(h_ref.dtype)

    o_ref[...] = jnp.dot(h_ref[...], w_ref[...], preferred_element_type=F32).astype(o_ref.dtype)


def _norm_matmul(x, g, w, out_dtype, *, tm=1024, tn=1024, name):
    m, d = x.shape
    n = w.shape[1]
    assert m % tm == 0 and n % tn == 0
    blocks = _nbytes((tm, d), F32) + _nbytes((d, tn), BF16) + _nbytes((tm, tn), out_dtype)
    return pl.pallas_call(
        _norm_matmul_kernel,
        grid=(m // tm, n // tn),
        in_specs=[
            pl.BlockSpec((tm, d), lambda i, j: (i, 0)),
            pl.BlockSpec((1, d), lambda i, j: (0, 0)),
            pl.BlockSpec((d, tn), lambda i, j: (0, j)),
        ],
        out_specs=pl.BlockSpec((tm, tn), lambda i, j: (i, j)),
        out_shape=jax.ShapeDtypeStruct((m, n), out_dtype),
        scratch_shapes=[pltpu.VMEM((tm, d), BF16)],
        compiler_params=pltpu.CompilerParams(
            dimension_semantics=("parallel", "arbitrary"),
            vmem_limit_bytes=_vmem_limit(blocks, _nbytes((tm, d), BF16), _nbytes((tm, d), F32) + _nbytes((tm, tn), F32)),
        ),
        name=name,
    )(x, g.reshape(1, d), w)


def _proj_residual_kernel(a_ref, b_ref, wa_ref, wb_ref, x_ref, o_ref):
    acc = jnp.dot(a_ref[...], wa_ref[...], preferred_element_type=F32)
    acc = acc + jnp.dot(b_ref[...], wb_ref[...], preferred_element_type=F32)
    o_ref[...] = x_ref[...] + acc


def _proj_residual(a, a_col, b, b_col, w, x, *, tm=1024, tn=1024, name):
    m, d = x.shape
    kh = w.shape[0] // 2
    assert m % tm == 0 and d % tn == 0
    blocks = 2 * _nbytes((tm, kh), BF16) + 2 * _nbytes((kh, tn), BF16) + 2 * _nbytes((tm, tn), F32)
    return pl.pallas_call(
        _proj_residual_kernel,
        grid=(m // tm, d // tn),
        in_specs=[
            pl.BlockSpec((tm, kh), lambda i, j: (i, a_col)),
            pl.BlockSpec((tm, kh), lambda i, j: (i, b_col)),
            pl.BlockSpec((kh, tn), lambda i, j: (0, j)),
            pl.BlockSpec((kh, tn), lambda i, j: (1, j)),
            pl.BlockSpec((tm, tn), lambda i, j: (i, j)),
        ],
        out_specs=pl.BlockSpec((tm, tn), lambda i, j: (i, j)),
        out_shape=jax.ShapeDtypeStruct((m, d), F32),
        compiler_params=pltpu.CompilerParams(
            dimension_semantics=("parallel", "parallel"),
            vmem_limit_bytes=_vmem_limit(blocks, 0, 2 * _nbytes((tm, tn), F32)),
        ),
        name=name,
    )(a, b, w, w, x)


def _mlp_kernel(x_ref, g_ref, w1_ref, w2_ref, *rest, final_norm):
    gf_ref = rest[0] if final_norm else None
    o_ref, h_ref = rest[-2:]
    f = pl.program_id(1)

    @pl.when(f == 0)
    def _():
        x = x_ref[...]
        h_ref[...] = _rms_norm_f32(x, g_ref[...]).astype(h_ref.dtype)
        o_ref[...] = x

    a = jnp.maximum(jnp.dot(h_ref[...], w1_ref[...], preferred_element_type=F32), 0.0)
    o_ref[...] += jnp.dot((a * a).astype(BF16), w2_ref[...], preferred_element_type=F32)

    if final_norm:
        @pl.when(f == pl.num_programs(1) - 1)
        def _():
            o_ref[...] = _rms_norm_f32(o_ref[...], gf_ref[...])


def _mlp(x, g, w1, w2, layer, g_final=None, *, tm=1024, tf=1024, name):
    m, d = x.shape
    ff = w1.shape[2]
    assert m % tm == 0 and ff % tf == 0
    final_norm = g_final is not None
    blocks = _nbytes((tm, d), F32) + _nbytes((d, tf), BF16) + _nbytes((tf, d), BF16)
    resident = _nbytes((tm, d), F32) + _nbytes((tm, d), BF16)
    temps = _nbytes((tm, tf), F32) + _nbytes((tm, tf), BF16)
    gain_spec = pl.BlockSpec((1, d), lambda i, f: (0, 0))
    extra_specs, extra_args = ([gain_spec], [g_final.reshape(1, d)]) if final_norm else ([], [])
    return pl.pallas_call(
        functools.partial(_mlp_kernel, final_norm=final_norm),
        grid=(m // tm, ff // tf),
        in_specs=[
            pl.BlockSpec((tm, d), lambda i, f: (i, 0), pipeline_mode=pl.Buffered(1)),
            gain_spec,
            pl.BlockSpec((None, d, tf), lambda i, f: (layer, 0, f)),
            pl.BlockSpec((None, tf, d), lambda i, f: (layer, f, 0)),
        ] + extra_specs,
        out_specs=pl.BlockSpec((tm, d), lambda i, f: (i, 0)),
        out_shape=jax.ShapeDtypeStruct((m, d), F32),
        scratch_shapes=[pltpu.VMEM((tm, d), BF16)],
        compiler_params=pltpu.CompilerParams(
            dimension_semantics=("parallel", "arbitrary"),
            vmem_limit_bytes=_vmem_limit(blocks, resident, temps),
        ),
        name=name,
    )(x, g.reshape(1, d), w1, w2, *extra_args)


NA_ROWS_PER_ITER = 16


def _na_bias_table(rpb):
    n_heads = rpb.shape[0]
    col = jnp.arange(GRID_W)
    col_start = jnp.clip(col - NA_COL_WIN // 2, 0, GRID_W - NA_COL_WIN)
    kc = jnp.arange(GRID_W)[None, :]
    valid = (kc >= col_start[:, None]) & (kc < col_start[:, None] + NA_COL_WIN)
    dc = kc - col[:, None] + (NA_COL_WIN - 1)
    onehot = (dc[:, :, None] == jnp.arange(2 * NA_COL_WIN - 1)[None, None, :]).astype(F32)
    by_col = jnp.einsum("hrd,ckd->hrck", rpb.astype(F32) * LOG2E, onehot, precision=lax.Precision.HIGHEST)
    by_col = jnp.where(valid[None, None], by_col, NEG_INF)
    per_delta = [by_col[:, NA_ROW_WIN - 1 - dl: 2 * NA_ROW_WIN - 1 - dl] for dl in range(NA_ROW_WIN)]
    tbl = jnp.stack(per_delta, axis=1)
    tbl = tbl.transpose(0, 1, 3, 2, 4)
    return tbl.reshape(n_heads, NA_ROW_WIN, GRID_W, NA_ROW_WIN * GRID_W)


def _na_kernel(q_ref, k_ref, v_ref, bias_ref, o_ref, *, rows):
    span = NA_ROW_WIN * GRID_W

    def one_row(r):
        r0 = jnp.clip(r - NA_ROW_WIN // 2, 0, rows - NA_ROW_WIN)
        q_start = pl.multiple_of(r * GRID_W, GRID_W)
        k_start = pl.multiple_of(r0 * GRID_W, GRID_W)
        q = q_ref[pl.ds(q_start, GRID_W), :]
        k = k_ref[pl.ds(k_start, span), :]
        v = v_ref[pl.ds(k_start, span), :]
        s = _qk_scores(q, k) + bias_ref[r - r0]
        m = jnp.max(s, axis=-1, keepdims=True)
        p = jnp.exp2(s - m).astype(BF16)
        num, den = _pv_with_denominator(p, v)
        o_ref[pl.ds(q_start, GRID_W), :] = (num / den).astype(o_ref.dtype)

    def body(it, carry):
        for u in range(NA_ROWS_PER_ITER):
            one_row(it * NA_ROWS_PER_ITER + u)
        return carry

    lax.fori_loop(0, rows // NA_ROWS_PER_ITER, body, 0)


def _na_attention(proj, bias_tbl, n_heads, *, name):
    b, t, _ = proj.shape
    rows = t // GRID_W
    assert rows >= NA_ROW_WIN and rows % NA_ROWS_PER_ITER == 0
    span = NA_ROW_WIN * GRID_W
    head_blk = lambda off: pl.BlockSpec((None, t, HEAD_DIM), lambda bi, h: (bi, 0, off + h))
    blocks = 4 * _nbytes((t, HEAD_DIM), BF16) + _nbytes((NA_ROW_WIN, GRID_W, span), F32)
    return pl.pallas_call(
        functools.partial(_na_kernel, rows=rows),
        grid=(b, n_heads),
        in_specs=[
            head_blk(0), head_blk(n_heads), head_blk(2 * n_heads),
            pl.BlockSpec((None, NA_ROW_WIN, GRID_W, span), lambda bi, h: (h, 0, 0, 0)),
        ],
        out_specs=pl.BlockSpec((None, t, HEAD_DIM), lambda bi, h: (bi, 0, h)),
        out_shape=jax.ShapeDtypeStruct((b, t, n_heads * HEAD_DIM), BF16),
        compiler_params=pltpu.CompilerParams(
            dimension_semantics=("parallel", "parallel"),
            vmem_limit_bytes=_vmem_limit(blocks, 0, 8 << 20),
        ),
        name=name,
    )(proj, proj, proj, bias_tbl)


DIL_SUPER = 2048
DIL_QBLK = 128
DIL_BLOCKS_PER_ITER = 16
DIL_MERGE_ROWS = 256


def _band_bias(n_q, span, offset, radius, slope2):
    rel = (lax.broadcasted_iota(jnp.int32, (n_q, 1), 0)
           - lax.broadcasted_iota(jnp.int32, (1, span), 1)) + offset
    dist = jnp.abs(rel)
    return jnp.where(dist <= radius, -slope2 * dist.astype(F32), NEG_INF)


def _dilated_kernel(slope_ref, q_ref, k_ref, v_ref, o_ref, bias_ref, acc_ref, m_ref, l_ref, *, seq_len):
    h = pl.program_id(1)
    sb = pl.program_id(2)

    @pl.when(sb == 0)
    def _():
        slope2 = slope_ref[h] * LOG2E
        for br, (window, dil) in enumerate(DILATED_BRANCHES):
            radius = window // (2 * dil)
            for var in range(3):
                bias_ref[3 * br + var] = _band_bias(DIL_QBLK, DIL_QBLK + 2 * radius, var * radius, radius, slope2 * dil)

    for br, (window, dil) in enumerate(DILATED_BRANCHES):
        radius = window // (2 * dil)
        span = DIL_QBLK + 2 * radius
        sub = seq_len // dil
        blocks_per_res = DIL_SUPER // dil // DIL_QBLK

        def one_block(idx, br=br, dil=dil, radius=radius, span=span, sub=sub, blocks_per_res=blocks_per_res):
            res = idx // blocks_per_res
            c = idx % blocks_per_res
            q_row = res + dil * DIL_QBLK * c
            u0 = sb * (DIL_SUPER // dil) + DIL_QBLK * c
            ws = jnp.clip(u0 - radius, 0, sub - span)
            k_row = res + dil * ws
            if dil == 1:
                q_idx = pl.ds(q_row, DIL_QBLK)
                k_idx = pl.ds(k_row, span)
            else:
                q_idx = pl.ds(q_row, DIL_QBLK, stride=dil)
                k_idx = pl.ds(k_row, span, stride=dil)
            q = q_ref[q_idx, :].astype(BF16)
            k = k_ref[k_idx, :].astype(BF16)
            v = v_ref[k_idx, :].astype(BF16)
            s = _qk_scores(q, k) + bias_ref[3 * br + (u0 - ws) // radius]
            m = jnp.max(s, axis=-1, keepdims=True)
            p = jnp.exp2(s - m).astype(BF16)
            num, den = _pv_with_denominator(p, v)
            acc_ref[br, q_idx, :] = num
            l_ref[br, q_idx, :] = den
            m_ref[br, q_idx, :] = jnp.broadcast_to(m, (DIL_QBLK, HEAD_DIM))

        def body(it, carry, one_block=one_block):
            for u in range(DIL_BLOCKS_PER_ITER):
                one_block(it * DIL_BLOCKS_PER_ITER + u)
            return carry

        lax.fori_loop(0, DIL_SUPER // DIL_QBLK // DIL_BLOCKS_PER_ITER, body, 0)

    n_br = len(DILATED_BRANCHES)

    def merge(it, carry):
        rows = pl.ds(pl.multiple_of(it * DIL_MERGE_ROWS, DIL_MERGE_ROWS), DIL_MERGE_ROWS)
        ms = [m_ref[br, rows, :] for br in range(n_br)]
        m_all = functools.reduce(jnp.maximum, ms)
        ws = [jnp.exp2(m - m_all) for m in ms]
        num = sum(w * acc_ref[br, rows, :] for br, w in enumerate(ws))
        den = sum(w * l_ref[br, rows, :] for br, w in enumerate(ws))
        o_ref[rows, :] = (num / den).astype(o_ref.dtype)
        return carry

    lax.fori_loop(0, DIL_SUPER // DIL_MERGE_ROWS, merge, 0)


def _dilated_attention(proj, n_heads, *, name):
    b, t, _ = proj.shape
    n_br = len(DILATED_BRANCHES)
    spans = {DIL_QBLK + window // dil for window, dil in DILATED_BRANCHES}
    assert len(spans) == 1, "bias scratch assumes one key span for all branches"
    span = spans.pop()
    assert t % DIL_SUPER == 0 and DIL_SUPER // DIL_QBLK % DIL_BLOCKS_PER_ITER == 0 and DIL_SUPER % DIL_MERGE_ROWS == 0
    for window, dil in DILATED_BRANCHES:
        assert DIL_SUPER % (dil * DIL_QBLK) == 0 and t // dil >= span and window // (2 * dil) <= DIL_QBLK
    slopes = 2.0 ** (-8.0 * jnp.arange(1, n_heads + 1, dtype=F32) / n_heads)
    kv_blk = lambda off: pl.BlockSpec((None, t, HEAD_DIM), lambda bi, h, s: (bi, 0, off + h))
    blocks = (2 * _nbytes((t, HEAD_DIM), F32) + _nbytes((DIL_SUPER, HEAD_DIM), F32)
              + _nbytes((DIL_SUPER, HEAD_DIM), BF16))
    stats = pltpu.VMEM((n_br, DIL_SUPER, HEAD_DIM), F32)
    scratch = 3 * _nbytes((n_br, DIL_SUPER, HEAD_DIM), F32) + _nbytes((3 * n_br, DIL_QBLK, span), F32)
    return pl.pallas_call(
        functools.partial(_dilated_kernel, seq_len=t),
        grid=(b, n_heads, t // DIL_SUPER),
        in_specs=[
            pl.BlockSpec(memory_space=pltpu.SMEM),
            pl.BlockSpec((None, DIL_SUPER, HEAD_DIM), lambda bi, h, s: (bi, s, h)),
            kv_blk(n_heads), kv_blk(2 * n_heads),
        ],
        out_specs=pl.BlockSpec((None, DIL_SUPER, HEAD_DIM), lambda bi, h, s: (bi, s, h)),
        out_shape=jax.ShapeDtypeStruct((b, t, n_heads * HEAD_DIM), BF16),
        scratch_shapes=[pltpu.VMEM((3 * n_br, DIL_QBLK, span), F32), stats, stats, stats],
        compiler_params=pltpu.CompilerParams(
            dimension_semantics=("parallel", "parallel", "arbitrary"),
            vmem_limit_bytes=_vmem_limit(blocks, scratch, 8 << 20),
        ),
        name=name,
    )(slopes, proj, proj, proj)


SWA_QBLK = 256


def _swa_kernel(slope_ref, sink_ref, q_ref, k_ref, v_ref, o_ref, bias_ref, *, seq_len, group):
    kvh = pl.program_id(1)
    s0 = pl.program_id(2) * SWA_QBLK
    span = SWA_QBLK + 2 * C_RADIUS

    @pl.when(pl.program_id(2) == 0)
    def _():
        for g in range(group):
            slope2 = slope_ref[kvh * group + g] * LOG2E
            for var in range(3):
                bias_ref[3 * g + var] = _band_bias(SWA_QBLK, span, var * C_RADIUS, C_RADIUS, slope2)

    ks = pl.multiple_of(jnp.clip(s0 - C_RADIUS, 0, seq_len - span), C_RADIUS)
    var = (s0 - ks) // C_RADIUS
    k = k_ref[pl.ds(ks, span), :]
    v = v_ref[pl.ds(ks, span), :]
    for g in range(group):
        q = q_ref[:, g * HEAD_DIM:(g + 1) * HEAD_DIM]
        s = _qk_scores(q, k) + bias_ref[3 * g + var]
        sink2 = sink_ref[kvh * group + g] * LOG2E
        m = jnp.maximum(jnp.max(s, axis=-1, keepdims=True), sink2)
        p = jnp.exp2(s - m).astype(BF16)
        num, den = _pv_with_denominator(p, v)
        o = num / (den + jnp.exp2(sink2 - m))
        o_ref[:, g * HEAD_DIM:(g + 1) * HEAD_DIM] = o.astype(o_ref.dtype)


def _swa_attention(proj, sink, n_heads, n_kv, *, name):
    b, t, _ = proj.shape
    group = n_heads // n_kv
    span = SWA_QBLK + 2 * C_RADIUS
    assert t % SWA_QBLK == 0 and t >= span and C_RADIUS <= SWA_QBLK and SWA_QBLK % C_RADIUS == 0
    slopes = 2.0 ** (-8.0 * jnp.arange(1, n_heads + 1, dtype=F32) / n_heads)
    kv_blk = lambda off: pl.BlockSpec((None, t, HEAD_DIM), lambda bi, kh, s: (bi, 0, off + kh))
    q_blk = pl.BlockSpec((None, SWA_QBLK, group * HEAD_DIM), lambda bi, kh, s: (bi, s, kh))
    blocks = 2 * _nbytes((t, HEAD_DIM), BF16) + 2 * _nbytes((SWA_QBLK, group * HEAD_DIM), BF16)
    bias_bytes = _nbytes((3 * group, SWA_QBLK, span), F32)
    return pl.pallas_call(
        functools.partial(_swa_kernel, seq_len=t, group=group),
        grid=(b, n_kv, t // SWA_QBLK),
        in_specs=[
            pl.BlockSpec(memory_space=pltpu.SMEM),
            pl.BlockSpec(memory_space=pltpu.SMEM),
            q_blk, kv_blk(n_heads), kv_blk(n_heads + n_kv),
        ],
        out_specs=q_blk,
        out_shape=jax.ShapeDtypeStruct((b, t, n_heads * HEAD_DIM), BF16),
        scratch_shapes=[pltpu.VMEM((3 * group, SWA_QBLK, span), F32)],
        compiler_params=pltpu.CompilerParams(
            dimension_semantics=("parallel", "parallel", "arbitrary"),
            vmem_limit_bytes=_vmem_limit(blocks, bias_bytes, 8 << 20),
        ),
        name=name,
    )(slopes, sink.astype(F32), proj, proj, proj)


CAST_BLOCK_BYTES = 4 << 20


def _cast_kernel(w_ref, o_ref):
    o_ref[...] = w_ref[...].astype(o_ref.dtype)


def _to_bf16(w, *, name):
    lead = w.shape[:-2]
    w2 = w.reshape(-1, w.shape[-1])
    rows, cols = w2.shape
    rb = min(rows, max(16, CAST_BLOCK_BYTES // (4 * cols)))
    assert rows % rb == 0
    out = pl.pallas_call(
        _cast_kernel,
        grid=(rows // rb,),
        in_specs=[pl.BlockSpec((rb, cols), lambda i: (i, 0))],
        out_specs=pl.BlockSpec((rb, cols), lambda i: (i, 0)),
        out_shape=jax.ShapeDtypeStruct((rows, cols), BF16),
        compiler_params=pltpu.CompilerParams(
            dimension_semantics=("parallel",),
            vmem_limit_bytes=_vmem_limit(_nbytes((rb, cols), F32) + _nbytes((rb, cols), BF16), 0, 0),
        ),
        name=name,
    )(w2)
    return out.reshape(*lead, *w.shape[-2:])


def _scaled_bf16(w, n_q_cols):
    col = jnp.arange(w.shape[1])
    scale = jnp.where(col < n_q_cols, LOG2E * HEAD_DIM ** -0.5, 1.0).astype(F32)
    return (w * scale[None, :]).astype(BF16)


def kernel(x, attn_norm, mlp_norm, w_mlp_in, w_mlp_out, even_w_in, even_rpb, even_w_out,
           odd_w_qkv, odd_sink, odd_w_out, final_norm):
    b, t, d = x.shape
    depth = attn_norm.shape[0]
    n_heads_a = even_rpb.shape[1]
    wa = n_heads_a * HEAD_DIM
    n_heads_b = (even_w_in.shape[2] - 3 * wa) // (3 * HEAD_DIM)
    n_heads_c = odd_sink.shape[1]
    n_kv_c = (odd_w_qkv.shape[2] // HEAD_DIM - n_heads_c) // 2
    xf = x.reshape(b * t, d)
    w1_bf16 = _to_bf16(w_mlp_in, name="cast_mlp_in")
    w2_bf16 = _to_bf16(w_mlp_out, name="cast_mlp_out")

    for i in range(depth):
        j = i // 2
        if i % 2 == 0:
            w_in = even_w_in[j]
            proj_a = _norm_matmul(xf, attn_norm[i], _scaled_bf16(w_in[:, :3 * wa], wa), BF16, name=f"l{i}_proj_a")
            proj_b = _norm_matmul(xf, attn_norm[i], _scaled_bf16(w_in[:, 3 * wa:], n_heads_b * HEAD_DIM), F32,
                                  name=f"l{i}_proj_b")
            o_a = _na_attention(proj_a.reshape(b, t, -1), _na_bias_table(even_rpb[j]), n_heads_a, name=f"l{i}_na")
            o_b = _dilated_attention(proj_b.reshape(b, t, -1), n_heads_b, name=f"l{i}_dilated")
            xf = _proj_residual(o_a.reshape(b * t, -1), 0, o_b.reshape(b * t, -1), 0,
                                even_w_out[j].astype(BF16), xf, name=f"l{i}_out")
        else:
            proj_c = _norm_matmul(xf, attn_norm[i], _scaled_bf16(odd_w_qkv[j], n_heads_c * HEAD_DIM), BF16,
                                  name=f"l{i}_proj_c")
            o_c = _swa_attention(proj_c.reshape(b, t, -1), odd_sink[j], n_heads_c, n_kv_c, name=f"l{i}_swa")
            o_c = o_c.reshape(b * t, -1)
            xf = _proj_residual(o_c, 0, o_c, 1, odd_w_out[j].astype(BF16), xf, name=f"l{i}_out")
        g_final = final_norm if i == depth - 1 else None
        xf = _mlp(xf, mlp_norm[i], w1_bf16, w2_bf16, i, g_final, name=f"l{i}_mlp")
    return xf.reshape(b, t, d)
```

```python
import functools
import math

import jax
import jax.numpy as jnp
from jax import lax
from jax.experimental import pallas as pl
from jax.experimental.pallas import tpu as pltpu

HEAD_DIM = 128
GRID_W = 64
NA_ROW_WIN = 8
NA_COL_WIN = 16
DILATED_BRANCHES = ((128, 1), (512, 4), (2048, 16))
C_RADIUS = 128
NORM_EPS = 1e-6
NEG_INF = -1e30
LOG2E = math.log2(math.e)

V7X_VMEM_LIMIT_CAP = 56 * 1024 * 1024
BF16 = jnp.bfloat16
F32 = jnp.float32


def _vmem_limit(block_bytes, scratch_bytes, temp_bytes):
    need = 2 * block_bytes + scratch_bytes + temp_bytes + (2 << 20)
    return int(min(max(need, 16 << 20), V7X_VMEM_LIMIT_CAP))


def _nbytes(shape, dtype):
    n = 1
    for s in shape:
        n *= s
    return n * jnp.dtype(dtype).itemsize


def _rms_norm_f32(x, g):
    y = x * lax.rsqrt(jnp.mean(x * x, axis=-1, keepdims=True) + NORM_EPS)
    return y * g


def _qk_scores(q, k):
    return lax.dot_general(q, k, (((1,), (1,)), ((), ())), preferred_element_type=F32)


def _pv_with_denominator(p, v):
    v_ext = jnp.concatenate([v, jnp.ones_like(v)], axis=1)
    pv = jnp.dot(p, v_ext, preferred_element_type=F32)
    return pv[:, :HEAD_DIM], pv[:, HEAD_DIM:]


def _norm_matmul_kernel(x_ref, g_ref, w_ref, o_ref, h_ref):
    @pl.when(pl.program_id(1) == 0)
    def _():
        h_ref[...] = _rms_norm_f32(x_ref[...], g_ref[...]).astype(h_ref.dtype)

    o_ref[...] = jnp.dot(h_ref[...], w_ref[...], preferred_element_type=F32).astype(o_ref.dtype)


def _norm_matmul(x, g, w, out_dtype, *, tm=1024, tn=1024, name):
    m, d = x.shape
    n = w.shape[1]
    assert m % tm == 0 and n % tn == 0
    blocks = _nbytes((tm, d), F32) + _nbytes((d, tn), BF16) + _nbytes((tm, tn), out_dtype)
    return pl.pallas_call(
        _norm_matmul_kernel,
        grid=(m // tm, n // tn),
        in_specs=[
            pl.BlockSpec((tm, d), lambda i, j: (i, 0)),
            pl.BlockSpec((1, d), lambda i, j: (0, 0)),
            pl.BlockSpec((d, tn), lambda i, j: (0, j)),
        ],
        out_specs=pl.BlockSpec((tm, tn), lambda i, j: (i, j)),
        out_shape=jax.ShapeDtypeStruct((m, n), out_dtype),
        scratch_shapes=[pltpu.VMEM((tm, d), BF16)],
        compiler_params=pltpu.CompilerParams(
            dimension_semantics=("parallel", "arbitrary"),
            vmem_limit_bytes=_vmem_limit(blocks, _nbytes((tm, d), BF16), _nbytes((tm, d), F32) + _nbytes((tm, tn), F32)),
        ),
        name=name,
    )(x, g.reshape(1, d), w)


def _proj_residual_kernel(a_ref, b_ref, wa_ref, wb_ref, x_ref, o_ref):
    acc = jnp.dot(a_ref[...], wa_ref[...], preferred_element_type=F32)
    acc = acc + jnp.dot(b_ref[...], wb_ref[...], preferred_element_type=F32)
    o_ref[...] = x_ref[...] + acc


def _proj_residual(a, a_col, b, b_col, w, x, *, tm=1024, tn=1024, name):
    m, d = x.shape
    kh = w.shape[0] // 2
    assert m % tm == 0 and d % tn == 0
    blocks = 2 * _nbytes((tm, kh), BF16) + 2 * _nbytes((kh, tn), BF16) + 2 * _nbytes((tm, tn), F32)
    return pl.pallas_call(
        _proj_residual_kernel,
        grid=(m // tm, d // tn),
        in_specs=[
            pl.BlockSpec((tm, kh), lambda i, j: (i, a_col)),
            pl.BlockSpec((tm, kh), lambda i, j: (i, b_col)),
            pl.BlockSpec((kh, tn), lambda i, j: (0, j)),
            pl.BlockSpec((kh, tn), lambda i, j: (1, j)),
            pl.BlockSpec((tm, tn), lambda i, j: (i, j)),
        ],
        out_specs=pl.BlockSpec((tm, tn), lambda i, j: (i, j)),
        out_shape=jax.ShapeDtypeStruct((m, d), F32),
        compiler_params=pltpu.CompilerParams(
            dimension_semantics=("parallel", "parallel"),
            vmem_limit_bytes=_vmem_limit(blocks, 0, 2 * _nbytes((tm, tn), F32)),
        ),
        name=name,
    )(a, b, w, w, x)


def _mlp_kernel(x_ref, g_ref, w1_ref, w2_ref, *rest, final_norm):
    gf_ref = rest[0] if final_norm else None
    o_ref, h_ref = rest[-2:]
    f = pl.program_id(1)

    @pl.when(f == 0)
    def _():
        x = x_ref[...]
        h_ref[...] = _rms_norm_f32(x, g_ref[...]).astype(h_ref.dtype)
        o_ref[...] = x

    a = jnp.maximum(jnp.dot(h_ref[...], w1_ref[...], preferred_element_type=F32), 0.0)
    o_ref[...] += jnp.dot((a * a).astype(BF16), w2_ref[...], preferred_element_type=F32)

    if final_norm:
        @pl.when(f == pl.num_programs(1) - 1)
        def _():
            o_ref[...] = _rms_norm_f32(o_ref[...], gf_ref[...])


def _mlp(x, g, w1, w2, layer, g_final=None, *, tm=512, tf=1024, name):
    m, d = x.shape
    ff = w1.shape[2]
    assert m % tm == 0 and ff % tf == 0
    final_norm = g_final is not None
    blocks = 2 * _nbytes((tm, d), F32) + _nbytes((d, tf), BF16) + _nbytes((tf, d), BF16)
    resident = _nbytes((tm, d), BF16)
    temps = _nbytes((tm, tf), F32) + _nbytes((tm, tf), BF16) + _nbytes((tm, d), F32)
    gain_spec = pl.BlockSpec((1, d), lambda i, f: (0, 0))
    extra_specs, extra_args = ([gain_spec], [g_final.reshape(1, d)]) if final_norm else ([], [])
    return pl.pallas_call(
        functools.partial(_mlp_kernel, final_norm=final_norm),
        grid=(m // tm, ff // tf),
        in_specs=[
            pl.BlockSpec((tm, d), lambda i, f: (i, 0)),
            gain_spec,
            pl.BlockSpec((None, d, tf), lambda i, f: (layer, 0, f)),
            pl.BlockSpec((None, tf, d), lambda i, f: (layer, f, 0)),
        ] + extra_specs,
        out_specs=pl.BlockSpec((tm, d), lambda i, f: (i, 0)),
        out_shape=jax.ShapeDtypeStruct((m, d), F32),
        scratch_shapes=[pltpu.VMEM((tm, d), BF16)],
        compiler_params=pltpu.CompilerParams(
            dimension_semantics=("parallel", "arbitrary"),
            vmem_limit_bytes=_vmem_limit(blocks, resident, temps),
        ),
        name=name,
    )(x, g.reshape(1, d), w1, w2, *extra_args)


NA_ROWS_PER_ITER = 16


def _na_bias_table(rpb):
    n_heads = rpb.shape[0]
    col = jnp.arange(GRID_W)
    col_start = jnp.clip(col - NA_COL_WIN // 2, 0, GRID_W - NA_COL_WIN)
    kc = jnp.arange(GRID_W)[None, :]
    valid = (kc >= col_start[:, None]) & (kc < col_start[:, None] + NA_COL_WIN)
    dc = kc - col[:, None] + (NA_COL_WIN - 1)
    onehot = (dc[:, :, None] == jnp.arange(2 * NA_COL_WIN - 1)[None, None, :]).astype(F32)
    by_col = jnp.einsum("hrd,ckd->hcrk", rpb.astype(F32) * LOG2E, onehot, precision=lax.Precision.HIGHEST)
    by_col = jnp.where(valid[None, :, None, :], by_col, NEG_INF)
    per_delta = [by_col[:, :, NA_ROW_WIN - 1 - dl: 2 * NA_ROW_WIN - 1 - dl].reshape(n_heads, GRID_W, -1)
                 for dl in range(NA_ROW_WIN)]
    return jnp.stack(per_delta, axis=1)


def _na_kernel(q_ref, k_ref, v_ref, bias_ref, o_ref, *, rows):
    span = NA_ROW_WIN * GRID_W

    def one_row(r):
        r0 = jnp.clip(r - NA_ROW_WIN // 2, 0, rows - NA_ROW_WIN)
        q_start = pl.multiple_of(r * GRID_W, GRID_W)
        k_start = pl.multiple_of(r0 * GRID_W, GRID_W)
        q = q_ref[pl.ds(q_start, GRID_W), :]
        k = k_ref[pl.ds(k_start, span), :]
        v = v_ref[pl.ds(k_start, span), :]
        s = _qk_scores(q, k) + bias_ref[r - r0]
        m = jnp.max(s, axis=-1, keepdims=True)
        p = jnp.exp2(s - m).astype(BF16)
        num, den = _pv_with_denominator(p, v)
        o_ref[pl.ds(q_start, GRID_W), :] = (num / den).astype(o_ref.dtype)

    def body(it, carry):
        for u in range(NA_ROWS_PER_ITER):
            one_row(it * NA_ROWS_PER_ITER + u)
        return carry

    lax.fori_loop(0, rows // NA_ROWS_PER_ITER, body, 0)


def _na_attention(proj, bias_tbl, n_heads, *, name):
    b, t, _ = proj.shape
    rows = t // GRID_W
    assert rows >= NA_ROW_WIN and rows % NA_ROWS_PER_ITER == 0
    span = NA_ROW_WIN * GRID_W
    head_blk = lambda off: pl.BlockSpec((None, t, HEAD_DIM), lambda bi, h: (bi, 0, off + h))
    blocks = 4 * _nbytes((t, HEAD_DIM), BF16) + _nbytes((NA_ROW_WIN, GRID_W, span), F32)
    return pl.pallas_call(
        functools.partial(_na_kernel, rows=rows),
        grid=(b, n_heads),
        in_specs=[
            head_blk(0), head_blk(n_heads), head_blk(2 * n_heads),
            pl.BlockSpec((None, NA_ROW_WIN, GRID_W, span), lambda bi, h: (h, 0, 0, 0)),
        ],
        out_specs=pl.BlockSpec((None, t, HEAD_DIM), lambda bi, h: (bi, 0, h)),
        out_shape=jax.ShapeDtypeStruct((b, t, n_heads * HEAD_DIM), BF16),
        compiler_params=pltpu.CompilerParams(
            dimension_semantics=("parallel", "parallel"),
            vmem_limit_bytes=_vmem_limit(blocks, 0, 8 << 20),
        ),
        name=name,
    )(proj, proj, proj, bias_tbl)


DIL_SUPER = 2048
DIL_QBLK = 128
DIL_BLOCKS_PER_ITER = 16
DIL_MERGE_ROWS = 256


def _band_bias(n_q, span, offset, radius, slope2):
    rel = (lax.broadcasted_iota(jnp.int32, (n_q, 1), 0)
           - lax.broadcasted_iota(jnp.int32, (1, span), 1)) + offset
    dist = jnp.abs(rel)
    return jnp.where(dist <= radius, -slope2 * dist.astype(F32), NEG_INF)


def _dilated_kernel(slope_ref, q_ref, k_ref, v_ref, o_ref, bias_ref, acc_ref, m_ref, l_ref, *, seq_len):
    h = pl.program_id(1)
    sb = pl.program_id(2)

    @pl.when(sb == 0)
    def _():
        slope2 = slope_ref[h] * LOG2E
        for br, (window, dil) in enumerate(DILATED_BRANCHES):
            radius = window // (2 * dil)
            for var in range(3):
                bias_ref[3 * br + var] = _band_bias(DIL_QBLK, DIL_QBLK + 2 * radius, var * radius, radius, slope2 * dil)

    for br, (window, dil) in enumerate(DILATED_BRANCHES):
        radius = window // (2 * dil)
        span = DIL_QBLK + 2 * radius
        sub = seq_len // dil
        blocks_per_res = DIL_SUPER // dil // DIL_QBLK

        def one_block(idx, br=br, dil=dil, radius=radius, span=span, sub=sub, blocks_per_res=blocks_per_res):
            res = idx // blocks_per_res
            c = idx % blocks_per_res
            q_row = res + dil * DIL_QBLK * c
            u0 = sb * (DIL_SUPER // dil) + DIL_QBLK * c
            ws = jnp.clip(u0 - radius, 0, sub - span)
            k_row = res + dil * ws
            if dil == 1:
                q_idx = pl.ds(q_row, DIL_QBLK)
                k_idx = pl.ds(k_row, span)
            else:
                q_idx = pl.ds(q_row, DIL_QBLK, stride=dil)
                k_idx = pl.ds(k_row, span, stride=dil)
            q = q_ref[q_idx, :].astype(BF16)
            k = k_ref[k_idx, :].astype(BF16)
            v = v_ref[k_idx, :].astype(BF16)
            s = _qk_scores(q, k) + bias_ref[3 * br + (u0 - ws) // radius]
            m = jnp.max(s, axis=-1, keepdims=True)
            p = jnp.exp2(s - m).astype(BF16)
            num, den = _pv_with_denominator(p, v)
            acc_ref[br, q_idx, :] = num
            l_ref[br, q_idx, :] = den
            m_ref[br, q_idx, :] = jnp.broadcast_to(m, (DIL_QBLK, HEAD_DIM))

        def body(it, carry, one_block=one_block):
            for u in range(DIL_BLOCKS_PER_ITER):
                one_block(it * DIL_BLOCKS_PER_ITER + u)
            return carry

        lax.fori_loop(0, DIL_SUPER // DIL_QBLK // DIL_BLOCKS_PER_ITER, body, 0)

    n_br = len(DILATED_BRANCHES)

    def merge(it, carry):
        rows = pl.ds(pl.multiple_of(it * DIL_MERGE_ROWS, DIL_MERGE_ROWS), DIL_MERGE_ROWS)
        ms = [m_ref[br, rows, :] for br in range(n_br)]
        m_all = functools.reduce(jnp.maximum, ms)
        ws = [jnp.exp2(m - m_all) for m in ms]
        num = sum(w * acc_ref[br, rows, :] for br, w in enumerate(ws))
        den = sum(w * l_ref[br, rows, :] for br, w in enumerate(ws))
        o_ref[rows, :] = (num / den).astype(o_ref.dtype)
        return carry

    lax.fori_loop(0, DIL_SUPER // DIL_MERGE_ROWS, merge, 0)


def _dilated_attention(proj, n_heads, *, name):
    b, t, _ = proj.shape
    n_br = len(DILATED_BRANCHES)
    spans = {DIL_QBLK + window // dil for window, dil in DILATED_BRANCHES}
    assert len(spans) == 1, "bias scratch assumes one key span for all branches"
    span = spans.pop()
    assert t % DIL_SUPER == 0 and DIL_SUPER // DIL_QBLK % DIL_BLOCKS_PER_ITER == 0 and DIL_SUPER % DIL_MERGE_ROWS == 0
    for window, dil in DILATED_BRANCHES:
        assert DIL_SUPER % (dil * DIL_QBLK) == 0 and t // dil >= span and window // (2 * dil) <= DIL_QBLK
    slopes = 2.0 ** (-8.0 * jnp.arange(1, n_heads + 1, dtype=F32) / n_heads)
    kv_blk = lambda off: pl.BlockSpec((None, t, HEAD_DIM), lambda bi, h, s: (bi, 0, off + h))
    blocks = (2 * _nbytes((t, HEAD_DIM), F32) + _nbytes((DIL_SUPER, HEAD_DIM), F32)
              + _nbytes((DIL_SUPER, HEAD_DIM), BF16))
    stats = pltpu.VMEM((n_br, DIL_SUPER, HEAD_DIM), F32)
    scratch = 3 * _nbytes((n_br, DIL_SUPER, HEAD_DIM), F32) + _nbytes((3 * n_br, DIL_QBLK, span), F32)
    return pl.pallas_call(
        functools.partial(_dilated_kernel, seq_len=t),
        grid=(b, n_heads, t // DIL_SUPER),
        in_specs=[
            pl.BlockSpec(memory_space=pltpu.SMEM),
            pl.BlockSpec((None, DIL_SUPER, HEAD_DIM), lambda bi, h, s: (bi, s, h)),
            kv_blk(n_heads), kv_blk(2 * n_heads),
        ],
        out_specs=pl.BlockSpec((None, DIL_SUPER, HEAD_DIM), lambda bi, h, s: (bi, s, h)),
        out_shape=jax.ShapeDtypeStruct((b, t, n_heads * HEAD_DIM), BF16),
        scratch_shapes=[pltpu.VMEM((3 * n_br, DIL_QBLK, span), F32), stats, stats, stats],
        compiler_params=pltpu.CompilerParams(
            dimension_semantics=("parallel", "parallel", "arbitrary"),
            vmem_limit_bytes=_vmem_limit(blocks, scratch, 8 << 20),
        ),
        name=name,
    )(slopes, proj, proj, proj)


SWA_QBLK = 256
SWA_BLOCKS_PER_STEP = 8


def _swa_kernel(slope_ref, sink_ref, q_ref, k_ref, v_ref, o_ref, bias_ref, *, seq_len, group):
    kvh = pl.program_id(1)
    span = SWA_QBLK + 2 * C_RADIUS

    @pl.when(pl.program_id(2) == 0)
    def _():
        for g in range(group):
            slope2 = slope_ref[kvh * group + g] * LOG2E
            for var in range(3):
                bias_ref[3 * g + var] = _band_bias(SWA_QBLK, span, var * C_RADIUS, C_RADIUS, slope2)

    for blk in range(SWA_BLOCKS_PER_STEP):
        rows = slice(blk * SWA_QBLK, (blk + 1) * SWA_QBLK)
        s0 = (pl.program_id(2) * SWA_BLOCKS_PER_STEP + blk) * SWA_QBLK
        ks = pl.multiple_of(jnp.clip(s0 - C_RADIUS, 0, seq_len - span), C_RADIUS)
        var = (s0 - ks) // C_RADIUS
        k = k_ref[pl.ds(ks, span), :]
        v = v_ref[pl.ds(ks, span), :]
        for g in range(group):
            cols = slice(g * HEAD_DIM, (g + 1) * HEAD_DIM)
            s = _qk_scores(q_ref[rows, cols], k) + bias_ref[3 * g + var]
            sink2 = sink_ref[kvh * group + g] * LOG2E
            m = jnp.maximum(jnp.max(s, axis=-1, keepdims=True), sink2)
            p = jnp.exp2(s - m).astype(BF16)
            num, den = _pv_with_denominator(p, v)
            o = num / (den + jnp.exp2(sink2 - m))
            o_ref[rows, cols] = o.astype(o_ref.dtype)


def _swa_attention(proj, sink, n_heads, n_kv, *, name):
    b, t, _ = proj.shape
    group = n_heads // n_kv
    span = SWA_QBLK + 2 * C_RADIUS
    step_rows = SWA_QBLK * SWA_BLOCKS_PER_STEP
    assert t % step_rows == 0 and t >= span and C_RADIUS <= SWA_QBLK and SWA_QBLK % C_RADIUS == 0
    slopes = 2.0 ** (-8.0 * jnp.arange(1, n_heads + 1, dtype=F32) / n_heads)
    kv_blk = lambda off: pl.BlockSpec((None, t, HEAD_DIM), lambda bi, kh, s: (bi, 0, off + kh))
    q_blk = pl.BlockSpec((None, step_rows, group * HEAD_DIM), lambda bi, kh, s: (bi, s, kh))
    blocks = 2 * _nbytes((t, HEAD_DIM), BF16) + 2 * _nbytes((step_rows, group * HEAD_DIM), BF16)
    bias_bytes = _nbytes((3 * group, SWA_QBLK, span), F32)
    return pl.pallas_call(
        functools.partial(_swa_kernel, seq_len=t, group=group),
        grid=(b, n_kv, t // step_rows),
        in_specs=[
            pl.BlockSpec(memory_space=pltpu.SMEM),
            pl.BlockSpec(memory_space=pltpu.SMEM),
            q_blk, kv_blk(n_heads), kv_blk(n_heads + n_kv),
        ],
        out_specs=q_blk,
        out_shape=jax.ShapeDtypeStruct((b, t, n_heads * HEAD_DIM), BF16),
        scratch_shapes=[pltpu.VMEM((3 * group, SWA_QBLK, span), F32)],
        compiler_params=pltpu.CompilerParams(
            dimension_semantics=("parallel", "parallel", "arbitrary"),
            vmem_limit_bytes=_vmem_limit(blocks, bias_bytes, 8 << 20),
        ),
        name=name,
    )(slopes, sink.astype(F32), proj, proj, proj)


CAST_BLOCK_BYTES = 4 << 20


def _cast_kernel(w_ref, o_ref):
    o_ref[...] = w_ref[...].astype(o_ref.dtype)


def _to_bf16(w, *, name):
    lead = w.shape[:-2]
    w2 = w.reshape(-1, w.shape[-1])
    rows, cols = w2.shape
    rb = min(rows, max(16, CAST_BLOCK_BYTES // (4 * cols)))
    assert rows % rb == 0
    out = pl.pallas_call(
        _cast_kernel,
        grid=(rows // rb,),
        in_specs=[pl.BlockSpec((rb, cols), lambda i: (i, 0))],
        out_specs=pl.BlockSpec((rb, cols), lambda i: (i, 0)),
        out_shape=jax.ShapeDtypeStruct((rows, cols), BF16),
        compiler_params=pltpu.CompilerParams(
            dimension_semantics=("parallel",),
            vmem_limit_bytes=_vmem_limit(_nbytes((rb, cols), F32) + _nbytes((rb, cols), BF16), 0, 0),
        ),
        name=name,
    )(w2)
    return out.reshape(*lead, *w.shape[-2:])


def _scaled_bf16(w, n_q_cols):
    col = jnp.arange(w.shape[1])
    scale = jnp.where(col < n_q_cols, LOG2E * HEAD_DIM ** -0.5, 1.0).astype(F32)
    return (w * scale[None, :]).astype(BF16)


def kernel(x, attn_norm, mlp_norm, w_mlp_in, w_mlp_out, even_w_in, even_rpb, even_w_out,
           odd_w_qkv, odd_sink, odd_w_out, final_norm):
    b, t, d = x.shape
    depth = attn_norm.shape[0]
    n_heads_a = even_rpb.shape[1]
    wa = n_heads_a * HEAD_DIM
    n_heads_b = (even_w_in.shape[2] - 3 * wa) // (3 * HEAD_DIM)
    n_heads_c = odd_sink.shape[1]
    n_kv_c = (odd_w_qkv.shape[2] // HEAD_DIM - n_heads_c) // 2
    xf = x.reshape(b * t, d)
    w1_bf16 = _to_bf16(w_mlp_in, name="cast_mlp_in")
    w2_bf16 = _to_bf16(w_mlp_out, name="cast_mlp_out")

    for i in range(depth):
        j = i // 2
        if i % 2 == 0:
            w_in = even_w_in[j]
            proj_a = _norm_matmul(xf, attn_norm[i], _scaled_bf16(w_in[:, :3 * wa], wa), BF16, name=f"l{i}_proj_a")
            proj_b = _norm_matmul(xf, attn_norm[i], _scaled_bf16(w_in[:, 3 * wa:], n_heads_b * HEAD_DIM), F32,
                                  name=f"l{i}_proj_b")
            o_a = _na_attention(proj_a.reshape(b, t, -1), _na_bias_table(even_rpb[j]), n_heads_a, name=f"l{i}_na")
            o_b = _dilated_attention(proj_b.reshape(b, t, -1), n_heads_b, name=f"l{i}_dilated")
            xf = _proj_residual(o_a.reshape(b * t, -1), 0, o_b.reshape(b * t, -1), 0,
                                even_w_out[j].astype(BF16), xf, name=f"l{i}_out")
        else:
            proj_c = _norm_matmul(xf, attn_norm[i], _scaled_bf16(odd_w_qkv[j], n_heads_c * HEAD_DIM), BF16,
                                  name=f"l{i}_proj_c")
            o_c = _swa_attention(proj_c.reshape(b, t, -1), odd_sink[j], n_heads_c, n_kv_c, name=f"l{i}_swa")
            o_c = o_c.reshape(b * t, -1)
            xf = _proj_residual(o_c, 0, o_c, 1, odd_w_out[j].astype(BF16), xf, name=f"l{i}_out")
        g_final = final_norm if i == depth - 1 else None
        xf = _mlp(xf, mlp_norm[i], w1_bf16, w2_bf16, i, g_final, name=f"l{i}_mlp")
    return xf.reshape(b, t, d)
```

```python
import functools
import math

import jax
import jax.numpy as jnp
from jax import lax
from jax.experimental import pallas as pl
from jax.experimental.pallas import tpu as pltpu

HEAD_DIM = 128
GRID_W = 64
NA_ROW_WIN = 8
NA_COL_WIN = 16
DILATED_BRANCHES = ((128, 1), (512, 4), (2048, 16))
C_RADIUS = 128
NORM_EPS = 1e-6
NEG_INF = -1e30
LOG2E = math.log2(math.e)

V7X_VMEM_LIMIT_CAP = 56 * 1024 * 1024
BF16 = jnp.bfloat16
F32 = jnp.float32


def _vmem_limit(block_bytes, scratch_bytes, temp_bytes):
    need = 2 * block_bytes + scratch_bytes + temp_bytes + (2 << 20)
    return int(min(max(need, 16 << 20), V7X_VMEM_LIMIT_CAP))


def _nbytes(shape, dtype):
    n = 1
    for s in shape:
        n *= s
    return n * jnp.dtype(dtype).itemsize


def _rms_norm_f32(x, g):
    y = x * lax.rsqrt(jnp.mean(x * x, axis=-1, keepdims=True) + NORM_EPS)
    return y * g


def _qk_scores(q, k):
    return lax.dot_general(q, k, (((1,), (1,)), ((), ())), preferred_element_type=F32)


def _pv_with_denominator(p, v):
    v_ext = jnp.concatenate([v, jnp.ones_like(v)], axis=1)
    pv = jnp.dot(p, v_ext, preferred_element_type=F32)
    return pv[:, :HEAD_DIM], pv[:, HEAD_DIM:]


def _norm_matmul_kernel(x_ref, g_ref, w_ref, o_ref, h_ref):
    @pl.when(pl.program_id(1) == 0)
    def _():
        h_ref[...] = _rms_norm_f32(x_ref[...], g_ref[...]).astype(h_ref.dtype)

    o_ref[...] = jnp.dot(h_ref[...], w_ref[...], preferred_element_type=F32).astype(o_ref.dtype)


def _norm_matmul(x, g, w, out_dtype, *, tm=1024, tn=1024, name):
    m, d = x.shape
    n = w.shape[1]
    assert m % tm == 0 and n % tn == 0
    blocks = _nbytes((tm, d), F32) + _nbytes((d, tn), BF16) + _nbytes((tm, tn), out_dtype)
    return pl.pallas_call(
        _norm_matmul_kernel,
        grid=(m // tm, n // tn),
        in_specs=[
            pl.BlockSpec((tm, d), lambda i, j: (i, 0)),
            pl.BlockSpec((1, d), lambda i, j: (0, 0)),
            pl.BlockSpec((d, tn), lambda i, j: (0, j)),
        ],
        out_specs=pl.BlockSpec((tm, tn), lambda i, j: (i, j)),
        out_shape=jax.ShapeDtypeStruct((m, n), out_dtype),
        scratch_shapes=[pltpu.VMEM((tm, d), BF16)],
        compiler_params=pltpu.CompilerParams(
            dimension_semantics=("parallel", "arbitrary"),
            vmem_limit_bytes=_vmem_limit(blocks, _nbytes((tm, d), BF16), _nbytes((tm, d), F32) + _nbytes((tm, tn), F32)),
        ),
        name=name,
    )(x, g.reshape(1, d), w)


def _proj_residual_kernel(a_ref, b_ref, wa_ref, wb_ref, x_ref, o_ref):
    acc = jnp.dot(a_ref[...], wa_ref[...], preferred_element_type=F32)
    acc = acc + jnp.dot(b_ref[...], wb_ref[...], preferred_element_type=F32)
    o_ref[...] = x_ref[...] + acc


def _proj_residual(a, a_col, b, b_col, w, x, *, tm=1024, tn=1024, name):
    m, d = x.shape
    kh = w.shape[0] // 2
    assert m % tm == 0 and d % tn == 0
    blocks = 2 * _nbytes((tm, kh), BF16) + 2 * _nbytes((kh, tn), BF16) + 2 * _nbytes((tm, tn), F32)
    return pl.pallas_call(
        _proj_residual_kernel,
        grid=(m // tm, d // tn),
        in_specs=[
            pl.BlockSpec((tm, kh), lambda i, j: (i, a_col)),
            pl.BlockSpec((tm, kh), lambda i, j: (i, b_col)),
            pl.BlockSpec((kh, tn), lambda i, j: (0, j)),
            pl.BlockSpec((kh, tn), lambda i, j: (1, j)),
            pl.BlockSpec((tm, tn), lambda i, j: (i, j)),
        ],
        out_specs=pl.BlockSpec((tm, tn), lambda i, j: (i, j)),
        out_shape=jax.ShapeDtypeStruct((m, d), F32),
        compiler_params=pltpu.CompilerParams(
            dimension_semantics=("parallel", "parallel"),
            vmem_limit_bytes=_vmem_limit(blocks, 0, 2 * _nbytes((tm, tn), F32)),
        ),
        name=name,
    )(a, b, w, w, x)


def _mlp_kernel(x_ref, g_ref, w1_ref, w2_ref, *rest, final_norm):
    gf_ref = rest[0] if final_norm else None
    o_ref, h_ref = rest[-2:]
    f = pl.program_id(1)

    @pl.when(f == 0)
    def _():
        x = x_ref[...]
        h_ref[...] = _rms_norm_f32(x, g_ref[...]).astype(h_ref.dtype)
        o_ref[...] = x

    a = jnp.maximum(jnp.dot(h_ref[...], w1_ref[...], preferred_element_type=F32), 0.0)
    o_ref[...] += jnp.dot((a * a).astype(BF16), w2_ref[...], preferred_element_type=F32)

    if final_norm:
        @pl.when(f == pl.num_programs(1) - 1)
        def _():
            o_ref[...] = _rms_norm_f32(o_ref[...], gf_ref[...])


def _mlp(x, g, w1, w2, layer, g_final=None, *, tm=512, tf=1024, name):
    m, d = x.shape
    ff = w1.shape[2]
    assert m % tm == 0 and ff % tf == 0
    final_norm = g_final is not None
    blocks = 2 * _nbytes((tm, d), F32) + _nbytes((d, tf), BF16) + _nbytes((tf, d), BF16)
    resident = _nbytes((tm, d), BF16)
    temps = _nbytes((tm, tf), F32) + _nbytes((tm, tf), BF16) + _nbytes((tm, d), F32)
    gain_spec = pl.BlockSpec((1, d), lambda i, f: (0, 0))
    extra_specs, extra_args = ([gain_spec], [g_final.reshape(1, d)]) if final_norm else ([], [])
    return pl.pallas_call(
        functools.partial(_mlp_kernel, final_norm=final_norm),
        grid=(m // tm, ff // tf),
        in_specs=[
            pl.BlockSpec((tm, d), lambda i, f: (i, 0)),
            gain_spec,
            pl.BlockSpec((None, d, tf), lambda i, f: (layer, 0, f)),
            pl.BlockSpec((None, tf, d), lambda i, f: (layer, f, 0)),
        ] + extra_specs,
        out_specs=pl.BlockSpec((tm, d), lambda i, f: (i, 0)),
        out_shape=jax.ShapeDtypeStruct((m, d), F32),
        scratch_shapes=[pltpu.VMEM((tm, d), BF16)],
        compiler_params=pltpu.CompilerParams(
            dimension_semantics=("parallel", "arbitrary"),
            vmem_limit_bytes=_vmem_limit(blocks, resident, temps),
        ),
        name=name,
    )(x, g.reshape(1, d), w1, w2, *extra_args)


def _side_cast_specs(w, n_steps, step_index):
    rows, cols = w.shape
    assert rows % n_steps == 0 and (rows // n_steps) % 16 == 0
    rb = rows // n_steps
    index_map = lambda *ids: (step_index(*ids), 0)
    spec = pl.BlockSpec((rb, cols), index_map)
    return spec, spec, jax.ShapeDtypeStruct((rows, cols), BF16), _nbytes((rb, cols), F32) + _nbytes((rb, cols), BF16)


NA_ROWS_PER_ITER = 16


def _na_bias_table(rpb):
    n_heads = rpb.shape[0]
    col = jnp.arange(GRID_W)
    col_start = jnp.clip(col - NA_COL_WIN // 2, 0, GRID_W - NA_COL_WIN)
    kc = jnp.arange(GRID_W)[None, :]
    valid = (kc >= col_start[:, None]) & (kc < col_start[:, None] + NA_COL_WIN)
    dc = kc - col[:, None] + (NA_COL_WIN - 1)
    onehot = (dc[:, :, None] == jnp.arange(2 * NA_COL_WIN - 1)[None, None, :]).astype(F32)
    by_col = jnp.einsum("hrd,ckd->hcrk", rpb.astype(F32) * LOG2E, onehot, precision=lax.Precision.HIGHEST)
    by_col = jnp.where(valid[None, :, None, :], by_col, NEG_INF)
    per_delta = [by_col[:, :, NA_ROW_WIN - 1 - dl: 2 * NA_ROW_WIN - 1 - dl].reshape(n_heads, GRID_W, -1)
                 for dl in range(NA_ROW_WIN)]
    return jnp.stack(per_delta, axis=1)


def _na_kernel(q_ref, k_ref, v_ref, bias_ref, w_ref, o_ref, wo_ref, *, rows):
    span = NA_ROW_WIN * GRID_W
    wo_ref[...] = w_ref[...].astype(wo_ref.dtype)

    def one_row(r):
        r0 = jnp.clip(r - NA_ROW_WIN // 2, 0, rows - NA_ROW_WIN)
        q_start = pl.multiple_of(r * GRID_W, GRID_W)
        k_start = pl.multiple_of(r0 * GRID_W, GRID_W)
        q = q_ref[pl.ds(q_start, GRID_W), :]
        k = k_ref[pl.ds(k_start, span), :]
        v = v_ref[pl.ds(k_start, span), :]
        s = _qk_scores(q, k) + bias_ref[r - r0]
        m = jnp.max(s, axis=-1, keepdims=True)
        p = jnp.exp2(s - m).astype(BF16)
        num, den = _pv_with_denominator(p, v)
        o_ref[pl.ds(q_start, GRID_W), :] = (num / den).astype(o_ref.dtype)

    def body(it, carry):
        for u in range(NA_ROWS_PER_ITER):
            one_row(it * NA_ROWS_PER_ITER + u)
        return carry

    lax.fori_loop(0, rows // NA_ROWS_PER_ITER, body, 0)


def _na_attention(proj, bias_tbl, n_heads, side_w, *, name):
    b, t, _ = proj.shape
    side_in, side_out, side_shape, side_bytes = _side_cast_specs(side_w, b * n_heads, lambda bi, h: bi * n_heads + h)
    rows = t // GRID_W
    assert rows >= NA_ROW_WIN and rows % NA_ROWS_PER_ITER == 0
    span = NA_ROW_WIN * GRID_W
    head_blk = lambda off: pl.BlockSpec((None, t, HEAD_DIM), lambda bi, h: (bi, 0, off + h))
    blocks = 4 * _nbytes((t, HEAD_DIM), BF16) + _nbytes((NA_ROW_WIN, GRID_W, span), F32) + side_bytes
    return pl.pallas_call(
        functools.partial(_na_kernel, rows=rows),
        grid=(b, n_heads),
        in_specs=[
            head_blk(0), head_blk(n_heads), head_blk(2 * n_heads),
            pl.BlockSpec((None, NA_ROW_WIN, GRID_W, span), lambda bi, h: (h, 0, 0, 0)),
            side_in,
        ],
        out_specs=[pl.BlockSpec((None, t, HEAD_DIM), lambda bi, h: (bi, 0, h)), side_out],
        out_shape=[jax.ShapeDtypeStruct((b, t, n_heads * HEAD_DIM), BF16), side_shape],
        compiler_params=pltpu.CompilerParams(
            dimension_semantics=("parallel", "parallel"),
            vmem_limit_bytes=_vmem_limit(blocks, 0, 8 << 20),
        ),
        name=name,
    )(proj, proj, proj, bias_tbl, side_w)


DIL_SUPER = 2048
DIL_QBLK = 128
DIL_BLOCKS_PER_ITER = 16
DIL_MERGE_ROWS = 256


def _band_bias(n_q, span, offset, radius, slope2):
    rel = (lax.broadcasted_iota(jnp.int32, (n_q, 1), 0)
           - lax.broadcasted_iota(jnp.int32, (1, span), 1)) + offset
    dist = jnp.abs(rel)
    return jnp.where(dist <= radius, -slope2 * dist.astype(F32), NEG_INF)


def _dilated_kernel(slope_ref, q_ref, k_ref, v_ref, w_ref, o_ref, wo_ref, bias_ref, acc_ref, m_ref, l_ref, *,
                    seq_len):
    h = pl.program_id(1)
    sb = pl.program_id(2)
    wo_ref[...] = w_ref[...].astype(wo_ref.dtype)

    @pl.when(sb == 0)
    def _():
        slope2 = slope_ref[h] * LOG2E
        for br, (window, dil) in enumerate(DILATED_BRANCHES):
            radius = window // (2 * dil)
            for var in range(3):
                bias_ref[3 * br + var] = _band_bias(DIL_QBLK, DIL_QBLK + 2 * radius, var * radius, radius, slope2 * dil)

    for br, (window, dil) in enumerate(DILATED_BRANCHES):
        radius = window // (2 * dil)
        span = DIL_QBLK + 2 * radius
        sub = seq_len // dil
        blocks_per_res = DIL_SUPER // dil // DIL_QBLK

        def one_block(idx, br=br, dil=dil, radius=radius, span=span, sub=sub, blocks_per_res=blocks_per_res):
            res = idx // blocks_per_res
            c = idx % blocks_per_res
            q_row = res + dil * DIL_QBLK * c
            u0 = sb * (DIL_SUPER // dil) + DIL_QBLK * c
            ws = jnp.clip(u0 - radius, 0, sub - span)
            k_row = res + dil * ws
            if dil == 1:
                q_idx = pl.ds(q_row, DIL_QBLK)
                k_idx = pl.ds(k_row, span)
            else:
                q_idx = pl.ds(q_row, DIL_QBLK, stride=dil)
                k_idx = pl.ds(k_row, span, stride=dil)
            q = q_ref[q_idx, :].astype(BF16)
            k = k_ref[k_idx, :].astype(BF16)
            v = v_ref[k_idx, :].astype(BF16)
            s = _qk_scores(q, k) + bias_ref[3 * br + (u0 - ws) // radius]
            m = jnp.max(s, axis=-1, keepdims=True)
            p = jnp.exp2(s - m).astype(BF16)
            num, den = _pv_with_denominator(p, v)
            acc_ref[br, q_idx, :] = num
            l_ref[br, q_idx, :] = den
            m_ref[br, q_idx, :] = jnp.broadcast_to(m, (DIL_QBLK, HEAD_DIM))

        def body(it, carry, one_block=one_block):
            for u in range(DIL_BLOCKS_PER_ITER):
                one_block(it * DIL_BLOCKS_PER_ITER + u)
            return carry

        lax.fori_loop(0, DIL_SUPER // DIL_QBLK // DIL_BLOCKS_PER_ITER, body, 0)

    n_br = len(DILATED_BRANCHES)

    def merge(it, carry):
        rows = pl.ds(pl.multiple_of(it * DIL_MERGE_ROWS, DIL_MERGE_ROWS), DIL_MERGE_ROWS)
        ms = [m_ref[br, rows, :] for br in range(n_br)]
        m_all = functools.reduce(jnp.maximum, ms)
        ws = [jnp.exp2(m - m_all) for m in ms]
        num = sum(w * acc_ref[br, rows, :] for br, w in enumerate(ws))
        den = sum(w * l_ref[br, rows, :] for br, w in enumerate(ws))
        o_ref[rows, :] = (num / den).astype(o_ref.dtype)
        return carry

    lax.fori_loop(0, DIL_SUPER // DIL_MERGE_ROWS, merge, 0)


def _dilated_attention(proj, n_heads, side_w, *, name):
    b, t, _ = proj.shape
    n_sb = t // DIL_SUPER
    side_in, side_out, side_shape, side_bytes = _side_cast_specs(
        side_w, b * n_heads * n_sb, lambda bi, h, s: (bi * n_heads + h) * n_sb + s)
    n_br = len(DILATED_BRANCHES)
    spans = {DIL_QBLK + window // dil for window, dil in DILATED_BRANCHES}
    assert len(spans) == 1, "bias scratch assumes one key span for all branches"
    span = spans.pop()
    assert t % DIL_SUPER == 0 and DIL_SUPER // DIL_QBLK % DIL_BLOCKS_PER_ITER == 0 and DIL_SUPER % DIL_MERGE_ROWS == 0
    for window, dil in DILATED_BRANCHES:
        assert DIL_SUPER % (dil * DIL_QBLK) == 0 and t // dil >= span and window // (2 * dil) <= DIL_QBLK
    slopes = 2.0 ** (-8.0 * jnp.arange(1, n_heads + 1, dtype=F32) / n_heads)
    kv_blk = lambda off: pl.BlockSpec((None, t, HEAD_DIM), lambda bi, h, s: (bi, 0, off + h))
    blocks = (2 * _nbytes((t, HEAD_DIM), F32) + _nbytes((DIL_SUPER, HEAD_DIM), F32)
              + _nbytes((DIL_SUPER, HEAD_DIM), BF16) + side_bytes)
    stats = pltpu.VMEM((n_br, DIL_SUPER, HEAD_DIM), F32)
    scratch = 3 * _nbytes((n_br, DIL_SUPER, HEAD_DIM), F32) + _nbytes((3 * n_br, DIL_QBLK, span), F32)
    return pl.pallas_call(
        functools.partial(_dilated_kernel, seq_len=t),
        grid=(b, n_heads, t // DIL_SUPER),
        in_specs=[
            pl.BlockSpec(memory_space=pltpu.SMEM),
            pl.BlockSpec((None, DIL_SUPER, HEAD_DIM), lambda bi, h, s: (bi, s, h)),
            kv_blk(n_heads), kv_blk(2 * n_heads), side_in,
        ],
        out_specs=[pl.BlockSpec((None, DIL_SUPER, HEAD_DIM), lambda bi, h, s: (bi, s, h)), side_out],
        out_shape=[jax.ShapeDtypeStruct((b, t, n_heads * HEAD_DIM), BF16), side_shape],
        scratch_shapes=[pltpu.VMEM((3 * n_br, DIL_QBLK, span), F32), stats, stats, stats],
        compiler_params=pltpu.CompilerParams(
            dimension_semantics=("parallel", "parallel", "arbitrary"),
            vmem_limit_bytes=_vmem_limit(blocks, scratch, 8 << 20),
        ),
        name=name,
    )(slopes, proj, proj, proj, side_w)


SWA_QBLK = 256
SWA_BLOCKS_PER_STEP = 8


def _swa_kernel(slope_ref, sink_ref, q_ref, k_ref, v_ref, o_ref, bias_ref, *, seq_len, group):
    kvh = pl.program_id(1)
    span = SWA_QBLK + 2 * C_RADIUS

    @pl.when(pl.program_id(2) == 0)
    def _():
        for g in range(group):
            slope2 = slope_ref[kvh * group + g] * LOG2E
            for var in range(3):
                bias_ref[3 * g + var] = _band_bias(SWA_QBLK, span, var * C_RADIUS, C_RADIUS, slope2)

    for blk in range(SWA_BLOCKS_PER_STEP):
        rows = slice(blk * SWA_QBLK, (blk + 1) * SWA_QBLK)
        s0 = (pl.program_id(2) * SWA_BLOCKS_PER_STEP + blk) * SWA_QBLK
        ks = pl.multiple_of(jnp.clip(s0 - C_RADIUS, 0, seq_len - span), C_RADIUS)
        var = (s0 - ks) // C_RADIUS
        k = k_ref[pl.ds(ks, span), :]
        v = v_ref[pl.ds(ks, span), :]
        for g in range(group):
            cols = slice(g * HEAD_DIM, (g + 1) * HEAD_DIM)
            s = _qk_scores(q_ref[rows, cols], k) + bias_ref[3 * g + var]
            sink2 = sink_ref[kvh * group + g] * LOG2E
            m = jnp.maximum(jnp.max(s, axis=-1, keepdims=True), sink2)
            p = jnp.exp2(s - m).astype(BF16)
            num, den = _pv_with_denominator(p, v)
            o = num / (den + jnp.exp2(sink2 - m))
            o_ref[rows, cols] = o.astype(o_ref.dtype)


def _swa_attention(proj, sink, n_heads, n_kv, *, name):
    b, t, _ = proj.shape
    group = n_heads // n_kv
    span = SWA_QBLK + 2 * C_RADIUS
    step_rows = SWA_QBLK * SWA_BLOCKS_PER_STEP
    assert t % step_rows == 0 and t >= span and C_RADIUS <= SWA_QBLK and SWA_QBLK % C_RADIUS == 0
    slopes = 2.0 ** (-8.0 * jnp.arange(1, n_heads + 1, dtype=F32) / n_heads)
    kv_blk = lambda off: pl.BlockSpec((None, t, HEAD_DIM), lambda bi, kh, s: (bi, 0, off + kh))
    q_blk = pl.BlockSpec((None, step_rows, group * HEAD_DIM), lambda bi, kh, s: (bi, s, kh))
    blocks = 2 * _nbytes((t, HEAD_DIM), BF16) + 2 * _nbytes((step_rows, group * HEAD_DIM), BF16)
    bias_bytes = _nbytes((3 * group, SWA_QBLK, span), F32)
    return pl.pallas_call(
        functools.partial(_swa_kernel, seq_len=t, group=group),
        grid=(b, n_kv, t // step_rows),
        in_specs=[
            pl.BlockSpec(memory_space=pltpu.SMEM),
            pl.BlockSpec(memory_space=pltpu.SMEM),
            q_blk, kv_blk(n_heads), kv_blk(n_heads + n_kv),
        ],
        out_specs=q_blk,
        out_shape=jax.ShapeDtypeStruct((b, t, n_heads * HEAD_DIM), BF16),
        scratch_shapes=[pltpu.VMEM((3 * group, SWA_QBLK, span), F32)],
        compiler_params=pltpu.CompilerParams(
            dimension_semantics=("parallel", "parallel", "arbitrary"),
            vmem_limit_bytes=_vmem_limit(blocks, bias_bytes, 8 << 20),
        ),
        name=name,
    )(slopes, sink.astype(F32), proj, proj, proj)


def _scaled_bf16(w, n_q_cols):
    col = jnp.arange(w.shape[1])
    scale = jnp.where(col < n_q_cols, LOG2E * HEAD_DIM ** -0.5, 1.0).astype(F32)
    return (w * scale[None, :]).astype(BF16)


def kernel(x, attn_norm, mlp_norm, w_mlp_in, w_mlp_out, even_w_in, even_rpb, even_w_out,
           odd_w_qkv, odd_sink, odd_w_out, final_norm):
    b, t, d = x.shape
    depth = attn_norm.shape[0]
    n_heads_a = even_rpb.shape[1]
    wa = n_heads_a * HEAD_DIM
    n_heads_b = (even_w_in.shape[2] - 3 * wa) // (3 * HEAD_DIM)
    n_heads_c = odd_sink.shape[1]
    n_kv_c = (odd_w_qkv.shape[2] // HEAD_DIM - n_heads_c) // 2
    xf = x.reshape(b * t, d)
    w1_bf16 = w2_bf16 = None

    for i in range(depth):
        j = i // 2
        if i % 2 == 0:
            w_in = even_w_in[j]
            proj_a = _norm_matmul(xf, attn_norm[i], _scaled_bf16(w_in[:, :3 * wa], wa), BF16, name=f"l{i}_proj_a")
            proj_b = _norm_matmul(xf, attn_norm[i], _scaled_bf16(w_in[:, 3 * wa:], n_heads_b * HEAD_DIM), F32,
                                  name=f"l{i}_proj_b")
            w1_pair, w2_pair = w_mlp_in[i:i + 2], w_mlp_out[i:i + 2]
            o_a, w1_bf16 = _na_attention(proj_a.reshape(b, t, -1), _na_bias_table(even_rpb[j]), n_heads_a,
                                         w1_pair.reshape(-1, w1_pair.shape[-1]), name=f"l{i}_na")
            o_b, w2_bf16 = _dilated_attention(proj_b.reshape(b, t, -1), n_heads_b,
                                              w2_pair.reshape(-1, w2_pair.shape[-1]), name=f"l{i}_dilated")
            w1_bf16, w2_bf16 = w1_bf16.reshape(w1_pair.shape), w2_bf16.reshape(w2_pair.shape)
            xf = _proj_residual(o_a.reshape(b * t, -1), 0, o_b.reshape(b * t, -1), 0,
                                even_w_out[j].astype(BF16), xf, name=f"l{i}_out")
        else:
            proj_c = _norm_matmul(xf, attn_norm[i], _scaled_bf16(odd_w_qkv[j], n_heads_c * HEAD_DIM), BF16,
                                  name=f"l{i}_proj_c")
            o_c = _swa_attention(proj_c.reshape(b, t, -1), odd_sink[j], n_heads_c, n_kv_c, name=f"l{i}_swa")
            o_c = o_c.reshape(b * t, -1)
            xf = _proj_residual(o_c, 0, o_c, 1, odd_w_out[j].astype(BF16), xf, name=f"l{i}_out")
        g_final = final_norm if i == depth - 1 else None
        xf = _mlp(xf, mlp_norm[i], w1_bf16, w2_bf16, i % 2, g_final, name=f"l{i}_mlp")
    return xf.reshape(b, t, d)
```

```python
import functools
import math

import jax
import jax.numpy as jnp
from jax import lax
from jax.experimental import pallas as pl
from jax.experimental.pallas import tpu as pltpu

HEAD_DIM = 128
GRID_W = 64
NA_ROW_WIN = 8
NA_COL_WIN = 16
DILATED_BRANCHES = ((128, 1), (512, 4), (2048, 16))
C_RADIUS = 128
NORM_EPS = 1e-6
NEG_INF = -1e30
LOG2E = math.log2(math.e)

V7X_VMEM_LIMIT_CAP = 56 * 1024 * 1024
BF16 = jnp.bfloat16
F32 = jnp.float32


def _vmem_limit(block_bytes, scratch_bytes, temp_bytes):
    need = 2 * block_bytes + scratch_bytes + temp_bytes + (2 << 20)
    return int(min(max(need, 16 << 20), V7X_VMEM_LIMIT_CAP))


def _nbytes(shape, dtype):
    n = 1
    for s in shape:
        n *= s
    return n * jnp.dtype(dtype).itemsize


def _rms_norm_f32(x, g):
    y = x * lax.rsqrt(jnp.mean(x * x, axis=-1, keepdims=True) + NORM_EPS)
    return y * g


def _qk_scores(q, k):
    return lax.dot_general(q, k, (((1,), (1,)), ((), ())), preferred_element_type=F32)


def _pv_with_denominator(p, v):
    v_ext = jnp.concatenate([v, jnp.ones_like(v)], axis=1)
    pv = jnp.dot(p, v_ext, preferred_element_type=F32)
    return pv[:, :HEAD_DIM], pv[:, HEAD_DIM:]


def _norm_matmul_kernel(x0_ref, xn_ref, g_ref, w_ref, o_ref, ha_ref, hb_ref, *, chunk):
    i = pl.program_id(0)
    j = pl.program_id(1)

    @pl.when((i == 0) & (j == 0))
    def _():
        ha_ref[...] = _rms_norm_f32(x0_ref[...], g_ref[...]).astype(ha_ref.dtype)

    rows = pl.ds(pl.multiple_of(j * chunk, chunk), chunk)

    def step(h_cur, h_next):
        h_next[rows, :] = _rms_norm_f32(xn_ref[rows, :], g_ref[...]).astype(h_next.dtype)
        o_ref[...] = jnp.dot(h_cur[...], w_ref[...], preferred_element_type=F32).astype(o_ref.dtype)

    @pl.when(i % 2 == 0)
    def _():
        step(ha_ref, hb_ref)

    @pl.when(i % 2 == 1)
    def _():
        step(hb_ref, ha_ref)


def _norm_matmul(x, g, w, out_dtype, *, tm=1024, tn=768, name):
    m, d = x.shape
    n = w.shape[1]
    n_i, n_j = m // tm, n // tn
    chunk = tm // n_j
    assert m % tm == 0 and n % tn == 0 and tm % n_j == 0 and chunk % 16 == 0
    blocks = _nbytes((tm, d), F32) + _nbytes((d, tn), BF16) + _nbytes((tm, tn), out_dtype)
    resident = _nbytes((tm, d), F32) + 2 * _nbytes((tm, d), BF16)
    return pl.pallas_call(
        functools.partial(_norm_matmul_kernel, chunk=chunk),
        grid=(n_i, n_j),
        in_specs=[
            pl.BlockSpec((tm, d), lambda i, j: (0, 0), pipeline_mode=pl.Buffered(1)),
            pl.BlockSpec((tm, d), lambda i, j: (jnp.minimum(i + 1, n_i - 1), 0)),
            pl.BlockSpec((1, d), lambda i, j: (0, 0)),
            pl.BlockSpec((d, tn), lambda i, j: (0, j)),
        ],
        out_specs=pl.BlockSpec((tm, tn), lambda i, j: (i, j)),
        out_shape=jax.ShapeDtypeStruct((m, n), out_dtype),
        scratch_shapes=[pltpu.VMEM((tm, d), BF16), pltpu.VMEM((tm, d), BF16)],
        compiler_params=pltpu.CompilerParams(
            dimension_semantics=("arbitrary", "arbitrary"),
            vmem_limit_bytes=_vmem_limit(blocks, resident, _nbytes((chunk, d), F32) + _nbytes((tm, tn), F32)),
        ),
        name=name,
    )(x, x, g.reshape(1, d), w)


def _proj_residual_kernel(a_ref, b_ref, wa_ref, wb_ref, x_ref, o_ref):
    acc = jnp.dot(a_ref[...], wa_ref[...], preferred_element_type=F32)
    acc = acc + jnp.dot(b_ref[...], wb_ref[...], preferred_element_type=F32)
    o_ref[...] = x_ref[...] + acc


def _proj_residual(a, a_col, b, b_col, w, x, *, tm=1024, tn=1024, name):
    m, d = x.shape
    kh = w.shape[0] // 2
    assert m % tm == 0 and d % tn == 0
    blocks = 2 * _nbytes((tm, kh), BF16) + 2 * _nbytes((kh, tn), BF16) + 2 * _nbytes((tm, tn), F32)
    return pl.pallas_call(
        _proj_residual_kernel,
        grid=(m // tm, d // tn),
        in_specs=[
            pl.BlockSpec((tm, kh), lambda i, j: (i, a_col)),
            pl.BlockSpec((tm, kh), lambda i, j: (i, b_col)),
            pl.BlockSpec((kh, tn), lambda i, j: (0, j)),
            pl.BlockSpec((kh, tn), lambda i, j: (1, j)),
            pl.BlockSpec((tm, tn), lambda i, j: (i, j)),
        ],
        out_specs=pl.BlockSpec((tm, tn), lambda i, j: (i, j)),
        out_shape=jax.ShapeDtypeStruct((m, d), F32),
        compiler_params=pltpu.CompilerParams(
            dimension_semantics=("parallel", "parallel"),
            vmem_limit_bytes=_vmem_limit(blocks, 0, 2 * _nbytes((tm, tn), F32)),
        ),
        name=name,
    )(a, b, w, w, x)


def _mlp_kernel(x_ref, xn_ref, g_ref, w1_ref, w2_ref, *rest, final_norm, chunk):
    gf_ref = rest[0] if final_norm else None
    o_ref, ha_ref, hb_ref = rest[-3:]
    i = pl.program_id(0)
    f = pl.program_id(1)

    @pl.when(f == 0)
    def _():
        o_ref[...] = x_ref[...]

    @pl.when((i == 0) & (f == 0))
    def _():
        ha_ref[...] = _rms_norm_f32(x_ref[...], g_ref[...]).astype(ha_ref.dtype)

    rows = pl.ds(pl.multiple_of(f * chunk, chunk), chunk)

    def step(h_cur, h_next):
        h_next[rows, :] = _rms_norm_f32(xn_ref[rows, :], g_ref[...]).astype(h_next.dtype)
        a = jnp.maximum(jnp.dot(h_cur[...], w1_ref[...], preferred_element_type=F32), 0.0)
        o_ref[...] += jnp.dot((a * a).astype(BF16), w2_ref[...], preferred_element_type=F32)

    @pl.when(i % 2 == 0)
    def _():
        step(ha_ref, hb_ref)

    @pl.when(i % 2 == 1)
    def _():
        step(hb_ref, ha_ref)

    if final_norm:
        @pl.when(f == pl.num_programs(1) - 1)
        def _():
            o_ref[...] = _rms_norm_f32(o_ref[...], gf_ref[...])


def _mlp(x, g, w1, w2, layer, g_final=None, *, tm=512, tf=1024, name):
    m, d = x.shape
    ff = w1.shape[2]
    n_i, n_f = m // tm, ff // tf
    chunk = tm // n_f
    assert m % tm == 0 and ff % tf == 0 and tm % n_f == 0 and chunk % 16 == 0
    final_norm = g_final is not None
    blocks = 3 * _nbytes((tm, d), F32) + _nbytes((d, tf), BF16) + _nbytes((tf, d), BF16)
    resident = 2 * _nbytes((tm, d), BF16)
    temps = _nbytes((tm, tf), F32) + _nbytes((tm, tf), BF16) + _nbytes((chunk, d), F32)
    gain_spec = pl.BlockSpec((1, d), lambda i, f: (0, 0))
    extra_specs, extra_args = ([gain_spec], [g_final.reshape(1, d)]) if final_norm else ([], [])
    return pl.pallas_call(
        functools.partial(_mlp_kernel, final_norm=final_norm, chunk=chunk),
        grid=(n_i, n_f),
        in_specs=[
            pl.BlockSpec((tm, d), lambda i, f: (i, 0)),
            pl.BlockSpec((tm, d), lambda i, f: (jnp.minimum(i + 1, n_i - 1), 0)),
            gain_spec,
            pl.BlockSpec((None, d, tf), lambda i, f: (layer, 0, f)),
            pl.BlockSpec((None, tf, d), lambda i, f: (layer, f, 0)),
        ] + extra_specs,
        out_specs=pl.BlockSpec((tm, d), lambda i, f: (i, 0)),
        out_shape=jax.ShapeDtypeStruct((m, d), F32),
        scratch_shapes=[pltpu.VMEM((tm, d), BF16), pltpu.VMEM((tm, d), BF16)],
        compiler_params=pltpu.CompilerParams(
            dimension_semantics=("arbitrary", "arbitrary"),
            vmem_limit_bytes=_vmem_limit(blocks, resident, temps),
        ),
        name=name,
    )(x, x, g.reshape(1, d), w1, w2, *extra_args)


def _side_cast_specs(w, n_steps, step_index):
    rows, cols = w.shape
    assert rows % n_steps == 0 and (rows // n_steps) % 16 == 0
    rb = rows // n_steps
    index_map = lambda *ids: (step_index(*ids), 0)
    spec = pl.BlockSpec((rb, cols), index_map)
    return spec, spec, jax.ShapeDtypeStruct((rows, cols), BF16), _nbytes((rb, cols), F32) + _nbytes((rb, cols), BF16)


NA_ROWS_PER_ITER = 16


def _na_bias_table(rpb):
    n_heads = rpb.shape[0]
    col = jnp.arange(GRID_W)
    col_start = jnp.clip(col - NA_COL_WIN // 2, 0, GRID_W - NA_COL_WIN)
    kc = jnp.arange(GRID_W)[None, :]
    valid = (kc >= col_start[:, None]) & (kc < col_start[:, None] + NA_COL_WIN)
    dc = kc - col[:, None] + (NA_COL_WIN - 1)
    onehot = (dc[:, :, None] == jnp.arange(2 * NA_COL_WIN - 1)[None, None, :]).astype(F32)
    by_col = jnp.einsum("hrd,ckd->hcrk", rpb.astype(F32) * LOG2E, onehot, precision=lax.Precision.HIGHEST)
    by_col = jnp.where(valid[None, :, None, :], by_col, NEG_INF)
    per_delta = [by_col[:, :, NA_ROW_WIN - 1 - dl: 2 * NA_ROW_WIN - 1 - dl].reshape(n_heads, GRID_W, -1)
                 for dl in range(NA_ROW_WIN)]
    return jnp.stack(per_delta, axis=1)


def _na_kernel(q_ref, k_ref, v_ref, bias_ref, w_ref, o_ref, wo_ref, *, rows):
    span = NA_ROW_WIN * GRID_W
    wo_ref[...] = w_ref[...].astype(wo_ref.dtype)

    def one_row(r):
        r0 = jnp.clip(r - NA_ROW_WIN // 2, 0, rows - NA_ROW_WIN)
        q_start = pl.multiple_of(r * GRID_W, GRID_W)
        k_start = pl.multiple_of(r0 * GRID_W, GRID_W)
        q = q_ref[pl.ds(q_start, GRID_W), :]
        k = k_ref[pl.ds(k_start, span), :]
        v = v_ref[pl.ds(k_start, span), :]
        s = _qk_scores(q, k) + bias_ref[r - r0]
        m = jnp.max(s, axis=-1, keepdims=True)
        p = jnp.exp2(s - m).astype(BF16)
        num, den = _pv_with_denominator(p, v)
        o_ref[pl.ds(q_start, GRID_W), :] = (num / den).astype(o_ref.dtype)

    def body(it, carry):
        for u in range(NA_ROWS_PER_ITER):
            one_row(it * NA_ROWS_PER_ITER + u)
        return carry

    lax.fori_loop(0, rows // NA_ROWS_PER_ITER, body, 0)


def _na_attention(proj, bias_tbl, n_heads, side_w, *, name):
    b, t, _ = proj.shape
    side_in, side_out, side_shape, side_bytes = _side_cast_specs(side_w, b * n_heads, lambda bi, h: bi * n_heads + h)
    rows = t // GRID_W
    assert rows >= NA_ROW_WIN and rows % NA_ROWS_PER_ITER == 0
    span = NA_ROW_WIN * GRID_W
    head_blk = lambda off: pl.BlockSpec((None, t, HEAD_DIM), lambda bi, h: (bi, 0, off + h))
    blocks = 4 * _nbytes((t, HEAD_DIM), BF16) + _nbytes((NA_ROW_WIN, GRID_W, span), F32) + side_bytes
    return pl.pallas_call(
        functools.partial(_na_kernel, rows=rows),
        grid=(b, n_heads),
        in_specs=[
            head_blk(0), head_blk(n_heads), head_blk(2 * n_heads),
            pl.BlockSpec((None, NA_ROW_WIN, GRID_W, span), lambda bi, h: (h, 0, 0, 0)),
            side_in,
        ],
        out_specs=[pl.BlockSpec((None, t, HEAD_DIM), lambda bi, h: (bi, 0, h)), side_out],
        out_shape=[jax.ShapeDtypeStruct((b, t, n_heads * HEAD_DIM), BF16), side_shape],
        compiler_params=pltpu.CompilerParams(
            dimension_semantics=("parallel", "parallel"),
            vmem_limit_bytes=_vmem_limit(blocks, 0, 8 << 20),
        ),
        name=name,
    )(proj, proj, proj, bias_tbl, side_w)


DIL_SUPER = 2048
DIL_QBLK = 128
DIL_BLOCKS_PER_ITER = 16
DIL_MERGE_ROWS = 256


def _band_bias(n_q, span, offset, radius, slope2):
    rel = (lax.broadcasted_iota(jnp.int32, (n_q, 1), 0)
           - lax.broadcasted_iota(jnp.int32, (1, span), 1)) + offset
    dist = jnp.abs(rel)
    return jnp.where(dist <= radius, -slope2 * dist.astype(F32), NEG_INF)


def _dilated_kernel(slope_ref, q_ref, k_ref, v_ref, w_ref, o_ref, wo_ref, bias_ref, acc_ref, m_ref, l_ref, *,
                    seq_len):
    h = pl.program_id(1)
    sb = pl.program_id(2)
    wo_ref[...] = w_ref[...].astype(wo_ref.dtype)

    @pl.when(sb == 0)
    def _():
        slope2 = slope_ref[h] * LOG2E
        for br, (window, dil) in enumerate(DILATED_BRANCHES):
            radius = window // (2 * dil)
            for var in range(3):
                bias_ref[3 * br + var] = _band_bias(DIL_QBLK, DIL_QBLK + 2 * radius, var * radius, radius, slope2 * dil)

    for br, (window, dil) in enumerate(DILATED_BRANCHES):
        radius = window // (2 * dil)
        span = DIL_QBLK + 2 * radius
        sub = seq_len // dil
        blocks_per_res = DIL_SUPER // dil // DIL_QBLK

        def one_block(idx, br=br, dil=dil, radius=radius, span=span, sub=sub, blocks_per_res=blocks_per_res):
            res = idx // blocks_per_res
            c = idx % blocks_per_res
            q_row = res + dil * DIL_QBLK * c
            u0 = sb * (DIL_SUPER // dil) + DIL_QBLK * c
            ws = jnp.clip(u0 - radius, 0, sub - span)
            k_row = res + dil * ws
            if dil == 1:
                q_idx = pl.ds(q_row, DIL_QBLK)
                k_idx = pl.ds(k_row, span)
            else:
                q_idx = pl.ds(q_row, DIL_QBLK, stride=dil)
                k_idx = pl.ds(k_row, span, stride=dil)
            q = q_ref[q_idx, :].astype(BF16)
            k = k_ref[k_idx, :].astype(BF16)
            v = v_ref[k_idx, :].astype(BF16)
            s = _qk_scores(q, k) + bias_ref[3 * br + (u0 - ws) // radius]
            m = jnp.max(s, axis=-1, keepdims=True)
            p = jnp.exp2(s - m).astype(BF16)
            num, den = _pv_with_denominator(p, v)
            acc_ref[br, q_idx, :] = num
            l_ref[br, q_idx, :] = den
            m_ref[br, q_idx, :] = jnp.broadcast_to(m, (DIL_QBLK, HEAD_DIM))

        def body(it, carry, one_block=one_block):
            for u in range(DIL_BLOCKS_PER_ITER):
                one_block(it * DIL_BLOCKS_PER_ITER + u)
            return carry

        lax.fori_loop(0, DIL_SUPER // DIL_QBLK // DIL_BLOCKS_PER_ITER, body, 0)

    n_br = len(DILATED_BRANCHES)

    def merge(it, carry):
        rows = pl.ds(pl.multiple_of(it * DIL_MERGE_ROWS, DIL_MERGE_ROWS), DIL_MERGE_ROWS)
        ms = [m_ref[br, rows, :] for br in range(n_br)]
        m_all = functools.reduce(jnp.maximum, ms)
        ws = [jnp.exp2(m - m_all) for m in ms]
        num = sum(w * acc_ref[br, rows, :] for br, w in enumerate(ws))
        den = sum(w * l_ref[br, rows, :] for br, w in enumerate(ws))
        o_ref[rows, :] = (num / den).astype(o_ref.dtype)
        return carry

    lax.fori_loop(0, DIL_SUPER // DIL_MERGE_ROWS, merge, 0)


def _dilated_attention(proj, n_heads, side_w, *, name):
    b, t, _ = proj.shape
    n_sb = t // DIL_SUPER
    side_in, side_out, side_shape, side_bytes = _side_cast_specs(
        side_w, b * n_heads * n_sb, lambda bi, h, s: (bi * n_heads + h) * n_sb + s)
    n_br = len(DILATED_BRANCHES)
    spans = {DIL_QBLK + window // dil for window, dil in DILATED_BRANCHES}
    assert len(spans) == 1, "bias scratch assumes one key span for all branches"
    span = spans.pop()
    assert t % DIL_SUPER == 0 and DIL_SUPER // DIL_QBLK % DIL_BLOCKS_PER_ITER == 0 and DIL_SUPER % DIL_MERGE_ROWS == 0
    for window, dil in DILATED_BRANCHES:
        assert DIL_SUPER % (dil * DIL_QBLK) == 0 and t // dil >= span and window // (2 * dil) <= DIL_QBLK
    slopes = 2.0 ** (-8.0 * jnp.arange(1, n_heads + 1, dtype=F32) / n_heads)
    kv_blk = lambda off: pl.BlockSpec((None, t, HEAD_DIM), lambda bi, h, s: (bi, 0, off + h))
    blocks = (2 * _nbytes((t, HEAD_DIM), F32) + _nbytes((DIL_SUPER, HEAD_DIM), F32)
              + _nbytes((DIL_SUPER, HEAD_DIM), BF16) + side_bytes)
    stats = pltpu.VMEM((n_br, DIL_SUPER, HEAD_DIM), F32)
    scratch = 3 * _nbytes((n_br, DIL_SUPER, HEAD_DIM), F32) + _nbytes((3 * n_br, DIL_QBLK, span), F32)
    return pl.pallas_call(
        functools.partial(_dilated_kernel, seq_len=t),
        grid=(b, n_heads, t // DIL_SUPER),
        in_specs=[
            pl.BlockSpec(memory_space=pltpu.SMEM),
            pl.BlockSpec((None, DIL_SUPER, HEAD_DIM), lambda bi, h, s: (bi, s, h)),
            kv_blk(n_heads), kv_blk(2 * n_heads), side_in,
        ],
        out_specs=[pl.BlockSpec((None, DIL_SUPER, HEAD_DIM), lambda bi, h, s: (bi, s, h)), side_out],
        out_shape=[jax.ShapeDtypeStruct((b, t, n_heads * HEAD_DIM), BF16), side_shape],
        scratch_shapes=[pltpu.VMEM((3 * n_br, DIL_QBLK, span), F32), stats, stats, stats],
        compiler_params=pltpu.CompilerParams(
            dimension_semantics=("parallel", "parallel", "arbitrary"),
            vmem_limit_bytes=_vmem_limit(blocks, scratch, 8 << 20),
        ),
        name=name,
    )(slopes, proj, proj, proj, side_w)


SWA_QBLK = 256
SWA_BLOCKS_PER_STEP = 8


def _swa_kernel(slope_ref, sink_ref, q_ref, k_ref, v_ref, o_ref, bias_ref, *, seq_len, group):
    kvh = pl.program_id(1)
    span = SWA_QBLK + 2 * C_RADIUS

    @pl.when(pl.program_id(2) == 0)
    def _():
        for g in range(group):
            slope2 = slope_ref[kvh * group + g] * LOG2E
            for var in range(3):
                bias_ref[3 * g + var] = _band_bias(SWA_QBLK, span, var * C_RADIUS, C_RADIUS, slope2)

    for blk in range(SWA_BLOCKS_PER_STEP):
        rows = slice(blk * SWA_QBLK, (blk + 1) * SWA_QBLK)
        s0 = (pl.program_id(2) * SWA_BLOCKS_PER_STEP + blk) * SWA_QBLK
        ks = pl.multiple_of(jnp.clip(s0 - C_RADIUS, 0, seq_len - span), C_RADIUS)
        var = (s0 - ks) // C_RADIUS
        k = k_ref[pl.ds(ks, span), :]
        v = v_ref[pl.ds(ks, span), :]
        for g in range(group):
            cols = slice(g * HEAD_DIM, (g + 1) * HEAD_DIM)
            s = _qk_scores(q_ref[rows, cols], k) + bias_ref[3 * g + var]
            sink2 = sink_ref[kvh * group + g] * LOG2E
            m = jnp.maximum(jnp.max(s, axis=-1, keepdims=True), sink2)
            p = jnp.exp2(s - m).astype(BF16)
            num, den = _pv_with_denominator(p, v)
            o = num / (den + jnp.exp2(sink2 - m))
            o_ref[rows, cols] = o.astype(o_ref.dtype)


def _swa_attention(proj, sink, n_heads, n_kv, *, name):
    b, t, _ = proj.shape
    group = n_heads // n_kv
    span = SWA_QBLK + 2 * C_RADIUS
    step_rows = SWA_QBLK * SWA_BLOCKS_PER_STEP
    assert t % step_rows == 0 and t >= span and C_RADIUS <= SWA_QBLK and SWA_QBLK % C_RADIUS == 0
    slopes = 2.0 ** (-8.0 * jnp.arange(1, n_heads + 1, dtype=F32) / n_heads)
    kv_blk = lambda off: pl.BlockSpec((None, t, HEAD_DIM), lambda bi, kh, s: (bi, 0, off + kh))
    q_blk = pl.BlockSpec((None, step_rows, group * HEAD_DIM), lambda bi, kh, s: (bi, s, kh))
    blocks = 2 * _nbytes((t, HEAD_DIM), BF16) + 2 * _nbytes((step_rows, group * HEAD_DIM), BF16)
    bias_bytes = _nbytes((3 * group, SWA_QBLK, span), F32)
    return pl.pallas_call(
        functools.partial(_swa_kernel, seq_len=t, group=group),
        grid=(b, n_kv, t // step_rows),
        in_specs=[
            pl.BlockSpec(memory_space=pltpu.SMEM),
            pl.BlockSpec(memory_space=pltpu.SMEM),
            q_blk, kv_blk(n_heads), kv_blk(n_heads + n_kv),
        ],
        out_specs=q_blk,
        out_shape=jax.ShapeDtypeStruct((b, t, n_heads * HEAD_DIM), BF16),
        scratch_shapes=[pltpu.VMEM((3 * group, SWA_QBLK, span), F32)],
        compiler_params=pltpu.CompilerParams(
            dimension_semantics=("parallel", "parallel", "arbitrary"),
            vmem_limit_bytes=_vmem_limit(blocks, bias_bytes, 8 << 20),
        ),
        name=name,
    )(slopes, sink.astype(F32), proj, proj, proj)


def _scaled_bf16(w, n_q_cols):
    col = jnp.arange(w.shape[1])
    scale = jnp.where(col < n_q_cols, LOG2E * HEAD_DIM ** -0.5, 1.0).astype(F32)
    return (w * scale[None, :]).astype(BF16)


def kernel(x, attn_norm, mlp_norm, w_mlp_in, w_mlp_out, even_w_in, even_rpb, even_w_out,
           odd_w_qkv, odd_sink, odd_w_out, final_norm):
    b, t, d = x.shape
    depth = attn_norm.shape[0]
    n_heads_a = even_rpb.shape[1]
    wa = n_heads_a * HEAD_DIM
    n_heads_b = (even_w_in.shape[2] - 3 * wa) // (3 * HEAD_DIM)
    n_heads_c = odd_sink.shape[1]
    n_kv_c = (odd_w_qkv.shape[2] // HEAD_DIM - n_heads_c) // 2
    xf = x.reshape(b * t, d)
    w1_bf16 = w2_bf16 = None

    for i in range(depth):
        j = i // 2
        if i % 2 == 0:
            w_in = even_w_in[j]
            proj_a = _norm_matmul(xf, attn_norm[i], _scaled_bf16(w_in[:, :3 * wa], wa), BF16, name=f"l{i}_proj_a")
            proj_b = _norm_matmul(xf, attn_norm[i], _scaled_bf16(w_in[:, 3 * wa:], n_heads_b * HEAD_DIM), F32,
                                  name=f"l{i}_proj_b")
            w1_pair, w2_pair = w_mlp_in[i:i + 2], w_mlp_out[i:i + 2]
            o_a, w1_bf16 = _na_attention(proj_a.reshape(b, t, -1), _na_bias_table(even_rpb[j]), n_heads_a,
                                         w1_pair.reshape(-1, w1_pair.shape[-1]), name=f"l{i}_na")
            o_b, w2_bf16 = _dilated_attention(proj_b.reshape(b, t, -1), n_heads_b,
                                              w2_pair.reshape(-1, w2_pair.shape[-1]), name=f"l{i}_dilated")
            w1_bf16, w2_bf16 = w1_bf16.reshape(w1_pair.shape), w2_bf16.reshape(w2_pair.shape)
            xf = _proj_residual(o_a.reshape(b * t, -1), 0, o_b.reshape(b * t, -1), 0,
                                even_w_out[j].astype(BF16), xf, name=f"l{i}_out")
        else:
            proj_c = _norm_matmul(xf, attn_norm[i], _scaled_bf16(odd_w_qkv[j], n_heads_c * HEAD_DIM), BF16,
                                  name=f"l{i}_proj_c")
            o_c = _swa_attention(proj_c.reshape(b, t, -1), odd_sink[j], n_heads_c, n_kv_c, name=f"l{i}_swa")
            o_c = o_c.reshape(b * t, -1)
            xf = _proj_residual(o_c, 0, o_c, 1, odd_w_out[j].astype(BF16), xf, name=f"l{i}_out")
        g_final = final_norm if i == depth - 1 else None
        xf = _mlp(xf, mlp_norm[i], w1_bf16, w2_bf16, i % 2, g_final, name=f"l{i}_mlp")
    return xf.reshape(b, t, d)
```

```python
import functools
import math

import jax
import jax.numpy as jnp
from jax import lax
from jax.experimental import pallas as pl
from jax.experimental.pallas import tpu as pltpu

HEAD_DIM = 128
GRID_W = 64
NA_ROW_WIN = 8
NA_COL_WIN = 16
DILATED_BRANCHES = ((128, 1), (512, 4), (2048, 16))
C_RADIUS = 128
NORM_EPS = 1e-6
NEG_INF = -1e30
LOG2E = math.log2(math.e)

V7X_VMEM_LIMIT_CAP = 56 * 1024 * 1024
BF16 = jnp.bfloat16
F32 = jnp.float32


def _vmem_limit(block_bytes, scratch_bytes, temp_bytes):
    need = 2 * block_bytes + scratch_bytes + temp_bytes + (2 << 20)
    return int(min(max(need, 16 << 20), V7X_VMEM_LIMIT_CAP))


def _nbytes(shape, dtype):
    n = 1
    for s in shape:
        n *= s
    return n * jnp.dtype(dtype).itemsize


def _rms_norm_f32(x, g):
    y = x * lax.rsqrt(jnp.mean(x * x, axis=-1, keepdims=True) + NORM_EPS)
    return y * g


def _qk_scores(q, k):
    return lax.dot_general(q, k, (((1,), (1,)), ((), ())), preferred_element_type=F32)


def _pv_with_denominator(p, v):
    v_ext = jnp.concatenate([v, jnp.ones_like(v)], axis=1)
    pv = jnp.dot(p, v_ext, preferred_element_type=F32)
    return pv[:, :HEAD_DIM], pv[:, HEAD_DIM:]


def _norm_matmul_kernel(x_ref, g_ref, w_ref, o_ref, *rest, emit_h):
    h_ref = rest[-1]

    @pl.when(pl.program_id(1) == 0)
    def _():
        h_ref[...] = _rms_norm_f32(x_ref[...], g_ref[...]).astype(h_ref.dtype)
        if emit_h:
            rest[0][...] = h_ref[...]

    o_ref[...] = jnp.dot(h_ref[...], w_ref[...], preferred_element_type=F32).astype(o_ref.dtype)


def _norm_matmul(x, g, w, out_dtype, *, emit_h=False, tm=1024, tn=1024, name):
    m, d = x.shape
    n = w.shape[1]
    assert m % tm == 0 and n % tn == 0
    blocks = _nbytes((tm, d), F32) + _nbytes((d, tn), BF16) + _nbytes((tm, tn), out_dtype)
    out_specs = [pl.BlockSpec((tm, tn), lambda i, j: (i, j))]
    out_shape = [jax.ShapeDtypeStruct((m, n), out_dtype)]
    if emit_h:
        out_specs.append(pl.BlockSpec((tm, d), lambda i, j: (i, 0)))
        out_shape.append(jax.ShapeDtypeStruct((m, d), BF16))
        blocks += _nbytes((tm, d), BF16)
    outs = pl.pallas_call(
        functools.partial(_norm_matmul_kernel, emit_h=emit_h),
        grid=(m // tm, n // tn),
        in_specs=[
            pl.BlockSpec((tm, d), lambda i, j: (i, 0)),
            pl.BlockSpec((1, d), lambda i, j: (0, 0)),
            pl.BlockSpec((d, tn), lambda i, j: (0, j)),
        ],
        out_specs=out_specs,
        out_shape=out_shape,
        scratch_shapes=[pltpu.VMEM((tm, d), BF16)],
        compiler_params=pltpu.CompilerParams(
            dimension_semantics=("parallel", "arbitrary"),
            vmem_limit_bytes=_vmem_limit(blocks, _nbytes((tm, d), BF16), _nbytes((tm, d), F32) + _nbytes((tm, tn), F32)),
        ),
        name=name,
    )(x, g.reshape(1, d), w)
    return outs if emit_h else outs[0]


def _proj_residual_kernel(a_ref, b_ref, wa_ref, wb_ref, x_ref, o_ref):
    acc = jnp.dot(a_ref[...], wa_ref[...], preferred_element_type=F32)
    acc = acc + jnp.dot(b_ref[...], wb_ref[...], preferred_element_type=F32)
    o_ref[...] = x_ref[...] + acc


def _proj_residual(a, a_col, b, b_col, w, x, *, tm=1024, tn=1024, name):
    m, d = x.shape
    kh = w.shape[0] // 2
    assert m % tm == 0 and d % tn == 0
    blocks = 2 * _nbytes((tm, kh), BF16) + 2 * _nbytes((kh, tn), BF16) + 2 * _nbytes((tm, tn), F32)
    return pl.pallas_call(
        _proj_residual_kernel,
        grid=(m // tm, d // tn),
        in_specs=[
            pl.BlockSpec((tm, kh), lambda i, j: (i, a_col)),
            pl.BlockSpec((tm, kh), lambda i, j: (i, b_col)),
            pl.BlockSpec((kh, tn), lambda i, j: (0, j)),
            pl.BlockSpec((kh, tn), lambda i, j: (1, j)),
            pl.BlockSpec((tm, tn), lambda i, j: (i, j)),
        ],
        out_specs=pl.BlockSpec((tm, tn), lambda i, j: (i, j)),
        out_shape=jax.ShapeDtypeStruct((m, d), F32),
        compiler_params=pltpu.CompilerParams(
            dimension_semantics=("parallel", "parallel"),
            vmem_limit_bytes=_vmem_limit(blocks, 0, 2 * _nbytes((tm, tn), F32)),
        ),
        name=name,
    )(a, b, w, w, x)


def _mlp_kernel(x_ref, g_ref, w1_ref, w2_ref, *rest, final_norm):
    gf_ref = rest[0] if final_norm else None
    o_ref, h_ref = rest[-2:]
    f = pl.program_id(1)

    @pl.when(f == 0)
    def _():
        x = x_ref[...]
        h_ref[...] = _rms_norm_f32(x, g_ref[...]).astype(h_ref.dtype)
        o_ref[...] = x

    a = jnp.maximum(jnp.dot(h_ref[...], w1_ref[...], preferred_element_type=F32), 0.0)
    o_ref[...] += jnp.dot((a * a).astype(BF16), w2_ref[...], preferred_element_type=F32)

    if final_norm:
        @pl.when(f == pl.num_programs(1) - 1)
        def _():
            o_ref[...] = _rms_norm_f32(o_ref[...], gf_ref[...])


def _mlp(x, g, w1, w2, layer, g_final=None, *, tm=512, tf=1024, name):
    m, d = x.shape
    ff = w1.shape[2]
    assert m % tm == 0 and ff % tf == 0
    final_norm = g_final is not None
    blocks = 2 * _nbytes((tm, d), F32) + _nbytes((d, tf), BF16) + _nbytes((tf, d), BF16)
    resident = _nbytes((tm, d), BF16)
    temps = _nbytes((tm, tf), F32) + _nbytes((tm, tf), BF16) + _nbytes((tm, d), F32)
    gain_spec = pl.BlockSpec((1, d), lambda i, f: (0, 0))
    extra_specs, extra_args = ([gain_spec], [g_final.reshape(1, d)]) if final_norm else ([], [])
    return pl.pallas_call(
        functools.partial(_mlp_kernel, final_norm=final_norm),
        grid=(m // tm, ff // tf),
        in_specs=[
            pl.BlockSpec((tm, d), lambda i, f: (i, 0)),
            gain_spec,
            pl.BlockSpec((None, d, tf), lambda i, f: (layer, 0, f)),
            pl.BlockSpec((None, tf, d), lambda i, f: (layer, f, 0)),
        ] + extra_specs,
        out_specs=pl.BlockSpec((tm, d), lambda i, f: (i, 0)),
        out_shape=jax.ShapeDtypeStruct((m, d), F32),
        scratch_shapes=[pltpu.VMEM((tm, d), BF16)],
        compiler_params=pltpu.CompilerParams(
            dimension_semantics=("parallel", "arbitrary"),
            vmem_limit_bytes=_vmem_limit(blocks, resident, temps),
        ),
        name=name,
    )(x, g.reshape(1, d), w1, w2, *extra_args)


def _side_cast_specs(w, n_steps, step_index):
    rows, cols = w.shape
    assert rows % n_steps == 0 and (rows // n_steps) % 16 == 0
    rb = rows // n_steps
    index_map = lambda *ids: (step_index(*ids), 0)
    spec = pl.BlockSpec((rb, cols), index_map)
    return spec, spec, jax.ShapeDtypeStruct((rows, cols), BF16), _nbytes((rb, cols), F32) + _nbytes((rb, cols), BF16)


NA_ROWS_PER_STEP = 32


def _na_bias_table(rpb):
    n_heads = rpb.shape[0]
    col = jnp.arange(GRID_W)
    col_start = jnp.clip(col - NA_COL_WIN // 2, 0, GRID_W - NA_COL_WIN)
    kc = jnp.arange(GRID_W)[None, :]
    valid = (kc >= col_start[:, None]) & (kc < col_start[:, None] + NA_COL_WIN)
    dc = kc - col[:, None] + (NA_COL_WIN - 1)
    onehot = (dc[:, :, None] == jnp.arange(2 * NA_COL_WIN - 1)[None, None, :]).astype(F32)
    by_col = jnp.einsum("hrd,ckd->hcrk", rpb.astype(F32) * LOG2E, onehot, precision=lax.Precision.HIGHEST)
    by_col = jnp.where(valid[None, :, None, :], by_col, NEG_INF)
    per_delta = [by_col[:, :, NA_ROW_WIN - 1 - dl: 2 * NA_ROW_WIN - 1 - dl].reshape(n_heads, GRID_W, -1)
                 for dl in range(NA_ROW_WIN)]
    return jnp.stack(per_delta, axis=1)


def _na_kernel(q_ref, k_ref, v_ref, bias_ref, h_ref, wb_ref, o_ref, pb_ref, *, rows):
    span = NA_ROW_WIN * GRID_W
    pb_ref[...] = jnp.dot(h_ref[...], wb_ref[...], preferred_element_type=F32)

    def one_row(r):
        r0 = jnp.clip(r - NA_ROW_WIN // 2, 0, rows - NA_ROW_WIN)
        q_start = pl.multiple_of(r * GRID_W, GRID_W)
        k_start = pl.multiple_of(r0 * GRID_W, GRID_W)
        q = q_ref[pl.ds(q_start, GRID_W), :]
        k = k_ref[pl.ds(k_start, span), :]
        v = v_ref[pl.ds(k_start, span), :]
        s = _qk_scores(q, k) + bias_ref[r - r0]
        m = jnp.max(s, axis=-1, keepdims=True)
        p = jnp.exp2(s - m).astype(BF16)
        num, den = _pv_with_denominator(p, v)
        o_ref[pl.ds(q_start, GRID_W), :] = (num / den).astype(o_ref.dtype)

    for u in range(NA_ROWS_PER_STEP):
        one_row(pl.program_id(2) * NA_ROWS_PER_STEP + u)


def _na_attention_and_proj(proj, bias_tbl, n_heads, h_norm, w_b, *, name):
    b, t, _ = proj.shape
    rows = t // GRID_W
    assert rows >= NA_ROW_WIN and rows % NA_ROWS_PER_STEP == 0
    n_trips = rows // NA_ROWS_PER_STEP
    n_steps = b * n_heads * n_trips
    m, d = h_norm.shape
    n = w_b.shape[1]
    assert m % n_steps == 0 and (m // n_steps) % 16 == 0
    rb = m // n_steps
    span = NA_ROW_WIN * GRID_W
    step_id = lambda bi, h, tr: ((bi * n_heads + h) * n_trips + tr, 0)
    head_blk = lambda off: pl.BlockSpec((None, t, HEAD_DIM), lambda bi, h, tr: (bi, 0, off + h))
    blocks = (4 * _nbytes((t, HEAD_DIM), BF16) + _nbytes((NA_ROW_WIN, GRID_W, span), F32)
              + _nbytes((rb, d), BF16) + _nbytes((rb, n), F32))
    return pl.pallas_call(
        functools.partial(_na_kernel, rows=rows),
        grid=(b, n_heads, n_trips),
        in_specs=[
            head_blk(0), head_blk(n_heads), head_blk(2 * n_heads),
            pl.BlockSpec((None, NA_ROW_WIN, GRID_W, span), lambda bi, h, tr: (h, 0, 0, 0)),
            pl.BlockSpec((rb, d), step_id),
            pl.BlockSpec((d, n), lambda bi, h, tr: (0, 0), pipeline_mode=pl.Buffered(1)),
        ],
        out_specs=[pl.BlockSpec((None, t, HEAD_DIM), lambda bi, h, tr: (bi, 0, h)), pl.BlockSpec((rb, n), step_id)],
        out_shape=[jax.ShapeDtypeStruct((b, t, n_heads * HEAD_DIM), BF16), jax.ShapeDtypeStruct((m, n), F32)],
        compiler_params=pltpu.CompilerParams(
            dimension_semantics=("parallel", "parallel", "arbitrary"),
            vmem_limit_bytes=_vmem_limit(blocks, _nbytes((d, n), BF16), 8 << 20),
        ),
        name=name,
    )(proj, proj, proj, bias_tbl, h_norm, w_b)


DIL_SUPER = 2048
DIL_QBLK = 128
DIL_BLOCKS_PER_ITER = 16
DIL_MERGE_ROWS = 256


def _band_bias(n_q, span, offset, radius, slope2):
    rel = (lax.broadcasted_iota(jnp.int32, (n_q, 1), 0)
           - lax.broadcasted_iota(jnp.int32, (1, span), 1)) + offset
    dist = jnp.abs(rel)
    return jnp.where(dist <= radius, -slope2 * dist.astype(F32), NEG_INF)


def _dilated_kernel(slope_ref, q_ref, k_ref, v_ref, w1_ref, w2_ref, o_ref, w1o_ref, w2o_ref, bias_ref, acc_ref,
                    m_ref, l_ref, *, seq_len):
    h = pl.program_id(1)
    sb = pl.program_id(2)
    w1o_ref[...] = w1_ref[...].astype(w1o_ref.dtype)
    w2o_ref[...] = w2_ref[...].astype(w2o_ref.dtype)

    @pl.when(sb == 0)
    def _():
        slope2 = slope_ref[h] * LOG2E
        for br, (window, dil) in enumerate(DILATED_BRANCHES):
            radius = window // (2 * dil)
            for var in range(3):
                bias_ref[3 * br + var] = _band_bias(DIL_QBLK, DIL_QBLK + 2 * radius, var * radius, radius, slope2 * dil)

    for br, (window, dil) in enumerate(DILATED_BRANCHES):
        radius = window // (2 * dil)
        span = DIL_QBLK + 2 * radius
        sub = seq_len // dil
        blocks_per_res = DIL_SUPER // dil // DIL_QBLK

        def one_block(idx, br=br, dil=dil, radius=radius, span=span, sub=sub, blocks_per_res=blocks_per_res):
            res = idx // blocks_per_res
            c = idx % blocks_per_res
            q_row = res + dil * DIL_QBLK * c
            u0 = sb * (DIL_SUPER // dil) + DIL_QBLK * c
            ws = jnp.clip(u0 - radius, 0, sub - span)
            k_row = res + dil * ws
            if dil == 1:
                q_idx = pl.ds(q_row, DIL_QBLK)
                k_idx = pl.ds(k_row, span)
            else:
                q_idx = pl.ds(q_row, DIL_QBLK, stride=dil)
                k_idx = pl.ds(k_row, span, stride=dil)
            q = q_ref[q_idx, :].astype(BF16)
            k = k_ref[k_idx, :].astype(BF16)
            v = v_ref[k_idx, :].astype(BF16)
            s = _qk_scores(q, k) + bias_ref[3 * br + (u0 - ws) // radius]
            m = jnp.max(s, axis=-1, keepdims=True)
            p = jnp.exp2(s - m).astype(BF16)
            num, den = _pv_with_denominator(p, v)
            acc_ref[br, q_idx, :] = num
            l_ref[br, q_idx, :] = den
            m_ref[br, q_idx, :] = jnp.broadcast_to(m, (DIL_QBLK, HEAD_DIM))

        def body(it, carry, one_block=one_block):
            for u in range(DIL_BLOCKS_PER_ITER):
                one_block(it * DIL_BLOCKS_PER_ITER + u)
            return carry

        lax.fori_loop(0, DIL_SUPER // DIL_QBLK // DIL_BLOCKS_PER_ITER, body, 0)

    n_br = len(DILATED_BRANCHES)

    def merge(it, carry):
        rows = pl.ds(pl.multiple_of(it * DIL_MERGE_ROWS, DIL_MERGE_ROWS), DIL_MERGE_ROWS)
        ms = [m_ref[br, rows, :] for br in range(n_br)]
        m_all = functools.reduce(jnp.maximum, ms)
        ws = [jnp.exp2(m - m_all) for m in ms]
        num = sum(w * acc_ref[br, rows, :] for br, w in enumerate(ws))
        den = sum(w * l_ref[br, rows, :] for br, w in enumerate(ws))
        o_ref[rows, :] = (num / den).astype(o_ref.dtype)
        return carry

    lax.fori_loop(0, DIL_SUPER // DIL_MERGE_ROWS, merge, 0)


def _dilated_attention(proj, n_heads, side_w1, side_w2, *, name):
    b, t, _ = proj.shape
    n_sb = t // DIL_SUPER
    step_id = lambda bi, h, s: (bi * n_heads + h) * n_sb + s
    side1_in, side1_out, side1_shape, side1_bytes = _side_cast_specs(side_w1, b * n_heads * n_sb, step_id)
    side2_in, side2_out, side2_shape, side2_bytes = _side_cast_specs(side_w2, b * n_heads * n_sb, step_id)
    side_bytes = side1_bytes + side2_bytes
    n_br = len(DILATED_BRANCHES)
    spans = {DIL_QBLK + window // dil for window, dil in DILATED_BRANCHES}
    assert len(spans) == 1, "bias scratch assumes one key span for all branches"
    span = spans.pop()
    assert t % DIL_SUPER == 0 and DIL_SUPER // DIL_QBLK % DIL_BLOCKS_PER_ITER == 0 and DIL_SUPER % DIL_MERGE_ROWS == 0
    for window, dil in DILATED_BRANCHES:
        assert DIL_SUPER % (dil * DIL_QBLK) == 0 and t // dil >= span and window // (2 * dil) <= DIL_QBLK
    slopes = 2.0 ** (-8.0 * jnp.arange(1, n_heads + 1, dtype=F32) / n_heads)
    kv_blk = lambda off: pl.BlockSpec((None, t, HEAD_DIM), lambda bi, h, s: (bi, 0, off + h))
    blocks = (2 * _nbytes((t, HEAD_DIM), F32) + _nbytes((DIL_SUPER, HEAD_DIM), F32)
              + _nbytes((DIL_SUPER, HEAD_DIM), BF16) + side_bytes)
    stats = pltpu.VMEM((n_br, DIL_SUPER, HEAD_DIM), F32)
    scratch = 3 * _nbytes((n_br, DIL_SUPER, HEAD_DIM), F32) + _nbytes((3 * n_br, DIL_QBLK, span), F32)
    return pl.pallas_call(
        functools.partial(_dilated_kernel, seq_len=t),
        grid=(b, n_heads, t // DIL_SUPER),
        in_specs=[
            pl.BlockSpec(memory_space=pltpu.SMEM),
            pl.BlockSpec((None, DIL_SUPER, HEAD_DIM), lambda bi, h, s: (bi, s, h)),
            kv_blk(n_heads), kv_blk(2 * n_heads), side1_in, side2_in,
        ],
        out_specs=[pl.BlockSpec((None, DIL_SUPER, HEAD_DIM), lambda bi, h, s: (bi, s, h)), side1_out, side2_out],
        out_shape=[jax.ShapeDtypeStruct((b, t, n_heads * HEAD_DIM), BF16), side1_shape, side2_shape],
        scratch_shapes=[pltpu.VMEM((3 * n_br, DIL_QBLK, span), F32), stats, stats, stats],
        compiler_params=pltpu.CompilerParams(
            dimension_semantics=("parallel", "parallel", "arbitrary"),
            vmem_limit_bytes=_vmem_limit(blocks, scratch, 8 << 20),
        ),
        name=name,
    )(slopes, proj, proj, proj, side_w1, side_w2)


SWA_QBLK = 256
SWA_BLOCKS_PER_STEP = 8


def _swa_kernel(slope_ref, sink_ref, q_ref, k_ref, v_ref, o_ref, bias_ref, *, seq_len, group):
    kvh = pl.program_id(1)
    span = SWA_QBLK + 2 * C_RADIUS

    @pl.when(pl.program_id(2) == 0)
    def _():
        for g in range(group):
            slope2 = slope_ref[kvh * group + g] * LOG2E
            for var in range(3):
                bias_ref[3 * g + var] = _band_bias(SWA_QBLK, span, var * C_RADIUS, C_RADIUS, slope2)

    for blk in range(SWA_BLOCKS_PER_STEP):
        rows = slice(blk * SWA_QBLK, (blk + 1) * SWA_QBLK)
        s0 = (pl.program_id(2) * SWA_BLOCKS_PER_STEP + blk) * SWA_QBLK
        ks = pl.multiple_of(jnp.clip(s0 - C_RADIUS, 0, seq_len - span), C_RADIUS)
        var = (s0 - ks) // C_RADIUS
        k = k_ref[pl.ds(ks, span), :]
        v = v_ref[pl.ds(ks, span), :]
        for g in range(group):
            cols = slice(g * HEAD_DIM, (g + 1) * HEAD_DIM)
            s = _qk_scores(q_ref[rows, cols], k) + bias_ref[3 * g + var]
            sink2 = sink_ref[kvh * group + g] * LOG2E
            m = jnp.maximum(jnp.max(s, axis=-1, keepdims=True), sink2)
            p = jnp.exp2(s - m).astype(BF16)
            num, den = _pv_with_denominator(p, v)
            o = num / (den + jnp.exp2(sink2 - m))
            o_ref[rows, cols] = o.astype(o_ref.dtype)


def _swa_attention(proj, sink, n_heads, n_kv, *, name):
    b, t, _ = proj.shape
    group = n_heads // n_kv
    span = SWA_QBLK + 2 * C_RADIUS
    step_rows = SWA_QBLK * SWA_BLOCKS_PER_STEP
    assert t % step_rows == 0 and t >= span and C_RADIUS <= SWA_QBLK and SWA_QBLK % C_RADIUS == 0
    slopes = 2.0 ** (-8.0 * jnp.arange(1, n_heads + 1, dtype=F32) / n_heads)
    kv_blk = lambda off: pl.BlockSpec((None, t, HEAD_DIM), lambda bi, kh, s: (bi, 0, off + kh))
    q_blk = pl.BlockSpec((None, step_rows, group * HEAD_DIM), lambda bi, kh, s: (bi, s, kh))
    blocks = 2 * _nbytes((t, HEAD_DIM), BF16) + 2 * _nbytes((step_rows, group * HEAD_DIM), BF16)
    bias_bytes = _nbytes((3 * group, SWA_QBLK, span), F32)
    return pl.pallas_call(
        functools.partial(_swa_kernel, seq_len=t, group=group),
        grid=(b, n_kv, t // step_rows),
        in_specs=[
            pl.BlockSpec(memory_space=pltpu.SMEM),
            pl.BlockSpec(memory_space=pltpu.SMEM),
            q_blk, kv_blk(n_heads), kv_blk(n_heads + n_kv),
        ],
        out_specs=q_blk,
        out_shape=jax.ShapeDtypeStruct((b, t, n_heads * HEAD_DIM), BF16),
        scratch_shapes=[pltpu.VMEM((3 * group, SWA_QBLK, span), F32)],
        compiler_params=pltpu.CompilerParams(
            dimension_semantics=("parallel", "parallel", "arbitrary"),
            vmem_limit_bytes=_vmem_limit(blocks, bias_bytes, 8 << 20),
        ),
        name=name,
    )(slopes, sink.astype(F32), proj, proj, proj)


def _scaled_bf16(w, n_q_cols):
    col = jnp.arange(w.shape[1])
    scale = jnp.where(col < n_q_cols, LOG2E * HEAD_DIM ** -0.5, 1.0).astype(F32)
    return (w * scale[None, :]).astype(BF16)


def kernel(x, attn_norm, mlp_norm, w_mlp_in, w_mlp_out, even_w_in, even_rpb, even_w_out,
           odd_w_qkv, odd_sink, odd_w_out, final_norm):
    b, t, d = x.shape
    depth = attn_norm.shape[0]
    n_heads_a = even_rpb.shape[1]
    wa = n_heads_a * HEAD_DIM
    n_heads_b = (even_w_in.shape[2] - 3 * wa) // (3 * HEAD_DIM)
    n_heads_c = odd_sink.shape[1]
    n_kv_c = (odd_w_qkv.shape[2] // HEAD_DIM - n_heads_c) // 2
    xf = x.reshape(b * t, d)
    w1_bf16 = w2_bf16 = None

    for i in range(depth):
        j = i // 2
        if i % 2 == 0:
            w_in = even_w_in[j]
            proj_a, h_norm = _norm_matmul(xf, attn_norm[i], _scaled_bf16(w_in[:, :3 * wa], wa), BF16, emit_h=True,
                                          name=f"l{i}_proj_a")
            o_a, proj_b = _na_attention_and_proj(
                proj_a.reshape(b, t, -1), _na_bias_table(even_rpb[j]), n_heads_a, h_norm,
                _scaled_bf16(w_in[:, 3 * wa:], n_heads_b * HEAD_DIM), name=f"l{i}_na_proj_b")
            w1_pair, w2_pair = w_mlp_in[i:i + 2], w_mlp_out[i:i + 2]
            o_b, w1_bf16, w2_bf16 = _dilated_attention(
                proj_b.reshape(b, t, -1), n_heads_b, w1_pair.reshape(-1, w1_pair.shape[-1]),
                w2_pair.reshape(-1, w2_pair.shape[-1]), name=f"l{i}_dilated")
            w1_bf16, w2_bf16 = w1_bf16.reshape(w1_pair.shape), w2_bf16.reshape(w2_pair.shape)
            xf = _proj_residual(o_a.reshape(b * t, -1), 0, o_b.reshape(b * t, -1), 0,
                                even_w_out[j].astype(BF16), xf, name=f"l{i}_out")
        else:
            proj_c = _norm_matmul(xf, attn_norm[i], _scaled_bf16(odd_w_qkv[j], n_heads_c * HEAD_DIM), BF16,
                                  name=f"l{i}_proj_c")
            o_c = _swa_attention(proj_c.reshape(b, t, -1), odd_sink[j], n_heads_c, n_kv_c, name=f"l{i}_swa")
            o_c = o_c.reshape(b * t, -1)
            xf = _proj_residual(o_c, 0, o_c, 1, odd_w_out[j].astype(BF16), xf, name=f"l{i}_out")
        g_final = final_norm if i == depth - 1 else None
        xf = _mlp(xf, mlp_norm[i], w1_bf16, w2_bf16, i % 2, g_final, name=f"l{i}_mlp")
    return xf.reshape(b, t, d)
```

```python
import functools
import math

import jax
import jax.numpy as jnp
from jax import lax
from jax.experimental import pallas as pl
from jax.experimental.pallas import tpu as pltpu

HEAD_DIM = 128
GRID_W = 64
NA_ROW_WIN = 8
NA_COL_WIN = 16
DILATED_BRANCHES = ((128, 1), (512, 4), (2048, 16))
C_RADIUS = 128
NORM_EPS = 1e-6
NEG_INF = -1e30
LOG2E = math.log2(math.e)

V7X_VMEM_LIMIT_CAP = 56 * 1024 * 1024
BF16 = jnp.bfloat16
F32 = jnp.float32


def _vmem_limit(block_bytes, scratch_bytes, temp_bytes):
    need = 2 * block_bytes + scratch_bytes + temp_bytes + (2 << 20)
    return int(min(max(need, 16 << 20), V7X_VMEM_LIMIT_CAP))


def _nbytes(shape, dtype):
    n = 1
    for s in shape:
        n *= s
    return n * jnp.dtype(dtype).itemsize


def _rms_norm_f32(x, g):
    y = x * lax.rsqrt(jnp.mean(x * x, axis=-1, keepdims=True) + NORM_EPS)
    return y * g


def _qk_scores(q, k):
    return lax.dot_general(q, k, (((1,), (1,)), ((), ())), preferred_element_type=F32)


def _pv_with_denominator(p, v):
    v_ext = jnp.concatenate([v, jnp.ones_like(v)], axis=1)
    pv = jnp.dot(p, v_ext, preferred_element_type=F32)
    return pv[:, :HEAD_DIM], pv[:, HEAD_DIM:]


def _norm_matmul_kernel(x_ref, g_ref, w_ref, o_ref, h_ref):
    @pl.when(pl.program_id(1) == 0)
    def _():
        h_ref[...] = _rms_norm_f32(x_ref[...], g_ref[...]).astype(h_ref.dtype)

    o_ref[...] = jnp.dot(h_ref[...], w_ref[...], preferred_element_type=F32).astype(o_ref.dtype)


def _norm_matmul(x, g, w, out_dtype, *, tm=1024, tn=1024, name):
    m, d = x.shape
    n = w.shape[1]
    assert m % tm == 0 and n % tn == 0
    blocks = _nbytes((tm, d), F32) + _nbytes((d, tn), BF16) + _nbytes((tm, tn), out_dtype)
    return pl.pallas_call(
        _norm_matmul_kernel,
        grid=(m // tm, n // tn),
        in_specs=[
            pl.BlockSpec((tm, d), lambda i, j: (i, 0)),
            pl.BlockSpec((1, d), lambda i, j: (0, 0)),
            pl.BlockSpec((d, tn), lambda i, j: (0, j)),
        ],
        out_specs=pl.BlockSpec((tm, tn), lambda i, j: (i, j)),
        out_shape=jax.ShapeDtypeStruct((m, n), out_dtype),
        scratch_shapes=[pltpu.VMEM((tm, d), BF16)],
        compiler_params=pltpu.CompilerParams(
            dimension_semantics=("parallel", "arbitrary"),
            vmem_limit_bytes=_vmem_limit(blocks, _nbytes((tm, d), BF16), _nbytes((tm, d), F32) + _nbytes((tm, tn), F32)),
        ),
        name=name,
    )(x, g.reshape(1, d), w)


def _proj_residual_kernel(a_ref, b_ref, wa_ref, wb_ref, x_ref, o_ref):
    acc = jnp.dot(a_ref[...], wa_ref[...], preferred_element_type=F32)
    acc = acc + jnp.dot(b_ref[...], wb_ref[...], preferred_element_type=F32)
    o_ref[...] = x_ref[...] + acc


def _proj_residual(a, a_col, b, b_col, w, x, *, tm=1024, tn=1024, name):
    m, d = x.shape
    kh = w.shape[0] // 2
    assert m % tm == 0 and d % tn == 0
    blocks = 2 * _nbytes((tm, kh), BF16) + 2 * _nbytes((kh, tn), BF16) + 2 * _nbytes((tm, tn), F32)
    return pl.pallas_call(
        _proj_residual_kernel,
        grid=(m // tm, d // tn),
        in_specs=[
            pl.BlockSpec((tm, kh), lambda i, j: (i, a_col)),
            pl.BlockSpec((tm, kh), lambda i, j: (i, b_col)),
            pl.BlockSpec((kh, tn), lambda i, j: (0, j)),
            pl.BlockSpec((kh, tn), lambda i, j: (1, j)),
            pl.BlockSpec((tm, tn), lambda i, j: (i, j)),
        ],
        out_specs=pl.BlockSpec((tm, tn), lambda i, j: (i, j)),
        out_shape=jax.ShapeDtypeStruct((m, d), F32),
        compiler_params=pltpu.CompilerParams(
            dimension_semantics=("parallel", "parallel"),
            vmem_limit_bytes=_vmem_limit(blocks, 0, 2 * _nbytes((tm, tn), F32)),
        ),
        name=name,
    )(a, b, w, w, x)


def _mlp_kernel(x_ref, g_ref, w1_ref, w2_ref, *rest, final_norm):
    gf_ref = rest[0] if final_norm else None
    o_ref, h_ref = rest[-2:]
    f = pl.program_id(1)

    @pl.when(f == 0)
    def _():
        x = x_ref[...]
        h_ref[...] = _rms_norm_f32(x, g_ref[...]).astype(h_ref.dtype)
        o_ref[...] = x

    a = jnp.maximum(jnp.dot(h_ref[...], w1_ref[...], preferred_element_type=F32), 0.0)
    o_ref[...] += jnp.dot((a * a).astype(BF16), w2_ref[...], preferred_element_type=F32)

    if final_norm:
        @pl.when(f == pl.num_programs(1) - 1)
        def _():
            o_ref[...] = _rms_norm_f32(o_ref[...], gf_ref[...])


def _mlp(x, g, w1, w2, layer, g_final=None, *, tm=512, tf=1024, name):
    m, d = x.shape
    ff = w1.shape[2]
    assert m % tm == 0 and ff % tf == 0
    final_norm = g_final is not None
    blocks = 2 * _nbytes((tm, d), F32) + _nbytes((d, tf), BF16) + _nbytes((tf, d), BF16)
    resident = _nbytes((tm, d), BF16)
    temps = _nbytes((tm, tf), F32) + _nbytes((tm, tf), BF16) + _nbytes((tm, d), F32)
    gain_spec = pl.BlockSpec((1, d), lambda i, f: (0, 0))
    extra_specs, extra_args = ([gain_spec], [g_final.reshape(1, d)]) if final_norm else ([], [])
    return pl.pallas_call(
        functools.partial(_mlp_kernel, final_norm=final_norm),
        grid=(m // tm, ff // tf),
        in_specs=[
            pl.BlockSpec((tm, d), lambda i, f: (i, 0)),
            gain_spec,
            pl.BlockSpec((None, d, tf), lambda i, f: (layer, 0, f)),
            pl.BlockSpec((None, tf, d), lambda i, f: (layer, f, 0)),
        ] + extra_specs,
        out_specs=pl.BlockSpec((tm, d), lambda i, f: (i, 0)),
        out_shape=jax.ShapeDtypeStruct((m, d), F32),
        scratch_shapes=[pltpu.VMEM((tm, d), BF16)],
        compiler_params=pltpu.CompilerParams(
            dimension_semantics=("parallel", "arbitrary"),
            vmem_limit_bytes=_vmem_limit(blocks, resident, temps),
        ),
        name=name,
    )(x, g.reshape(1, d), w1, w2, *extra_args)


def _side_cast_specs(w, n_steps, step_index):
    rows, cols = w.shape
    assert rows % n_steps == 0 and (rows // n_steps) % 16 == 0
    rb = rows // n_steps
    index_map = lambda *ids: (step_index(*ids), 0)
    spec = pl.BlockSpec((rb, cols), index_map)
    return spec, spec, jax.ShapeDtypeStruct((rows, cols), BF16), _nbytes((rb, cols), F32) + _nbytes((rb, cols), BF16)


NA_ROWS_PER_STEP = 32


def _na_bias_table(rpb):
    n_heads = rpb.shape[0]
    col = jnp.arange(GRID_W)
    col_start = jnp.clip(col - NA_COL_WIN // 2, 0, GRID_W - NA_COL_WIN)
    kc = jnp.arange(GRID_W)[None, :]
    valid = (kc >= col_start[:, None]) & (kc < col_start[:, None] + NA_COL_WIN)
    dc = kc - col[:, None] + (NA_COL_WIN - 1)
    onehot = (dc[:, :, None] == jnp.arange(2 * NA_COL_WIN - 1)[None, None, :]).astype(F32)
    by_col = jnp.einsum("hrd,ckd->hcrk", rpb.astype(F32) * LOG2E, onehot, precision=lax.Precision.HIGHEST)
    by_col = jnp.where(valid[None, :, None, :], by_col, NEG_INF)
    per_delta = [by_col[:, :, NA_ROW_WIN - 1 - dl: 2 * NA_ROW_WIN - 1 - dl].reshape(n_heads, GRID_W, -1)
                 for dl in range(NA_ROW_WIN)]
    return jnp.stack(per_delta, axis=1)


def _na_kernel(q_ref, k_ref, v_ref, bias_ref, x_ref, g_ref, wb_ref, w_ref, o_ref, pb_ref, wo_ref, *, rows):
    span = NA_ROW_WIN * GRID_W
    h = _rms_norm_f32(x_ref[...], g_ref[...]).astype(BF16)
    pb_ref[...] = jnp.dot(h, wb_ref[...], preferred_element_type=F32)
    wo_ref[...] = w_ref[...].astype(wo_ref.dtype)

    def one_row(r):
        r0 = jnp.clip(r - NA_ROW_WIN // 2, 0, rows - NA_ROW_WIN)
        q_start = pl.multiple_of(r * GRID_W, GRID_W)
        k_start = pl.multiple_of(r0 * GRID_W, GRID_W)
        q = q_ref[pl.ds(q_start, GRID_W), :]
        k = k_ref[pl.ds(k_start, span), :]
        v = v_ref[pl.ds(k_start, span), :]
        s = _qk_scores(q, k) + bias_ref[r - r0]
        m = jnp.max(s, axis=-1, keepdims=True)
        p = jnp.exp2(s - m).astype(BF16)
        num, den = _pv_with_denominator(p, v)
        o_ref[pl.ds(q_start, GRID_W), :] = (num / den).astype(o_ref.dtype)

    for u in range(NA_ROWS_PER_STEP):
        one_row(pl.program_id(2) * NA_ROWS_PER_STEP + u)


def _na_attention_and_proj(proj, bias_tbl, n_heads, x, g, w_b, side_w, *, name):
    b, t, _ = proj.shape
    rows = t // GRID_W
    assert rows >= NA_ROW_WIN and rows % NA_ROWS_PER_STEP == 0
    n_trips = rows // NA_ROWS_PER_STEP
    n_steps = b * n_heads * n_trips
    m, d = x.shape
    n = w_b.shape[1]
    assert m % n_steps == 0 and (m // n_steps) % 16 == 0
    rb = m // n_steps
    span = NA_ROW_WIN * GRID_W
    step_row = lambda bi, h, tr: (bi * n_heads + h) * n_trips + tr
    step_id = lambda bi, h, tr: (step_row(bi, h, tr), 0)
    side_in, side_out, side_shape, side_bytes = _side_cast_specs(side_w, n_steps, step_row)
    head_blk = lambda off: pl.BlockSpec((None, t, HEAD_DIM), lambda bi, h, tr: (bi, 0, off + h))
    blocks = (4 * _nbytes((t, HEAD_DIM), BF16) + _nbytes((NA_ROW_WIN, GRID_W, span), F32)
              + _nbytes((rb, d), F32) + _nbytes((rb, n), F32) + side_bytes)
    return pl.pallas_call(
        functools.partial(_na_kernel, rows=rows),
        grid=(b, n_heads, n_trips),
        in_specs=[
            head_blk(0), head_blk(n_heads), head_blk(2 * n_heads),
            pl.BlockSpec((None, NA_ROW_WIN, GRID_W, span), lambda bi, h, tr: (h, 0, 0, 0)),
            pl.BlockSpec((rb, d), step_id),
            pl.BlockSpec((1, d), lambda bi, h, tr: (0, 0)),
            pl.BlockSpec((d, n), lambda bi, h, tr: (0, 0), pipeline_mode=pl.Buffered(1)),
            side_in,
        ],
        out_specs=[pl.BlockSpec((None, t, HEAD_DIM), lambda bi, h, tr: (bi, 0, h)), pl.BlockSpec((rb, n), step_id),
                   side_out],
        out_shape=[jax.ShapeDtypeStruct((b, t, n_heads * HEAD_DIM), BF16), jax.ShapeDtypeStruct((m, n), F32),
                   side_shape],
        compiler_params=pltpu.CompilerParams(
            dimension_semantics=("parallel", "parallel", "arbitrary"),
            vmem_limit_bytes=_vmem_limit(blocks, _nbytes((d, n), BF16), 8 << 20),
        ),
        name=name,
    )(proj, proj, proj, bias_tbl, x, g.reshape(1, d), w_b, side_w)


DIL_SUPER = 2048
DIL_QBLK = 128
DIL_BLOCKS_PER_ITER = 16
DIL_MERGE_ROWS = 256


def _band_bias(n_q, span, offset, radius, slope2):
    rel = (lax.broadcasted_iota(jnp.int32, (n_q, 1), 0)
           - lax.broadcasted_iota(jnp.int32, (1, span), 1)) + offset
    dist = jnp.abs(rel)
    return jnp.where(dist <= radius, -slope2 * dist.astype(F32), NEG_INF)


def _dilated_kernel(slope_ref, q_ref, k_ref, v_ref, w_ref, o_ref, wo_ref, bias_ref, acc_ref, m_ref, l_ref, *,
                    seq_len):
    h = pl.program_id(1)
    sb = pl.program_id(2)
    wo_ref[...] = w_ref[...].astype(wo_ref.dtype)

    @pl.when(sb == 0)
    def _():
        slope2 = slope_ref[h] * LOG2E
        for br, (window, dil) in enumerate(DILATED_BRANCHES):
            radius = window // (2 * dil)
            for var in range(3):
                bias_ref[3 * br + var] = _band_bias(DIL_QBLK, DIL_QBLK + 2 * radius, var * radius, radius, slope2 * dil)

    for br, (window, dil) in enumerate(DILATED_BRANCHES):
        radius = window // (2 * dil)
        span = DIL_QBLK + 2 * radius
        sub = seq_len // dil
        blocks_per_res = DIL_SUPER // dil // DIL_QBLK

        def one_block(idx, br=br, dil=dil, radius=radius, span=span, sub=sub, blocks_per_res=blocks_per_res):
            res = idx // blocks_per_res
            c = idx % blocks_per_res
            q_row = res + dil * DIL_QBLK * c
            u0 = sb * (DIL_SUPER // dil) + DIL_QBLK * c
            ws = jnp.clip(u0 - radius, 0, sub - span)
            k_row = res + dil * ws
            if dil == 1:
                q_idx = pl.ds(q_row, DIL_QBLK)
                k_idx = pl.ds(k_row, span)
            else:
                q_idx = pl.ds(q_row, DIL_QBLK, stride=dil)
                k_idx = pl.ds(k_row, span, stride=dil)
            q = q_ref[q_idx, :].astype(BF16)
            k = k_ref[k_idx, :].astype(BF16)
            v = v_ref[k_idx, :].astype(BF16)
            s = _qk_scores(q, k) + bias_ref[3 * br + (u0 - ws) // radius]
            m = jnp.max(s, axis=-1, keepdims=True)
            p = jnp.exp2(s - m).astype(BF16)
            num, den = _pv_with_denominator(p, v)
            acc_ref[br, q_idx, :] = num
            l_ref[br, q_idx, :] = den
            m_ref[br, q_idx, :] = jnp.broadcast_to(m, (DIL_QBLK, HEAD_DIM))

        def body(it, carry, one_block=one_block):
            for u in range(DIL_BLOCKS_PER_ITER):
                one_block(it * DIL_BLOCKS_PER_ITER + u)
            return carry

        lax.fori_loop(0, DIL_SUPER // DIL_QBLK // DIL_BLOCKS_PER_ITER, body, 0)

    n_br = len(DILATED_BRANCHES)

    def merge(it, carry):
        rows = pl.ds(pl.multiple_of(it * DIL_MERGE_ROWS, DIL_MERGE_ROWS), DIL_MERGE_ROWS)
        ms = [m_ref[br, rows, :] for br in range(n_br)]
        m_all = functools.reduce(jnp.maximum, ms)
        ws = [jnp.exp2(m - m_all) for m in ms]
        num = sum(w * acc_ref[br, rows, :] for br, w in enumerate(ws))
        den = sum(w * l_ref[br, rows, :] for br, w in enumerate(ws))
        o_ref[rows, :] = (num / den).astype(o_ref.dtype)
        return carry

    lax.fori_loop(0, DIL_SUPER // DIL_MERGE_ROWS, merge, 0)


def _dilated_attention(proj, n_heads, side_w, *, name):
    b, t, _ = proj.shape
    n_sb = t // DIL_SUPER
    side_in, side_out, side_shape, side_bytes = _side_cast_specs(
        side_w, b * n_heads * n_sb, lambda bi, h, s: (bi * n_heads + h) * n_sb + s)
    n_br = len(DILATED_BRANCHES)
    spans = {DIL_QBLK + window // dil for window, dil in DILATED_BRANCHES}
    assert len(spans) == 1, "bias scratch assumes one key span for all branches"
    span = spans.pop()
    assert t % DIL_SUPER == 0 and DIL_SUPER // DIL_QBLK % DIL_BLOCKS_PER_ITER == 0 and DIL_SUPER % DIL_MERGE_ROWS == 0
    for window, dil in DILATED_BRANCHES:
        assert DIL_SUPER % (dil * DIL_QBLK) == 0 and t // dil >= span and window // (2 * dil) <= DIL_QBLK
    slopes = 2.0 ** (-8.0 * jnp.arange(1, n_heads + 1, dtype=F32) / n_heads)
    kv_blk = lambda off: pl.BlockSpec((None, t, HEAD_DIM), lambda bi, h, s: (bi, 0, off + h))
    blocks = (2 * _nbytes((t, HEAD_DIM), F32) + _nbytes((DIL_SUPER, HEAD_DIM), F32)
              + _nbytes((DIL_SUPER, HEAD_DIM), BF16) + side_bytes)
    stats = pltpu.VMEM((n_br, DIL_SUPER, HEAD_DIM), F32)
    scratch = 3 * _nbytes((n_br, DIL_SUPER, HEAD_DIM), F32) + _nbytes((3 * n_br, DIL_QBLK, span), F32)
    return pl.pallas_call(
        functools.partial(_dilated_kernel, seq_len=t),
        grid=(b, n_heads, t // DIL_SUPER),
        in_specs=[
            pl.BlockSpec(memory_space=pltpu.SMEM),
            pl.BlockSpec((None, DIL_SUPER, HEAD_DIM), lambda bi, h, s: (bi, s, h)),
            kv_blk(n_heads), kv_blk(2 * n_heads), side_in,
        ],
        out_specs=[pl.BlockSpec((None, DIL_SUPER, HEAD_DIM), lambda bi, h, s: (bi, s, h)), side_out],
        out_shape=[jax.ShapeDtypeStruct((b, t, n_heads * HEAD_DIM), BF16), side_shape],
        scratch_shapes=[pltpu.VMEM((3 * n_br, DIL_QBLK, span), F32), stats, stats, stats],
        compiler_params=pltpu.CompilerParams(
            dimension_semantics=("parallel", "parallel", "arbitrary"),
            vmem_limit_bytes=_vmem_limit(blocks, scratch, 8 << 20),
        ),
        name=name,
    )(slopes, proj, proj, proj, side_w)


SWA_QBLK = 256
SWA_BLOCKS_PER_STEP = 8


def _swa_kernel(slope_ref, sink_ref, q_ref, k_ref, v_ref, o_ref, bias_ref, *, seq_len, group):
    kvh = pl.program_id(1)
    span = SWA_QBLK + 2 * C_RADIUS

    @pl.when(pl.program_id(2) == 0)
    def _():
        for g in range(group):
            slope2 = slope_ref[kvh * group + g] * LOG2E
            for var in range(3):
                bias_ref[3 * g + var] = _band_bias(SWA_QBLK, span, var * C_RADIUS, C_RADIUS, slope2)

    for blk in range(SWA_BLOCKS_PER_STEP):
        rows = slice(blk * SWA_QBLK, (blk + 1) * SWA_QBLK)
        s0 = (pl.program_id(2) * SWA_BLOCKS_PER_STEP + blk) * SWA_QBLK
        ks = pl.multiple_of(jnp.clip(s0 - C_RADIUS, 0, seq_len - span), C_RADIUS)
        var = (s0 - ks) // C_RADIUS
        k = k_ref[pl.ds(ks, span), :]
        v = v_ref[pl.ds(ks, span), :]
        for g in range(group):
            cols = slice(g * HEAD_DIM, (g + 1) * HEAD_DIM)
            s = _qk_scores(q_ref[rows, cols], k) + bias_ref[3 * g + var]
            sink2 = sink_ref[kvh * group + g] * LOG2E
            m = jnp.maximum(jnp.max(s, axis=-1, keepdims=True), sink2)
            p = jnp.exp2(s - m).astype(BF16)
            num, den = _pv_with_denominator(p, v)
            o = num / (den + jnp.exp2(sink2 - m))
            o_ref[rows, cols] = o.astype(o_ref.dtype)


def _swa_attention(proj, sink, n_heads, n_kv, *, name):
    b, t, _ = proj.shape
    group = n_heads // n_kv
    span = SWA_QBLK + 2 * C_RADIUS
    step_rows = SWA_QBLK * SWA_BLOCKS_PER_STEP
    assert t % step_rows == 0 and t >= span and C_RADIUS <= SWA_QBLK and SWA_QBLK % C_RADIUS == 0
    slopes = 2.0 ** (-8.0 * jnp.arange(1, n_heads + 1, dtype=F32) / n_heads)
    kv_blk = lambda off: pl.BlockSpec((None, t, HEAD_DIM), lambda bi, kh, s: (bi, 0, off + kh))
    q_blk = pl.BlockSpec((None, step_rows, group * HEAD_DIM), lambda bi, kh, s: (bi, s, kh))
    blocks = 2 * _nbytes((t, HEAD_DIM), BF16) + 2 * _nbytes((step_rows, group * HEAD_DIM), BF16)
    bias_bytes = _nbytes((3 * group, SWA_QBLK, span), F32)
    return pl.pallas_call(
        functools.partial(_swa_kernel, seq_len=t, group=group),
        grid=(b, n_kv, t // step_rows),
        in_specs=[
            pl.BlockSpec(memory_space=pltpu.SMEM),
            pl.BlockSpec(memory_space=pltpu.SMEM),
            q_blk, kv_blk(n_heads), kv_blk(n_heads + n_kv),
        ],
        out_specs=q_blk,
        out_shape=jax.ShapeDtypeStruct((b, t, n_heads * HEAD_DIM), BF16),
        scratch_shapes=[pltpu.VMEM((3 * group, SWA_QBLK, span), F32)],
        compiler_params=pltpu.CompilerParams(
            dimension_semantics=("parallel", "parallel", "arbitrary"),
            vmem_limit_bytes=_vmem_limit(blocks, bias_bytes, 8 << 20),
        ),
        name=name,
    )(slopes, sink.astype(F32), proj, proj, proj)


def _scaled_bf16(w, n_q_cols):
    col = jnp.arange(w.shape[1])
    scale = jnp.where(col < n_q_cols, LOG2E * HEAD_DIM ** -0.5, 1.0).astype(F32)
    return (w * scale[None, :]).astype(BF16)


def kernel(x, attn_norm, mlp_norm, w_mlp_in, w_mlp_out, even_w_in, even_rpb, even_w_out,
           odd_w_qkv, odd_sink, odd_w_out, final_norm):
    b, t, d = x.shape
    depth = attn_norm.shape[0]
    n_heads_a = even_rpb.shape[1]
    wa = n_heads_a * HEAD_DIM
    n_heads_b = (even_w_in.shape[2] - 3 * wa) // (3 * HEAD_DIM)
    n_heads_c = odd_sink.shape[1]
    n_kv_c = (odd_w_qkv.shape[2] // HEAD_DIM - n_heads_c) // 2
    xf = x.reshape(b * t, d)
    w1_bf16 = w2_bf16 = None

    for i in range(depth):
        j = i // 2
        if i % 2 == 0:
            w_in = even_w_in[j]
            proj_a = _norm_matmul(xf, attn_norm[i], _scaled_bf16(w_in[:, :3 * wa], wa), BF16, name=f"l{i}_proj_a")
            w1_pair, w2_pair = w_mlp_in[i:i + 2], w_mlp_out[i:i + 2]
            o_a, proj_b, w1_bf16 = _na_attention_and_proj(
                proj_a.reshape(b, t, -1), _na_bias_table(even_rpb[j]), n_heads_a, xf, attn_norm[i],
                _scaled_bf16(w_in[:, 3 * wa:], n_heads_b * HEAD_DIM), w1_pair.reshape(-1, w1_pair.shape[-1]),
                name=f"l{i}_na_proj_b")
            o_b, w2_bf16 = _dilated_attention(proj_b.reshape(b, t, -1), n_heads_b,
                                              w2_pair.reshape(-1, w2_pair.shape[-1]), name=f"l{i}_dilated")
            w1_bf16, w2_bf16 = w1_bf16.reshape(w1_pair.shape), w2_bf16.reshape(w2_pair.shape)
            xf = _proj_residual(o_a.reshape(b * t, -1), 0, o_b.reshape(b * t, -1), 0,
                                even_w_out[j].astype(BF16), xf, name=f"l{i}_out")
        else:
            proj_c = _norm_matmul(xf, attn_norm[i], _scaled_bf16(odd_w_qkv[j], n_heads_c * HEAD_DIM), BF16,
                                  name=f"l{i}_proj_c")
            o_c = _swa_attention(proj_c.reshape(b, t, -1), odd_sink[j], n_heads_c, n_kv_c, name=f"l{i}_swa")
            o_c = o_c.reshape(b * t, -1)
            xf = _proj_residual(o_c, 0, o_c, 1, odd_w_out[j].astype(BF16), xf, name=f"l{i}_out")
        g_final = final_norm if i == depth - 1 else None
        xf = _mlp(xf, mlp_norm[i], w1_bf16, w2_bf16, i % 2, g_final, name=f"l{i}_mlp")
    return xf.reshape(b, t, d)
```

```python
import functools
import math

import jax
import jax.numpy as jnp
from jax import lax
from jax.experimental import pallas as pl
from jax.experimental.pallas import tpu as pltpu

HEAD_DIM = 128
GRID_W = 64
NA_ROW_WIN = 8
NA_COL_WIN = 16
DILATED_BRANCHES = ((128, 1), (512, 4), (2048, 16))
C_RADIUS = 128
NORM_EPS = 1e-6
NEG_INF = -1e30
LOG2E = math.log2(math.e)

V7X_VMEM_LIMIT_CAP = 56 * 1024 * 1024
BF16 = jnp.bfloat16
F32 = jnp.float32


def _vmem_limit(block_bytes, scratch_bytes, temp_bytes):
    need = 2 * block_bytes + scratch_bytes + temp_bytes + (2 << 20)
    return int(min(max(need, 16 << 20), V7X_VMEM_LIMIT_CAP))


def _nbytes(shape, dtype):
    n = 1
    for s in shape:
        n *= s
    return n * jnp.dtype(dtype).itemsize


def _rms_norm_f32(x, g):
    y = x * lax.rsqrt(jnp.mean(x * x, axis=-1, keepdims=True) + NORM_EPS)
    return y * g


def _qk_scores(q, k):
    return lax.dot_general(q, k, (((1,), (1,)), ((), ())), preferred_element_type=F32)


def _pv_with_denominator(p, v):
    v_ext = jnp.concatenate([v, jnp.ones_like(v)], axis=1)
    pv = jnp.dot(p, v_ext, preferred_element_type=F32)
    return pv[:, :HEAD_DIM], pv[:, HEAD_DIM:]


def _norm_matmul_kernel(x_ref, g_ref, w_ref, o_ref, h_ref):
    @pl.when(pl.program_id(1) == 0)
    def _():
        h_ref[...] = _rms_norm_f32(x_ref[...], g_ref[...]).astype(h_ref.dtype)

    o_ref[...] = jnp.dot(h_ref[...], w_ref[...], preferred_element_type=F32).astype(o_ref.dtype)


def _norm_matmul(x, g, w, out_dtype, *, tm=1024, tn=1024, name):
    m, d = x.shape
    n = w.shape[1]
    assert m % tm == 0 and n % tn == 0
    blocks = _nbytes((tm, d), F32) + _nbytes((d, tn), BF16) + _nbytes((tm, tn), out_dtype)
    return pl.pallas_call(
        _norm_matmul_kernel,
        grid=(m // tm, n // tn),
        in_specs=[
            pl.BlockSpec((tm, d), lambda i, j: (i, 0)),
            pl.BlockSpec((1, d), lambda i, j: (0, 0)),
            pl.BlockSpec((d, tn), lambda i, j: (0, j)),
        ],
        out_specs=pl.BlockSpec((tm, tn), lambda i, j: (i, j)),
        out_shape=jax.ShapeDtypeStruct((m, n), out_dtype),
        scratch_shapes=[pltpu.VMEM((tm, d), BF16)],
        compiler_params=pltpu.CompilerParams(
            dimension_semantics=("parallel", "arbitrary"),
            vmem_limit_bytes=_vmem_limit(blocks, _nbytes((tm, d), BF16), _nbytes((tm, d), F32) + _nbytes((tm, tn), F32)),
        ),
        name=name,
    )(x, g.reshape(1, d), w)


def _proj_residual_kernel(a_ref, b_ref, wa_ref, wb_ref, x_ref, o_ref):
    acc = jnp.dot(a_ref[...], wa_ref[...], preferred_element_type=F32)
    acc = acc + jnp.dot(b_ref[...], wb_ref[...], preferred_element_type=F32)
    o_ref[...] = x_ref[...] + acc


def _proj_residual(a, a_col, b, b_col, w, x, *, tm=1024, tn=1024, name):
    m, d = x.shape
    kh = w.shape[0] // 2
    assert m % tm == 0 and d % tn == 0
    blocks = 2 * _nbytes((tm, kh), BF16) + 2 * _nbytes((kh, tn), BF16) + 2 * _nbytes((tm, tn), F32)
    return pl.pallas_call(
        _proj_residual_kernel,
        grid=(m // tm, d // tn),
        in_specs=[
            pl.BlockSpec((tm, kh), lambda i, j: (i, a_col)),
            pl.BlockSpec((tm, kh), lambda i, j: (i, b_col)),
            pl.BlockSpec((kh, tn), lambda i, j: (0, j)),
            pl.BlockSpec((kh, tn), lambda i, j: (1, j)),
            pl.BlockSpec((tm, tn), lambda i, j: (i, j)),
        ],
        out_specs=pl.BlockSpec((tm, tn), lambda i, j: (i, j)),
        out_shape=jax.ShapeDtypeStruct((m, d), F32),
        compiler_params=pltpu.CompilerParams(
            dimension_semantics=("parallel", "parallel"),
            vmem_limit_bytes=_vmem_limit(blocks, 0, 2 * _nbytes((tm, tn), F32)),
        ),
        name=name,
    )(a, b, w, w, x)


def _mlp_kernel(x_ref, g_ref, w1_ref, w2_ref, *rest, final_norm):
    gf_ref = rest[0] if final_norm else None
    o_ref, h_ref = rest[-2:]
    f = pl.program_id(1)

    @pl.when(f == 0)
    def _():
        x = x_ref[...]
        h_ref[...] = _rms_norm_f32(x, g_ref[...]).astype(h_ref.dtype)
        o_ref[...] = x

    a = jnp.maximum(jnp.dot(h_ref[...], w1_ref[...], preferred_element_type=F32), 0.0)
    o_ref[...] += jnp.dot((a * a).astype(BF16), w2_ref[...], preferred_element_type=F32)

    if final_norm:
        @pl.when(f == pl.num_programs(1) - 1)
        def _():
            o_ref[...] = _rms_norm_f32(o_ref[...], gf_ref[...])


def _mlp(x, g, w1, w2, layer, g_final=None, *, tm=512, tf=1024, name):
    m, d = x.shape
    ff = w1.shape[2]
    assert m % tm == 0 and ff % tf == 0
    final_norm = g_final is not None
    blocks = 2 * _nbytes((tm, d), F32) + _nbytes((d, tf), BF16) + _nbytes((tf, d), BF16)
    resident = _nbytes((tm, d), BF16)
    temps = _nbytes((tm, tf), F32) + _nbytes((tm, tf), BF16) + _nbytes((tm, d), F32)
    gain_spec = pl.BlockSpec((1, d), lambda i, f: (0, 0))
    extra_specs, extra_args = ([gain_spec], [g_final.reshape(1, d)]) if final_norm else ([], [])
    return pl.pallas_call(
        functools.partial(_mlp_kernel, final_norm=final_norm),
        grid=(m // tm, ff // tf),
        in_specs=[
            pl.BlockSpec((tm, d), lambda i, f: (i, 0)),
            gain_spec,
            pl.BlockSpec((None, d, tf), lambda i, f: (layer, 0, f)),
            pl.BlockSpec((None, tf, d), lambda i, f: (layer, f, 0)),
        ] + extra_specs,
        out_specs=pl.BlockSpec((tm, d), lambda i, f: (i, 0)),
        out_shape=jax.ShapeDtypeStruct((m, d), F32),
        scratch_shapes=[pltpu.VMEM((tm, d), BF16)],
        compiler_params=pltpu.CompilerParams(
            dimension_semantics=("parallel", "arbitrary"),
            vmem_limit_bytes=_vmem_limit(blocks, resident, temps),
        ),
        name=name,
    )(x, g.reshape(1, d), w1, w2, *extra_args)


def _side_cast_specs(w, n_steps, step_index):
    rows, cols = w.shape
    assert rows % n_steps == 0 and (rows // n_steps) % 16 == 0
    rb = rows // n_steps
    index_map = lambda *ids: (step_index(*ids), 0)
    spec = pl.BlockSpec((rb, cols), index_map)
    return spec, spec, jax.ShapeDtypeStruct((rows, cols), BF16), _nbytes((rb, cols), F32) + _nbytes((rb, cols), BF16)


NA_ROWS_PER_STEP = 32


def _na_bias_table(rpb):
    n_heads = rpb.shape[0]
    col = jnp.arange(GRID_W)
    col_start = jnp.clip(col - NA_COL_WIN // 2, 0, GRID_W - NA_COL_WIN)
    kc = jnp.arange(GRID_W)[None, :]
    valid = (kc >= col_start[:, None]) & (kc < col_start[:, None] + NA_COL_WIN)
    dc = kc - col[:, None] + (NA_COL_WIN - 1)
    onehot = (dc[:, :, None] == jnp.arange(2 * NA_COL_WIN - 1)[None, None, :]).astype(F32)
    by_col = jnp.einsum("hrd,ckd->hcrk", rpb.astype(F32) * LOG2E, onehot, precision=lax.Precision.HIGHEST)
    by_col = jnp.where(valid[None, :, None, :], by_col, NEG_INF)
    per_delta = [by_col[:, :, NA_ROW_WIN - 1 - dl: 2 * NA_ROW_WIN - 1 - dl].reshape(n_heads, GRID_W, -1)
                 for dl in range(NA_ROW_WIN)]
    return jnp.stack(per_delta, axis=1)


def _na_kernel(q_ref, k_ref, v_ref, bias_ref, x_ref, g_ref, wb_ref, w_ref, o_ref, pb_ref, wo_ref, *, rows):
    span = NA_ROW_WIN * GRID_W
    h = _rms_norm_f32(x_ref[...], g_ref[...]).astype(BF16)
    pb_ref[...] = jnp.dot(h, wb_ref[...], preferred_element_type=F32)
    wo_ref[...] = w_ref[...].astype(wo_ref.dtype)

    def one_row(r):
        r0 = jnp.clip(r - NA_ROW_WIN // 2, 0, rows - NA_ROW_WIN)
        q_start = pl.multiple_of(r * GRID_W, GRID_W)
        k_start = pl.multiple_of(r0 * GRID_W, GRID_W)
        q = q_ref[pl.ds(q_start, GRID_W), :]
        k = k_ref[pl.ds(k_start, span), :]
        v = v_ref[pl.ds(k_start, span), :]
        s = _qk_scores(q, k) + bias_ref[r - r0]
        m = jnp.max(s, axis=-1, keepdims=True)
        p = jnp.exp2(s - m).astype(BF16)
        num, den = _pv_with_denominator(p, v)
        o_ref[pl.ds(q_start, GRID_W), :] = (num / den).astype(o_ref.dtype)

    for u in range(NA_ROWS_PER_STEP):
        one_row(pl.program_id(2) * NA_ROWS_PER_STEP + u)


def _na_attention_and_proj(proj, bias_tbl, n_heads, x, g, w_b, side_w, *, name):
    b, t, _ = proj.shape
    rows = t // GRID_W
    assert rows >= NA_ROW_WIN and rows % NA_ROWS_PER_STEP == 0
    n_trips = rows // NA_ROWS_PER_STEP
    n_steps = b * n_heads * n_trips
    m, d = x.shape
    n = w_b.shape[1]
    n_slabs = n_heads * n_trips
    nb = n // b
    assert m % n_slabs == 0 and (m // n_slabs) % 16 == 0 and n % b == 0 and nb % 128 == 0
    rb = m // n_slabs
    span = NA_ROW_WIN * GRID_W
    slab_row = lambda bi, h, tr: h * n_trips + tr
    side_in, side_out, side_shape, side_bytes = _side_cast_specs(
        side_w, n_steps, lambda bi, h, tr: (bi * n_heads + h) * n_trips + tr)
    head_blk = lambda off: pl.BlockSpec((None, t, HEAD_DIM), lambda bi, h, tr: (bi, 0, off + h))
    blocks = (4 * _nbytes((t, HEAD_DIM), BF16) + _nbytes((NA_ROW_WIN, GRID_W, span), F32)
              + _nbytes((rb, d), F32) + _nbytes((rb, nb), F32) + side_bytes)
    return pl.pallas_call(
        functools.partial(_na_kernel, rows=rows),
        grid=(b, n_heads, n_trips),
        in_specs=[
            head_blk(0), head_blk(n_heads), head_blk(2 * n_heads),
            pl.BlockSpec((None, NA_ROW_WIN, GRID_W, span), lambda bi, h, tr: (h, 0, 0, 0)),
            pl.BlockSpec((rb, d), lambda bi, h, tr: (slab_row(bi, h, tr), 0)),
            pl.BlockSpec((1, d), lambda bi, h, tr: (0, 0)),
            pl.BlockSpec((d, nb), lambda bi, h, tr: (0, bi), pipeline_mode=pl.Buffered(1)),
            side_in,
        ],
        out_specs=[pl.BlockSpec((None, t, HEAD_DIM), lambda bi, h, tr: (bi, 0, h)),
                   pl.BlockSpec((rb, nb), lambda bi, h, tr: (slab_row(bi, h, tr), bi)), side_out],
        out_shape=[jax.ShapeDtypeStruct((b, t, n_heads * HEAD_DIM), BF16), jax.ShapeDtypeStruct((m, n), F32),
                   side_shape],
        compiler_params=pltpu.CompilerParams(
            dimension_semantics=("arbitrary", "parallel", "arbitrary"),
            vmem_limit_bytes=_vmem_limit(blocks, _nbytes((d, nb), BF16), 8 << 20),
        ),
        name=name,
    )(proj, proj, proj, bias_tbl, x, g.reshape(1, d), w_b, side_w)


DIL_SUPER = 2048
DIL_QBLK = 128
DIL_BLOCKS_PER_ITER = 16
DIL_MERGE_ROWS = 256


def _band_bias(n_q, span, offset, radius, slope2):
    rel = (lax.broadcasted_iota(jnp.int32, (n_q, 1), 0)
           - lax.broadcasted_iota(jnp.int32, (1, span), 1)) + offset
    dist = jnp.abs(rel)
    return jnp.where(dist <= radius, -slope2 * dist.astype(F32), NEG_INF)


def _dilated_kernel(slope_ref, q_ref, k_ref, v_ref, w_ref, o_ref, wo_ref, bias_ref, acc_ref, m_ref, l_ref, *,
                    seq_len):
    h = pl.program_id(1)
    sb = pl.program_id(2)
    wo_ref[...] = w_ref[...].astype(wo_ref.dtype)

    @pl.when(sb == 0)
    def _():
        slope2 = slope_ref[h] * LOG2E
        for br, (window, dil) in enumerate(DILATED_BRANCHES):
            radius = window // (2 * dil)
            for var in range(3):
                bias_ref[3 * br + var] = _band_bias(DIL_QBLK, DIL_QBLK + 2 * radius, var * radius, radius, slope2 * dil)

    for br, (window, dil) in enumerate(DILATED_BRANCHES):
        radius = window // (2 * dil)
        span = DIL_QBLK + 2 * radius
        sub = seq_len // dil
        blocks_per_res = DIL_SUPER // dil // DIL_QBLK

        def one_block(idx, br=br, dil=dil, radius=radius, span=span, sub=sub, blocks_per_res=blocks_per_res):
            res = idx // blocks_per_res
            c = idx % blocks_per_res
            q_row = res + dil * DIL_QBLK * c
            u0 = sb * (DIL_SUPER // dil) + DIL_QBLK * c
            ws = jnp.clip(u0 - radius, 0, sub - span)
            k_row = res + dil * ws
            if dil == 1:
                q_idx = pl.ds(q_row, DIL_QBLK)
                k_idx = pl.ds(k_row, span)
            else:
                q_idx = pl.ds(q_row, DIL_QBLK, stride=dil)
                k_idx = pl.ds(k_row, span, stride=dil)
            q = q_ref[q_idx, :].astype(BF16)
            k = k_ref[k_idx, :].astype(BF16)
            v = v_ref[k_idx, :].astype(BF16)
            s = _qk_scores(q, k) + bias_ref[3 * br + (u0 - ws) // radius]
            m = jnp.max(s, axis=-1, keepdims=True)
            p = jnp.exp2(s - m).astype(BF16)
            num, den = _pv_with_denominator(p, v)
            acc_ref[br, q_idx, :] = num
            l_ref[br, q_idx, :] = den
            m_ref[br, q_idx, :] = jnp.broadcast_to(m, (DIL_QBLK, HEAD_DIM))

        def body(it, carry, one_block=one_block):
            for u in range(DIL_BLOCKS_PER_ITER):
                one_block(it * DIL_BLOCKS_PER_ITER + u)
            return carry

        lax.fori_loop(0, DIL_SUPER // DIL_QBLK // DIL_BLOCKS_PER_ITER, body, 0)

    n_br = len(DILATED_BRANCHES)

    def merge(it, carry):
        rows = pl.ds(pl.multiple_of(it * DIL_MERGE_ROWS, DIL_MERGE_ROWS), DIL_MERGE_ROWS)
        ms = [m_ref[br, rows, :] for br in range(n_br)]
        m_all = functools.reduce(jnp.maximum, ms)
        ws = [jnp.exp2(m - m_all) for m in ms]
        num = sum(w * acc_ref[br, rows, :] for br, w in enumerate(ws))
        den = sum(w * l_ref[br, rows, :] for br, w in enumerate(ws))
        o_ref[rows, :] = (num / den).astype(o_ref.dtype)
        return carry

    lax.fori_loop(0, DIL_SUPER // DIL_MERGE_ROWS, merge, 0)


def _dilated_attention(proj, n_heads, side_w, *, name):
    b, t, _ = proj.shape
    n_sb = t // DIL_SUPER
    side_in, side_out, side_shape, side_bytes = _side_cast_specs(
        side_w, b * n_heads * n_sb, lambda bi, h, s: (bi * n_heads + h) * n_sb + s)
    n_br = len(DILATED_BRANCHES)
    spans = {DIL_QBLK + window // dil for window, dil in DILATED_BRANCHES}
    assert len(spans) == 1, "bias scratch assumes one key span for all branches"
    span = spans.pop()
    assert t % DIL_SUPER == 0 and DIL_SUPER // DIL_QBLK % DIL_BLOCKS_PER_ITER == 0 and DIL_SUPER % DIL_MERGE_ROWS == 0
    for window, dil in DILATED_BRANCHES:
        assert DIL_SUPER % (dil * DIL_QBLK) == 0 and t // dil >= span and window // (2 * dil) <= DIL_QBLK
    slopes = 2.0 ** (-8.0 * jnp.arange(1, n_heads + 1, dtype=F32) / n_heads)
    kv_blk = lambda off: pl.BlockSpec((None, t, HEAD_DIM), lambda bi, h, s: (bi, 0, off + h))
    blocks = (2 * _nbytes((t, HEAD_DIM), F32) + _nbytes((DIL_SUPER, HEAD_DIM), F32)
              + _nbytes((DIL_SUPER, HEAD_DIM), BF16) + side_bytes)
    stats = pltpu.VMEM((n_br, DIL_SUPER, HEAD_DIM), F32)
    scratch = 3 * _nbytes((n_br, DIL_SUPER, HEAD_DIM), F32) + _nbytes((3 * n_br, DIL_QBLK, span), F32)
    return pl.pallas_call(
        functools.partial(_dilated_kernel, seq_len=t),
        grid=(b, n_heads, t // DIL_SUPER),
        in_specs=[
            pl.BlockSpec(memory_space=pltpu.SMEM),
            pl.BlockSpec((None, DIL_SUPER, HEAD_DIM), lambda bi, h, s: (bi, s, h)),
            kv_blk(n_heads), kv_blk(2 * n_heads), side_in,
        ],
        out_specs=[pl.BlockSpec((None, DIL_SUPER, HEAD_DIM), lambda bi, h, s: (bi, s, h)), side_out],
        out_shape=[jax.ShapeDtypeStruct((b, t, n_heads * HEAD_DIM), BF16), side_shape],
        scratch_shapes=[pltpu.VMEM((3 * n_br, DIL_QBLK, span), F32), stats, stats, stats],
        compiler_params=pltpu.CompilerParams(
            dimension_semantics=("parallel", "parallel", "arbitrary"),
            vmem_limit_bytes=_vmem_limit(blocks, scratch, 8 << 20),
        ),
        name=name,
    )(slopes, proj, proj, proj, side_w)


SWA_QBLK = 256
SWA_BLOCKS_PER_STEP = 8


def _swa_kernel(slope_ref, sink_ref, q_ref, k_ref, v_ref, o_ref, bias_ref, *, seq_len, group):
    kvh = pl.program_id(1)
    span = SWA_QBLK + 2 * C_RADIUS

    @pl.when(pl.program_id(2) == 0)
    def _():
        for g in range(group):
            slope2 = slope_ref[kvh * group + g] * LOG2E
            for var in range(3):
                bias_ref[3 * g + var] = _band_bias(SWA_QBLK, span, var * C_RADIUS, C_RADIUS, slope2)

    for blk in range(SWA_BLOCKS_PER_STEP):
        rows = slice(blk * SWA_QBLK, (blk + 1) * SWA_QBLK)
        s0 = (pl.program_id(2) * SWA_BLOCKS_PER_STEP + blk) * SWA_QBLK
        ks = pl.multiple_of(jnp.clip(s0 - C_RADIUS, 0, seq_len - span), C_RADIUS)
        var = (s0 - ks) // C_RADIUS
        k = k_ref[pl.ds(ks, span), :]
        v = v_ref[pl.ds(ks, span), :]
        for g in range(group):
            cols = slice(g * HEAD_DIM, (g + 1) * HEAD_DIM)
            s = _qk_scores(q_ref[rows, cols], k) + bias_ref[3 * g + var]
            sink2 = sink_ref[kvh * group + g] * LOG2E
            m = jnp.maximum(jnp.max(s, axis=-1, keepdims=True), sink2)
            p = jnp.exp2(s - m).astype(BF16)
            num, den = _pv_with_denominator(p, v)
            o = num / (den + jnp.exp2(sink2 - m))
            o_ref[rows, cols] = o.astype(o_ref.dtype)


def _swa_attention(proj, sink, n_heads, n_kv, *, name):
    b, t, _ = proj.shape
    group = n_heads // n_kv
    span = SWA_QBLK + 2 * C_RADIUS
    step_rows = SWA_QBLK * SWA_BLOCKS_PER_STEP
    assert t % step_rows == 0 and t >= span and C_RADIUS <= SWA_QBLK and SWA_QBLK % C_RADIUS == 0
    slopes = 2.0 ** (-8.0 * jnp.arange(1, n_heads + 1, dtype=F32) / n_heads)
    kv_blk = lambda off: pl.BlockSpec((None, t, HEAD_DIM), lambda bi, kh, s: (bi, 0, off + kh))
    q_blk = pl.BlockSpec((None, step_rows, group * HEAD_DIM), lambda bi, kh, s: (bi, s, kh))
    blocks = 2 * _nbytes((t, HEAD_DIM), BF16) + 2 * _nbytes((step_rows, group * HEAD_DIM), BF16)
    bias_bytes = _nbytes((3 * group, SWA_QBLK, span), F32)
    return pl.pallas_call(
        functools.partial(_swa_kernel, seq_len=t, group=group),
        grid=(b, n_kv, t // step_rows),
        in_specs=[
            pl.BlockSpec(memory_space=pltpu.SMEM),
            pl.BlockSpec(memory_space=pltpu.SMEM),
            q_blk, kv_blk(n_heads), kv_blk(n_heads + n_kv),
        ],
        out_specs=q_blk,
        out_shape=jax.ShapeDtypeStruct((b, t, n_heads * HEAD_DIM), BF16),
        scratch_shapes=[pltpu.VMEM((3 * group, SWA_QBLK, span), F32)],
        compiler_params=pltpu.CompilerParams(
            dimension_semantics=("parallel", "parallel", "arbitrary"),
            vmem_limit_bytes=_vmem_limit(blocks, bias_bytes, 8 << 20),
        ),
        name=name,
    )(slopes, sink.astype(F32), proj, proj, proj)


def _scaled_bf16(w, n_q_cols):
    col = jnp.arange(w.shape[1])
    scale = jnp.where(col < n_q_cols, LOG2E * HEAD_DIM ** -0.5, 1.0).astype(F32)
    return (w * scale[None, :]).astype(BF16)


def kernel(x, attn_norm, mlp_norm, w_mlp_in, w_mlp_out, even_w_in, even_rpb, even_w_out,
           odd_w_qkv, odd_sink, odd_w_out, final_norm):
    b, t, d = x.shape
    depth = attn_norm.shape[0]
    n_heads_a = even_rpb.shape[1]
    wa = n_heads_a * HEAD_DIM
    n_heads_b = (even_w_in.shape[2] - 3 * wa) // (3 * HEAD_DIM)
    n_heads_c = odd_sink.shape[1]
    n_kv_c = (odd_w_qkv.shape[2] // HEAD_DIM - n_heads_c) // 2
    xf = x.reshape(b * t, d)
    w1_bf16 = w2_bf16 = None

    for i in range(depth):
        j = i // 2
        if i % 2 == 0:
            w_in = even_w_in[j]
            proj_a = _norm_matmul(xf, attn_norm[i], _scaled_bf16(w_in[:, :3 * wa], wa), BF16, name=f"l{i}_proj_a")
            w1_pair, w2_pair = w_mlp_in[i:i + 2], w_mlp_out[i:i + 2]
            o_a, proj_b, w1_bf16 = _na_attention_and_proj(
                proj_a.reshape(b, t, -1), _na_bias_table(even_rpb[j]), n_heads_a, xf, attn_norm[i],
                _scaled_bf16(w_in[:, 3 * wa:], n_heads_b * HEAD_DIM), w1_pair.reshape(-1, w1_pair.shape[-1]),
                name=f"l{i}_na_proj_b")
            o_b, w2_bf16 = _dilated_attention(proj_b.reshape(b, t, -1), n_heads_b,
                                              w2_pair.reshape(-1, w2_pair.shape[-1]), name=f"l{i}_dilated")
            w1_bf16, w2_bf16 = w1_bf16.reshape(w1_pair.shape), w2_bf16.reshape(w2_pair.shape)
            xf = _proj_residual(o_a.reshape(b * t, -1), 0, o_b.reshape(b * t, -1), 0,
                                even_w_out[j].astype(BF16), xf, name=f"l{i}_out")
        else:
            proj_c = _norm_matmul(xf, attn_norm[i], _scaled_bf16(odd_w_qkv[j], n_heads_c * HEAD_DIM), BF16,
                                  name=f"l{i}_proj_c")
            o_c = _swa_attention(proj_c.reshape(b, t, -1), odd_sink[j], n_heads_c, n_kv_c, name=f"l{i}_swa")
            o_c = o_c.reshape(b * t, -1)
            xf = _proj_residual(o_c, 0, o_c, 1, odd_w_out[j].astype(BF16), xf, name=f"l{i}_out")
        g_final = final_norm if i == depth - 1 else None
        xf = _mlp(xf, mlp_norm[i], w1_bf16, w2_bf16, i % 2, g_final, name=f"l{i}_mlp")
    return xf.reshape(b, t, d)
```

```python
import functools
import math

import jax
import jax.numpy as jnp
from jax import lax
from jax.experimental import pallas as pl
from jax.experimental.pallas import tpu as pltpu

HEAD_DIM = 128
GRID_W = 64
NA_ROW_WIN = 8
NA_COL_WIN = 16
DILATED_BRANCHES = ((128, 1), (512, 4), (2048, 16))
C_RADIUS = 128
NORM_EPS = 1e-6
NEG_INF = -1e30
LOG2E = math.log2(math.e)

V7X_VMEM_LIMIT_CAP = 56 * 1024 * 1024
BF16 = jnp.bfloat16
F32 = jnp.float32


def _vmem_limit(block_bytes, scratch_bytes, temp_bytes):
    need = 2 * block_bytes + scratch_bytes + temp_bytes + (2 << 20)
    return int(min(max(need, 16 << 20), V7X_VMEM_LIMIT_CAP))


def _nbytes(shape, dtype):
    n = 1
    for s in shape:
        n *= s
    return n * jnp.dtype(dtype).itemsize


def _rms_norm_f32(x, g):
    y = x * lax.rsqrt(jnp.mean(x * x, axis=-1, keepdims=True) + NORM_EPS)
    return y * g


def _qk_scores(q, k):
    return lax.dot_general(q, k, (((1,), (1,)), ((), ())), preferred_element_type=F32)


def _pv_with_denominator(p, v):
    v_ext = jnp.concatenate([v, jnp.ones_like(v)], axis=1)
    pv = jnp.dot(p, v_ext, preferred_element_type=F32)
    return pv[:, :HEAD_DIM], pv[:, HEAD_DIM:]


def _norm_matmul_kernel(x_ref, g_ref, w_ref, o_ref):
    h = _rms_norm_f32(x_ref[...], g_ref[...]).astype(BF16)
    o_ref[...] = jnp.dot(h, w_ref[...], preferred_element_type=F32).astype(o_ref.dtype)


def _norm_matmul(x, g, w, out_dtype, *, tm=512, name):
    m, d = x.shape
    n = w.shape[1]
    assert m % tm == 0
    blocks = _nbytes((tm, d), F32) + _nbytes((tm, n), out_dtype)
    temps = _nbytes((tm, d), BF16) + _nbytes((tm, n), F32)
    return pl.pallas_call(
        _norm_matmul_kernel,
        grid=(m // tm,),
        in_specs=[
            pl.BlockSpec((tm, d), lambda i: (i, 0)),
            pl.BlockSpec((1, d), lambda i: (0, 0)),
            pl.BlockSpec((d, n), lambda i: (0, 0), pipeline_mode=pl.Buffered(1)),
        ],
        out_specs=pl.BlockSpec((tm, n), lambda i: (i, 0)),
        out_shape=jax.ShapeDtypeStruct((m, n), out_dtype),
        compiler_params=pltpu.CompilerParams(
            dimension_semantics=("parallel",),
            vmem_limit_bytes=_vmem_limit(blocks, _nbytes((d, n), BF16), temps),
        ),
        name=name,
    )(x, g.reshape(1, d), w)


def _proj_residual_kernel(a_ref, b_ref, wa_ref, wb_ref, x_ref, o_ref):
    acc = jnp.dot(a_ref[...], wa_ref[...], preferred_element_type=F32)
    acc = acc + jnp.dot(b_ref[...], wb_ref[...], preferred_element_type=F32)
    o_ref[...] = x_ref[...] + acc


def _proj_residual(a, a_col, b, b_col, w, x, *, tm=1024, tn=1024, name):
    m, d = x.shape
    kh = w.shape[0] // 2
    assert m % tm == 0 and d % tn == 0
    blocks = 2 * _nbytes((tm, kh), BF16) + 2 * _nbytes((kh, tn), BF16) + 2 * _nbytes((tm, tn), F32)
    return pl.pallas_call(
        _proj_residual_kernel,
        grid=(m // tm, d // tn),
        in_specs=[
            pl.BlockSpec((tm, kh), lambda i, j: (i, a_col)),
            pl.BlockSpec((tm, kh), lambda i, j: (i, b_col)),
            pl.BlockSpec((kh, tn), lambda i, j: (0, j)),
            pl.BlockSpec((kh, tn), lambda i, j: (1, j)),
            pl.BlockSpec((tm, tn), lambda i, j: (i, j)),
        ],
        out_specs=pl.BlockSpec((tm, tn), lambda i, j: (i, j)),
        out_shape=jax.ShapeDtypeStruct((m, d), F32),
        compiler_params=pltpu.CompilerParams(
            dimension_semantics=("parallel", "parallel"),
            vmem_limit_bytes=_vmem_limit(blocks, 0, 2 * _nbytes((tm, tn), F32)),
        ),
        name=name,
    )(a, b, w, w, x)


def _mlp_kernel(x_ref, g_ref, w1_ref, w2_ref, *rest, final_norm):
    gf_ref = rest[0] if final_norm else None
    o_ref, h_ref = rest[-2:]
    f = pl.program_id(1)

    @pl.when(f == 0)
    def _():
        x = x_ref[...]
        h_ref[...] = _rms_norm_f32(x, g_ref[...]).astype(h_ref.dtype)
        o_ref[...] = x

    a = jnp.maximum(jnp.dot(h_ref[...], w1_ref[...], preferred_element_type=F32), 0.0)
    o_ref[...] += jnp.dot((a * a).astype(BF16), w2_ref[...], preferred_element_type=F32)

    if final_norm:
        @pl.when(f == pl.num_programs(1) - 1)
        def _():
            o_ref[...] = _rms_norm_f32(o_ref[...], gf_ref[...])


def _mlp(x, g, w1, w2, layer, g_final=None, *, tm=512, tf=1024, name):
    m, d = x.shape
    ff = w1.shape[2]
    assert m % tm == 0 and ff % tf == 0
    final_norm = g_final is not None
    blocks = 2 * _nbytes((tm, d), F32) + _nbytes((d, tf), BF16) + _nbytes((tf, d), BF16)
    resident = _nbytes((tm, d), BF16)
    temps = _nbytes((tm, tf), F32) + _nbytes((tm, tf), BF16) + _nbytes((tm, d), F32)
    gain_spec = pl.BlockSpec((1, d), lambda i, f: (0, 0))
    extra_specs, extra_args = ([gain_spec], [g_final.reshape(1, d)]) if final_norm else ([], [])
    return pl.pallas_call(
        functools.partial(_mlp_kernel, final_norm=final_norm),
        grid=(m // tm, ff // tf),
        in_specs=[
            pl.BlockSpec((tm, d), lambda i, f: (i, 0)),
            gain_spec,
            pl.BlockSpec((None, d, tf), lambda i, f: (layer, 0, f)),
            pl.BlockSpec((None, tf, d), lambda i, f: (layer, f, 0)),
        ] + extra_specs,
        out_specs=pl.BlockSpec((tm, d), lambda i, f: (i, 0)),
        out_shape=jax.ShapeDtypeStruct((m, d), F32),
        scratch_shapes=[pltpu.VMEM((tm, d), BF16)],
        compiler_params=pltpu.CompilerParams(
            dimension_semantics=("parallel", "arbitrary"),
            vmem_limit_bytes=_vmem_limit(blocks, resident, temps),
        ),
        name=name,
    )(x, g.reshape(1, d), w1, w2, *extra_args)


def _side_cast_specs(w, n_steps, step_index):
    rows, cols = w.shape
    assert rows % n_steps == 0 and (rows // n_steps) % 16 == 0
    rb = rows // n_steps
    index_map = lambda *ids: (step_index(*ids), 0)
    spec = pl.BlockSpec((rb, cols), index_map)
    return spec, spec, jax.ShapeDtypeStruct((rows, cols), BF16), _nbytes((rb, cols), F32) + _nbytes((rb, cols), BF16)


NA_ROWS_PER_STEP = 32


def _na_bias_table(rpb):
    n_heads = rpb.shape[0]
    col = jnp.arange(GRID_W)
    col_start = jnp.clip(col - NA_COL_WIN // 2, 0, GRID_W - NA_COL_WIN)
    kc = jnp.arange(GRID_W)[None, :]
    valid = (kc >= col_start[:, None]) & (kc < col_start[:, None] + NA_COL_WIN)
    dc = kc - col[:, None] + (NA_COL_WIN - 1)
    onehot = (dc[:, :, None] == jnp.arange(2 * NA_COL_WIN - 1)[None, None, :]).astype(F32)
    by_col = jnp.einsum("hrd,ckd->hcrk", rpb.astype(F32) * LOG2E, onehot, precision=lax.Precision.HIGHEST)
    by_col = jnp.where(valid[None, :, None, :], by_col, NEG_INF)
    per_delta = [by_col[:, :, NA_ROW_WIN - 1 - dl: 2 * NA_ROW_WIN - 1 - dl].reshape(n_heads, GRID_W, -1)
                 for dl in range(NA_ROW_WIN)]
    return jnp.stack(per_delta, axis=1)


def _na_kernel(q_ref, k_ref, v_ref, bias_ref, x_ref, g_ref, wb_ref, w_ref, o_ref, pb_ref, wo_ref, *, rows):
    span = NA_ROW_WIN * GRID_W
    h = _rms_norm_f32(x_ref[...], g_ref[...]).astype(BF16)
    pb_ref[...] = jnp.dot(h, wb_ref[...], preferred_element_type=F32)
    wo_ref[...] = w_ref[...].astype(wo_ref.dtype)

    def one_row(r):
        r0 = jnp.clip(r - NA_ROW_WIN // 2, 0, rows - NA_ROW_WIN)
        q_start = pl.multiple_of(r * GRID_W, GRID_W)
        k_start = pl.multiple_of(r0 * GRID_W, GRID_W)
        q = q_ref[pl.ds(q_start, GRID_W), :]
        k = k_ref[pl.ds(k_start, span), :]
        v = v_ref[pl.ds(k_start, span), :]
        s = _qk_scores(q, k) + bias_ref[r - r0]
        m = jnp.max(s, axis=-1, keepdims=True)
        p = jnp.exp2(s - m).astype(BF16)
        num, den = _pv_with_denominator(p, v)
        o_ref[pl.ds(q_start, GRID_W), :] = (num / den).astype(o_ref.dtype)

    for u in range(NA_ROWS_PER_STEP):
        one_row(pl.program_id(2) * NA_ROWS_PER_STEP + u)


def _na_attention_and_proj(proj, bias_tbl, n_heads, x, g, w_b, side_w, *, name):
    b, t, _ = proj.shape
    rows = t // GRID_W
    assert rows >= NA_ROW_WIN and rows % NA_ROWS_PER_STEP == 0
    n_trips = rows // NA_ROWS_PER_STEP
    n_steps = b * n_heads * n_trips
    m, d = x.shape
    n = w_b.shape[1]
    assert m % n_steps == 0 and (m // n_steps) % 16 == 0
    rb = m // n_steps
    span = NA_ROW_WIN * GRID_W
    step_row = lambda bi, h, tr: (bi * n_heads + h) * n_trips + tr
    step_id = lambda bi, h, tr: (step_row(bi, h, tr), 0)
    side_in, side_out, side_shape, side_bytes = _side_cast_specs(side_w, n_steps, step_row)
    head_blk = lambda off: pl.BlockSpec((None, t, HEAD_DIM), lambda bi, h, tr: (bi, 0, off + h))
    blocks = (4 * _nbytes((t, HEAD_DIM), BF16) + _nbytes((NA_ROW_WIN, GRID_W, span), F32)
              + _nbytes((rb, d), F32) + _nbytes((rb, n), F32) + side_bytes)
    return pl.pallas_call(
        functools.partial(_na_kernel, rows=rows),
        grid=(b, n_heads, n_trips),
        in_specs=[
            head_blk(0), head_blk(n_heads), head_blk(2 * n_heads),
            pl.BlockSpec((None, NA_ROW_WIN, GRID_W, span), lambda bi, h, tr: (h, 0, 0, 0)),
            pl.BlockSpec((rb, d), step_id),
            pl.BlockSpec((1, d), lambda bi, h, tr: (0, 0)),
            pl.BlockSpec((d, n), lambda bi, h, tr: (0, 0), pipeline_mode=pl.Buffered(1)),
            side_in,
        ],
        out_specs=[pl.BlockSpec((None, t, HEAD_DIM), lambda bi, h, tr: (bi, 0, h)), pl.BlockSpec((rb, n), step_id),
                   side_out],
        out_shape=[jax.ShapeDtypeStruct((b, t, n_heads * HEAD_DIM), BF16), jax.ShapeDtypeStruct((m, n), F32),
                   side_shape],
        compiler_params=pltpu.CompilerParams(
            dimension_semantics=("parallel", "parallel", "arbitrary"),
            vmem_limit_bytes=_vmem_limit(blocks, _nbytes((d, n), BF16), 8 << 20),
        ),
        name=name,
    )(proj, proj, proj, bias_tbl, x, g.reshape(1, d), w_b, side_w)


DIL_SUPER = 2048
DIL_QBLK = 128
DIL_BLOCKS_PER_ITER = 16
DIL_MERGE_ROWS = 256


def _band_bias(n_q, span, offset, radius, slope2):
    rel = (lax.broadcasted_iota(jnp.int32, (n_q, 1), 0)
           - lax.broadcasted_iota(jnp.int32, (1, span), 1)) + offset
    dist = jnp.abs(rel)
    return jnp.where(dist <= radius, -slope2 * dist.astype(F32), NEG_INF)


def _dilated_kernel(slope_ref, q_ref, k_ref, v_ref, w_ref, o_ref, wo_ref, bias_ref, acc_ref, m_ref, l_ref, *,
                    seq_len):
    h = pl.program_id(1)
    sb = pl.program_id(2)
    wo_ref[...] = w_ref[...].astype(wo_ref.dtype)

    @pl.when(sb == 0)
    def _():
        slope2 = slope_ref[h] * LOG2E
        for br, (window, dil) in enumerate(DILATED_BRANCHES):
            radius = window // (2 * dil)
            for var in range(3):
                bias_ref[3 * br + var] = _band_bias(DIL_QBLK, DIL_QBLK + 2 * radius, var * radius, radius, slope2 * dil)

    for br, (window, dil) in enumerate(DILATED_BRANCHES):
        radius = window // (2 * dil)
        span = DIL_QBLK + 2 * radius
        sub = seq_len // dil
        blocks_per_res = DIL_SUPER // dil // DIL_QBLK

        def one_block(idx, br=br, dil=dil, radius=radius, span=span, sub=sub, blocks_per_res=blocks_per_res):
            res = idx // blocks_per_res
            c = idx % blocks_per_res
            q_row = res + dil * DIL_QBLK * c
            u0 = sb * (DIL_SUPER // dil) + DIL_QBLK * c
            ws = jnp.clip(u0 - radius, 0, sub - span)
            k_row = res + dil * ws
            if dil == 1:
                q_idx = pl.ds(q_row, DIL_QBLK)
                k_idx = pl.ds(k_row, span)
            else:
                q_idx = pl.ds(q_row, DIL_QBLK, stride=dil)
                k_idx = pl.ds(k_row, span, stride=dil)
            q = q_ref[q_idx, :].astype(BF16)
            k = k_ref[k_idx, :].astype(BF16)
            v = v_ref[k_idx, :].astype(BF16)
            s = _qk_scores(q, k) + bias_ref[3 * br + (u0 - ws) // radius]
            m = jnp.max(s, axis=-1, keepdims=True)
            p = jnp.exp2(s - m).astype(BF16)
            num, den = _pv_with_denominator(p, v)
            acc_ref[br, q_idx, :] = num
            l_ref[br, q_idx, :] = den
            m_ref[br, q_idx, :] = jnp.broadcast_to(m, (DIL_QBLK, HEAD_DIM))

        def body(it, carry, one_block=one_block):
            for u in range(DIL_BLOCKS_PER_ITER):
                one_block(it * DIL_BLOCKS_PER_ITER + u)
            return carry

        lax.fori_loop(0, DIL_SUPER // DIL_QBLK // DIL_BLOCKS_PER_ITER, body, 0)

    n_br = len(DILATED_BRANCHES)

    def merge(it, carry):
        rows = pl.ds(pl.multiple_of(it * DIL_MERGE_ROWS, DIL_MERGE_ROWS), DIL_MERGE_ROWS)
        ms = [m_ref[br, rows, :] for br in range(n_br)]
        m_all = functools.reduce(jnp.maximum, ms)
        ws = [jnp.exp2(m - m_all) for m in ms]
        num = sum(w * acc_ref[br, rows, :] for br, w in enumerate(ws))
        den = sum(w * l_ref[br, rows, :] for br, w in enumerate(ws))
        o_ref[rows, :] = (num / den).astype(o_ref.dtype)
        return carry

    lax.fori_loop(0, DIL_SUPER // DIL_MERGE_ROWS, merge, 0)


def _dilated_attention(proj, n_heads, side_w, *, name):
    b, t, _ = proj.shape
    n_sb = t // DIL_SUPER
    side_in, side_out, side_shape, side_bytes = _side_cast_specs(
        side_w, b * n_heads * n_sb, lambda bi, h, s: (bi * n_heads + h) * n_sb + s)
    n_br = len(DILATED_BRANCHES)
    spans = {DIL_QBLK + window // dil for window, dil in DILATED_BRANCHES}
    assert len(spans) == 1, "bias scratch assumes one key span for all branches"
    span = spans.pop()
    assert t % DIL_SUPER == 0 and DIL_SUPER // DIL_QBLK % DIL_BLOCKS_PER_ITER == 0 and DIL_SUPER % DIL_MERGE_ROWS == 0
    for window, dil in DILATED_BRANCHES:
        assert DIL_SUPER % (dil * DIL_QBLK) == 0 and t // dil >= span and window // (2 * dil) <= DIL_QBLK
    slopes = 2.0 ** (-8.0 * jnp.arange(1, n_heads + 1, dtype=F32) / n_heads)
    kv_blk = lambda off: pl.BlockSpec((None, t, HEAD_DIM), lambda bi, h, s: (bi, 0, off + h))
    blocks = (2 * _nbytes((t, HEAD_DIM), F32) + _nbytes((DIL_SUPER, HEAD_DIM), F32)
              + _nbytes((DIL_SUPER, HEAD_DIM), BF16) + side_bytes)
    stats = pltpu.VMEM((n_br, DIL_SUPER, HEAD_DIM), F32)
    scratch = 3 * _nbytes((n_br, DIL_SUPER, HEAD_DIM), F32) + _nbytes((3 * n_br, DIL_QBLK, span), F32)
    return pl.pallas_call(
        functools.partial(_dilated_kernel, seq_len=t),
        grid=(b, n_heads, t // DIL_SUPER),
        in_specs=[
            pl.BlockSpec(memory_space=pltpu.SMEM),
            pl.BlockSpec((None, DIL_SUPER, HEAD_DIM), lambda bi, h, s: (bi, s, h)),
            kv_blk(n_heads), kv_blk(2 * n_heads), side_in,
        ],
        out_specs=[pl.BlockSpec((None, DIL_SUPER, HEAD_DIM), lambda bi, h, s: (bi, s, h)), side_out],
        out_shape=[jax.ShapeDtypeStruct((b, t, n_heads * HEAD_DIM), BF16), side_shape],
        scratch_shapes=[pltpu.VMEM((3 * n_br, DIL_QBLK, span), F32), stats, stats, stats],
        compiler_params=pltpu.CompilerParams(
            dimension_semantics=("parallel", "parallel", "arbitrary"),
            vmem_limit_bytes=_vmem_limit(blocks, scratch, 8 << 20),
        ),
        name=name,
    )(slopes, proj, proj, proj, side_w)


SWA_QBLK = 256
SWA_BLOCKS_PER_STEP = 8


def _swa_kernel(slope_ref, sink_ref, q_ref, k_ref, v_ref, o_ref, bias_ref, *, seq_len, group):
    kvh = pl.program_id(1)
    span = SWA_QBLK + 2 * C_RADIUS

    @pl.when(pl.program_id(2) == 0)
    def _():
        for g in range(group):
            slope2 = slope_ref[kvh * group + g] * LOG2E
            for var in range(3):
                bias_ref[3 * g + var] = _band_bias(SWA_QBLK, span, var * C_RADIUS, C_RADIUS, slope2)

    for blk in range(SWA_BLOCKS_PER_STEP):
        rows = slice(blk * SWA_QBLK, (blk + 1) * SWA_QBLK)
        s0 = (pl.program_id(2) * SWA_BLOCKS_PER_STEP + blk) * SWA_QBLK
        ks = pl.multiple_of(jnp.clip(s0 - C_RADIUS, 0, seq_len - span), C_RADIUS)
        var = (s0 - ks) // C_RADIUS
        k = k_ref[pl.ds(ks, span), :]
        v = v_ref[pl.ds(ks, span), :]
        for g in range(group):
            cols = slice(g * HEAD_DIM, (g + 1) * HEAD_DIM)
            s = _qk_scores(q_ref[rows, cols], k) + bias_ref[3 * g + var]
            sink2 = sink_ref[kvh * group + g] * LOG2E
            m = jnp.maximum(jnp.max(s, axis=-1, keepdims=True), sink2)
            p = jnp.exp2(s - m).astype(BF16)
            num, den = _pv_with_denominator(p, v)
            o = num / (den + jnp.exp2(sink2 - m))
            o_ref[rows, cols] = o.astype(o_ref.dtype)


def _swa_attention(proj, sink, n_heads, n_kv, *, name):
    b, t, _ = proj.shape
    group = n_heads // n_kv
    span = SWA_QBLK + 2 * C_RADIUS
    step_rows = SWA_QBLK * SWA_BLOCKS_PER_STEP
    assert t % step_rows == 0 and t >= span and C_RADIUS <= SWA_QBLK and SWA_QBLK % C_RADIUS == 0
    slopes = 2.0 ** (-8.0 * jnp.arange(1, n_heads + 1, dtype=F32) / n_heads)
    kv_blk = lambda off: pl.BlockSpec((None, t, HEAD_DIM), lambda bi, kh, s: (bi, 0, off + kh))
    q_blk = pl.BlockSpec((None, step_rows, group * HEAD_DIM), lambda bi, kh, s: (bi, s, kh))
    blocks = 2 * _nbytes((t, HEAD_DIM), BF16) + 2 * _nbytes((step_rows, group * HEAD_DIM), BF16)
    bias_bytes = _nbytes((3 * group, SWA_QBLK, span), F32)
    return pl.pallas_call(
        functools.partial(_swa_kernel, seq_len=t, group=group),
        grid=(b, n_kv, t // step_rows),
        in_specs=[
            pl.BlockSpec(memory_space=pltpu.SMEM),
            pl.BlockSpec(memory_space=pltpu.SMEM),
            q_blk, kv_blk(n_heads), kv_blk(n_heads + n_kv),
        ],
        out_specs=q_blk,
        out_shape=jax.ShapeDtypeStruct((b, t, n_heads * HEAD_DIM), BF16),
        scratch_shapes=[pltpu.VMEM((3 * group, SWA_QBLK, span), F32)],
        compiler_params=pltpu.CompilerParams(
            dimension_semantics=("parallel", "parallel", "arbitrary"),
            vmem_limit_bytes=_vmem_limit(blocks, bias_bytes, 8 << 20),
        ),
        name=name,
    )(slopes, sink.astype(F32), proj, proj, proj)


def _scaled_bf16(w, n_q_cols):
    col = jnp.arange(w.shape[1])
    scale = jnp.where(col < n_q_cols, LOG2E * HEAD_DIM ** -0.5, 1.0).astype(F32)
    return (w * scale[None, :]).astype(BF16)


def kernel(x, attn_norm, mlp_norm, w_mlp_in, w_mlp_out, even_w_in, even_rpb, even_w_out,
           odd_w_qkv, odd_sink, odd_w_out, final_norm):
    b, t, d = x.shape
    depth = attn_norm.shape[0]
    n_heads_a = even_rpb.shape[1]
    wa = n_heads_a * HEAD_DIM
    n_heads_b = (even_w_in.shape[2] - 3 * wa) // (3 * HEAD_DIM)
    n_heads_c = odd_sink.shape[1]
    n_kv_c = (odd_w_qkv.shape[2] // HEAD_DIM - n_heads_c) // 2
    xf = x.reshape(b * t, d)
    w1_bf16 = w2_bf16 = None

    for i in range(depth):
        j = i // 2
        if i % 2 == 0:
            w_in = even_w_in[j]
            proj_a = _norm_matmul(xf, attn_norm[i], _scaled_bf16(w_in[:, :3 * wa], wa), BF16, name=f"l{i}_proj_a")
            w1_pair, w2_pair = w_mlp_in[i:i + 2], w_mlp_out[i:i + 2]
            o_a, proj_b, w1_bf16 = _na_attention_and_proj(
                proj_a.reshape(b, t, -1), _na_bias_table(even_rpb[j]), n_heads_a, xf, attn_norm[i],
                _scaled_bf16(w_in[:, 3 * wa:], n_heads_b * HEAD_DIM), w1_pair.reshape(-1, w1_pair.shape[-1]),
                name=f"l{i}_na_proj_b")
            o_b, w2_bf16 = _dilated_attention(proj_b.reshape(b, t, -1), n_heads_b,
                                              w2_pair.reshape(-1, w2_pair.shape[-1]), name=f"l{i}_dilated")
            w1_bf16, w2_bf16 = w1_bf16.reshape(w1_pair.shape), w2_bf16.reshape(w2_pair.shape)
            xf = _proj_residual(o_a.reshape(b * t, -1), 0, o_b.reshape(b * t, -1), 0,
                                even_w_out[j].astype(BF16), xf, name=f"l{i}_out")
        else:
            proj_c = _norm_matmul(xf, attn_norm[i], _scaled_bf16(odd_w_qkv[j], n_heads_c * HEAD_DIM), BF16,
                                  name=f"l{i}_proj_c")
            o_c = _swa_attention(proj_c.reshape(b, t, -1), odd_sink[j], n_heads_c, n_kv_c, name=f"l{i}_swa")
            o_c = o_c.reshape(b * t, -1)
            xf = _proj_residual(o_c, 0, o_c, 1, odd_w_out[j].astype(BF16), xf, name=f"l{i}_out")
        g_final = final_norm if i == depth - 1 else None
        xf = _mlp(xf, mlp_norm[i], w1_bf16, w2_bf16, i % 2, g_final, name=f"l{i}_mlp")
    return xf.reshape(b, t, d)
```

```python
import functools
import math

import jax
import jax.numpy as jnp
from jax import lax
from jax.experimental import pallas as pl
from jax.experimental.pallas import tpu as pltpu

HEAD_DIM = 128
GRID_W = 64
NA_ROW_WIN = 8
NA_COL_WIN = 16
DILATED_BRANCHES = ((128, 1), (512, 4), (2048, 16))
C_RADIUS = 128
NORM_EPS = 1e-6
NEG_INF = -1e30
LOG2E = math.log2(math.e)

V7X_VMEM_LIMIT_CAP = 56 * 1024 * 1024
BF16 = jnp.bfloat16
F32 = jnp.float32


def _vmem_limit(block_bytes, scratch_bytes, temp_bytes):
    need = 2 * block_bytes + scratch_bytes + temp_bytes + (2 << 20)
    return int(min(max(need, 16 << 20), V7X_VMEM_LIMIT_CAP))


def _nbytes(shape, dtype):
    n = 1
    for s in shape:
        n *= s
    return n * jnp.dtype(dtype).itemsize


def _rms_norm_f32(x, g):
    y = x * lax.rsqrt(jnp.mean(x * x, axis=-1, keepdims=True) + NORM_EPS)
    return y * g


def _qk_scores(q, k):
    return lax.dot_general(q, k, (((1,), (1,)), ((), ())), preferred_element_type=F32)


def _pv_with_denominator(p, v):
    v_ext = jnp.concatenate([v, jnp.ones_like(v)], axis=1)
    pv = jnp.dot(p, v_ext, preferred_element_type=F32)
    return pv[:, :HEAD_DIM], pv[:, HEAD_DIM:]


def _side_cast_specs(w, n_steps, step_index, col_block=None):
    rows = w.shape[0]
    width, col = (w.shape[1], 0) if col_block is None else col_block
    assert rows % n_steps == 0 and (rows // n_steps) % 16 == 0 and w.shape[1] % width == 0
    rb = rows // n_steps
    in_spec = pl.BlockSpec((rb, width), lambda *ids: (step_index(*ids), col))
    out_spec = pl.BlockSpec((rb, width), lambda *ids: (step_index(*ids), 0))
    return in_spec, out_spec, jax.ShapeDtypeStruct((rows, width), BF16), _nbytes((rb, width), F32) + _nbytes((rb, width), BF16)


def _side_casts(in_refs, out_refs, scaled):
    it = iter(in_refs)
    for o_ref, has_scale in zip(out_refs, scaled):
        w = next(it)[...]
        if has_scale:
            w = w * next(it)[...]
        o_ref[...] = w.astype(o_ref.dtype)


def _norm_matmul_kernel(x_ref, g_ref, w_ref, *rest, side_scaled):
    n_side_in = len(side_scaled) + sum(side_scaled)
    o_ref = rest[n_side_in]
    _side_casts(rest[:n_side_in], rest[n_side_in + 1:], side_scaled)
    h = _rms_norm_f32(x_ref[...], g_ref[...]).astype(BF16)
    o_ref[...] = jnp.dot(h, w_ref[...], preferred_element_type=F32).astype(o_ref.dtype)


def _norm_matmul(x, g, w, out_dtype, side_items=(), *, tm=512, name):
    m, d = x.shape
    n = w.shape[1]
    assert m % tm == 0
    n_steps = m // tm
    side_in_specs, side_args, side_out_specs, side_shapes, side_bytes = [], [], [], [], 0
    for w_side, col_block, scale in side_items:
        in_spec, out_spec, shape, nbytes = _side_cast_specs(w_side, n_steps, lambda i: i, col_block)
        side_in_specs.append(in_spec)
        side_args.append(w_side)
        if scale is not None:
            side_in_specs.append(pl.BlockSpec((1, scale.shape[1]), lambda i: (0, 0)))
            side_args.append(scale)
        side_out_specs.append(out_spec)
        side_shapes.append(shape)
        side_bytes += nbytes
    blocks = _nbytes((tm, d), F32) + _nbytes((tm, n), out_dtype) + side_bytes
    temps = _nbytes((tm, d), BF16) + _nbytes((tm, n), F32)
    outs = pl.pallas_call(
        functools.partial(_norm_matmul_kernel, side_scaled=tuple(scale is not None for _, _, scale in side_items)),
        grid=(n_steps,),
        in_specs=[
            pl.BlockSpec((tm, d), lambda i: (i, 0)),
            pl.BlockSpec((1, d), lambda i: (0, 0)),
            pl.BlockSpec((d, n), lambda i: (0, 0), pipeline_mode=pl.Buffered(1)),
        ] + side_in_specs,
        out_specs=[pl.BlockSpec((tm, n), lambda i: (i, 0))] + side_out_specs,
        out_shape=[jax.ShapeDtypeStruct((m, n), out_dtype)] + side_shapes,
        compiler_params=pltpu.CompilerParams(
            dimension_semantics=("parallel",),
            vmem_limit_bytes=_vmem_limit(blocks, _nbytes((d, n), BF16), temps),
        ),
        name=name,
    )(x, g.reshape(1, d), w, *side_args)
    return outs[0], list(outs[1:])


def _proj_residual_kernel(a_ref, b_ref, wa_ref, wb_ref, x_ref, o_ref):
    acc = jnp.dot(a_ref[...], wa_ref[...], preferred_element_type=F32)
    acc = acc + jnp.dot(b_ref[...], wb_ref[...], preferred_element_type=F32)
    o_ref[...] = x_ref[...] + acc


def _proj_residual(a, a_col, b, b_col, w, x, *, tm=1024, tn=1024, name):
    m, d = x.shape
    kh = w.shape[0] // 2
    assert m % tm == 0 and d % tn == 0
    blocks = 2 * _nbytes((tm, kh), BF16) + 2 * _nbytes((kh, tn), BF16) + 2 * _nbytes((tm, tn), F32)
    return pl.pallas_call(
        _proj_residual_kernel,
        grid=(m // tm, d // tn),
        in_specs=[
            pl.BlockSpec((tm, kh), lambda i, j: (i, a_col)),
            pl.BlockSpec((tm, kh), lambda i, j: (i, b_col)),
            pl.BlockSpec((kh, tn), lambda i, j: (0, j)),
            pl.BlockSpec((kh, tn), lambda i, j: (1, j)),
            pl.BlockSpec((tm, tn), lambda i, j: (i, j)),
        ],
        out_specs=pl.BlockSpec((tm, tn), lambda i, j: (i, j)),
        out_shape=jax.ShapeDtypeStruct((m, d), F32),
        compiler_params=pltpu.CompilerParams(
            dimension_semantics=("parallel", "parallel"),
            vmem_limit_bytes=_vmem_limit(blocks, 0, 2 * _nbytes((tm, tn), F32)),
        ),
        name=name,
    )(a, b, w, w, x)


def _mlp_kernel(x_ref, g_ref, w1_ref, w2_ref, *rest, final_norm):
    gf_ref = rest[0] if final_norm else None
    o_ref, h_ref = rest[-2:]
    f = pl.program_id(1)

    @pl.when(f == 0)
    def _():
        x = x_ref[...]
        h_ref[...] = _rms_norm_f32(x, g_ref[...]).astype(h_ref.dtype)
        o_ref[...] = x

    a = jnp.maximum(jnp.dot(h_ref[...], w1_ref[...], preferred_element_type=F32), 0.0)
    o_ref[...] += jnp.dot((a * a).astype(BF16), w2_ref[...], preferred_element_type=F32)

    if final_norm:
        @pl.when(f == pl.num_programs(1) - 1)
        def _():
            o_ref[...] = _rms_norm_f32(o_ref[...], gf_ref[...])


def _mlp(x, g, w1, w2, layer, g_final=None, *, tm=512, tf=1024, name):
    m, d = x.shape
    ff = w1.shape[2]
    assert m % tm == 0 and ff % tf == 0
    final_norm = g_final is not None
    blocks = 2 * _nbytes((tm, d), F32) + _nbytes((d, tf), BF16) + _nbytes((tf, d), BF16)
    resident = _nbytes((tm, d), BF16)
    temps = _nbytes((tm, tf), F32) + _nbytes((tm, tf), BF16) + _nbytes((tm, d), F32)
    gain_spec = pl.BlockSpec((1, d), lambda i, f: (0, 0))
    extra_specs, extra_args = ([gain_spec], [g_final.reshape(1, d)]) if final_norm else ([], [])
    return pl.pallas_call(
        functools.partial(_mlp_kernel, final_norm=final_norm),
        grid=(m // tm, ff // tf),
        in_specs=[
            pl.BlockSpec((tm, d), lambda i, f: (i, 0)),
            gain_spec,
            pl.BlockSpec((None, d, tf), lambda i, f: (layer, 0, f)),
            pl.BlockSpec((None, tf, d), lambda i, f: (layer, f, 0)),
        ] + extra_specs,
        out_specs=pl.BlockSpec((tm, d), lambda i, f: (i, 0)),
        out_shape=jax.ShapeDtypeStruct((m, d), F32),
        scratch_shapes=[pltpu.VMEM((tm, d), BF16)],
        compiler_params=pltpu.CompilerParams(
            dimension_semantics=("parallel", "arbitrary"),
            vmem_limit_bytes=_vmem_limit(blocks, resident, temps),
        ),
        name=name,
    )(x, g.reshape(1, d), w1, w2, *extra_args)


NA_ROWS_PER_STEP = 32


def _na_bias_table(rpb):
    n_heads = rpb.shape[0]
    col = jnp.arange(GRID_W)
    col_start = jnp.clip(col - NA_COL_WIN // 2, 0, GRID_W - NA_COL_WIN)
    kc = jnp.arange(GRID_W)[None, :]
    valid = (kc >= col_start[:, None]) & (kc < col_start[:, None] + NA_COL_WIN)
    dc = kc - col[:, None] + (NA_COL_WIN - 1)
    onehot = (dc[:, :, None] == jnp.arange(2 * NA_COL_WIN - 1)[None, None, :]).astype(F32)
    by_col = jnp.einsum("hrd,ckd->hcrk", rpb.astype(F32) * LOG2E, onehot, precision=lax.Precision.HIGHEST)
    by_col = jnp.where(valid[None, :, None, :], by_col, NEG_INF)
    per_delta = [by_col[:, :, NA_ROW_WIN - 1 - dl: 2 * NA_ROW_WIN - 1 - dl].reshape(n_heads, GRID_W, -1)
                 for dl in range(NA_ROW_WIN)]
    return jnp.stack(per_delta, axis=1)


def _na_kernel(q_ref, k_ref, v_ref, bias_ref, x_ref, g_ref, wb_ref, w_ref, o_ref, pb_ref, wo_ref, *, rows):
    span = NA_ROW_WIN * GRID_W
    h = _rms_norm_f32(x_ref[...], g_ref[...]).astype(BF16)
    pb_ref[...] = jnp.dot(h, wb_ref[...], preferred_element_type=F32)
    _side_casts([w_ref], [wo_ref], [False])

    def one_row(r):
        r0 = jnp.clip(r - NA_ROW_WIN // 2, 0, rows - NA_ROW_WIN)
        q_start = pl.multiple_of(r * GRID_W, GRID_W)
        k_start = pl.multiple_of(r0 * GRID_W, GRID_W)
        q = q_ref[pl.ds(q_start, GRID_W), :]
        k = k_ref[pl.ds(k_start, span), :]
        v = v_ref[pl.ds(k_start, span), :]
        s = _qk_scores(q, k) + bias_ref[r - r0]
        m = jnp.max(s, axis=-1, keepdims=True)
        p = jnp.exp2(s - m).astype(BF16)
        num, den = _pv_with_denominator(p, v)
        o_ref[pl.ds(q_start, GRID_W), :] = (num / den).astype(o_ref.dtype)

    for u in range(NA_ROWS_PER_STEP):
        one_row(pl.program_id(2) * NA_ROWS_PER_STEP + u)


def _na_attention_and_proj(proj, bias_tbl, n_heads, x, g, w_b, side_w, *, name):
    b, t, _ = proj.shape
    rows = t // GRID_W
    assert rows >= NA_ROW_WIN and rows % NA_ROWS_PER_STEP == 0
    n_trips = rows // NA_ROWS_PER_STEP
    n_steps = b * n_heads * n_trips
    m, d = x.shape
    n = w_b.shape[1]
    assert m % n_steps == 0 and (m // n_steps) % 16 == 0
    rb = m // n_steps
    span = NA_ROW_WIN * GRID_W
    step_row = lambda bi, h, tr: (bi * n_heads + h) * n_trips + tr
    step_id = lambda bi, h, tr: (step_row(bi, h, tr), 0)
    side_in, side_out, side_shape, side_bytes = _side_cast_specs(side_w, n_steps, step_row)
    head_blk = lambda off: pl.BlockSpec((None, t, HEAD_DIM), lambda bi, h, tr: (bi, 0, off + h))
    blocks = (4 * _nbytes((t, HEAD_DIM), BF16) + _nbytes((NA_ROW_WIN, GRID_W, span), F32)
              + _nbytes((rb, d), F32) + _nbytes((rb, n), F32) + side_bytes)
    return pl.pallas_call(
        functools.partial(_na_kernel, rows=rows),
        grid=(b, n_heads, n_trips),
        in_specs=[
            head_blk(0), head_blk(n_heads), head_blk(2 * n_heads),
            pl.BlockSpec((None, NA_ROW_WIN, GRID_W, span), lambda bi, h, tr: (h, 0, 0, 0)),
            pl.BlockSpec((rb, d), step_id),
            pl.BlockSpec((1, d), lambda bi, h, tr: (0, 0)),
            pl.BlockSpec((d, n), lambda bi, h, tr: (0, 0), pipeline_mode=pl.Buffered(1)),
            side_in,
        ],
        out_specs=[pl.BlockSpec((None, t, HEAD_DIM), lambda bi, h, tr: (bi, 0, h)), pl.BlockSpec((rb, n), step_id),
                   side_out],
        out_shape=[jax.ShapeDtypeStruct((b, t, n_heads * HEAD_DIM), BF16), jax.ShapeDtypeStruct((m, n), F32),
                   side_shape],
        compiler_params=pltpu.CompilerParams(
            dimension_semantics=("parallel", "parallel", "arbitrary"),
            vmem_limit_bytes=_vmem_limit(blocks, _nbytes((d, n), BF16), 8 << 20),
        ),
        name=name,
    )(proj, proj, proj, bias_tbl, x, g.reshape(1, d), w_b, side_w)


DIL_SUPER = 2048
DIL_QBLK = 128
DIL_BLOCKS_PER_ITER = 16
DIL_MERGE_ROWS = 256


def _band_bias(n_q, span, offset, radius, slope2):
    rel = (lax.broadcasted_iota(jnp.int32, (n_q, 1), 0)
           - lax.broadcasted_iota(jnp.int32, (1, span), 1)) + offset
    dist = jnp.abs(rel)
    return jnp.where(dist <= radius, -slope2 * dist.astype(F32), NEG_INF)


def _dilated_kernel(slope_ref, q_ref, k_ref, v_ref, w_ref, o_ref, wo_ref, bias_ref, acc_ref, m_ref, l_ref, *,
                    seq_len):
    h = pl.program_id(1)
    sb = pl.program_id(2)
    _side_casts([w_ref], [wo_ref], [False])

    @pl.when(sb == 0)
    def _():
        slope2 = slope_ref[h] * LOG2E
        for br, (window, dil) in enumerate(DILATED_BRANCHES):
            radius = window // (2 * dil)
            for var in range(3):
                bias_ref[3 * br + var] = _band_bias(DIL_QBLK, DIL_QBLK + 2 * radius, var * radius, radius, slope2 * dil)

    for br, (window, dil) in enumerate(DILATED_BRANCHES):
        radius = window // (2 * dil)
        span = DIL_QBLK + 2 * radius
        sub = seq_len // dil
        blocks_per_res = DIL_SUPER // dil // DIL_QBLK

        def one_block(idx, br=br, dil=dil, radius=radius, span=span, sub=sub, blocks_per_res=blocks_per_res):
            res = idx // blocks_per_res
            c = idx % blocks_per_res
            q_row = res + dil * DIL_QBLK * c
            u0 = sb * (DIL_SUPER // dil) + DIL_QBLK * c
            ws = jnp.clip(u0 - radius, 0, sub - span)
            k_row = res + dil * ws
            if dil == 1:
                q_idx = pl.ds(q_row, DIL_QBLK)
                k_idx = pl.ds(k_row, span)
            else:
                q_idx = pl.ds(q_row, DIL_QBLK, stride=dil)
                k_idx = pl.ds(k_row, span, stride=dil)
            q = q_ref[q_idx, :].astype(BF16)
            k = k_ref[k_idx, :].astype(BF16)
            v = v_ref[k_idx, :].astype(BF16)
            s = _qk_scores(q, k) + bias_ref[3 * br + (u0 - ws) // radius]
            m = jnp.max(s, axis=-1, keepdims=True)
            p = jnp.exp2(s - m).astype(BF16)
            num, den = _pv_with_denominator(p, v)
            acc_ref[br, q_idx, :] = num
            l_ref[br, q_idx, :] = den
            m_ref[br, q_idx, :] = jnp.broadcast_to(m, (DIL_QBLK, HEAD_DIM))

        def body(it, carry, one_block=one_block):
            for u in range(DIL_BLOCKS_PER_ITER):
                one_block(it * DIL_BLOCKS_PER_ITER + u)
            return carry

        lax.fori_loop(0, DIL_SUPER // DIL_QBLK // DIL_BLOCKS_PER_ITER, body, 0)

    n_br = len(DILATED_BRANCHES)

    def merge(it, carry):
        rows = pl.ds(pl.multiple_of(it * DIL_MERGE_ROWS, DIL_MERGE_ROWS), DIL_MERGE_ROWS)
        ms = [m_ref[br, rows, :] for br in range(n_br)]
        m_all = functools.reduce(jnp.maximum, ms)
        ws = [jnp.exp2(m - m_all) for m in ms]
        num = sum(w * acc_ref[br, rows, :] for br, w in enumerate(ws))
        den = sum(w * l_ref[br, rows, :] for br, w in enumerate(ws))
        o_ref[rows, :] = (num / den).astype(o_ref.dtype)
        return carry

    lax.fori_loop(0, DIL_SUPER // DIL_MERGE_ROWS, merge, 0)


def _dilated_attention(proj, n_heads, side_w, *, name):
    b, t, _ = proj.shape
    n_sb = t // DIL_SUPER
    side_in, side_out, side_shape, side_bytes = _side_cast_specs(
        side_w, b * n_heads * n_sb, lambda bi, h, s: (bi * n_heads + h) * n_sb + s)
    n_br = len(DILATED_BRANCHES)
    spans = {DIL_QBLK + window // dil for window, dil in DILATED_BRANCHES}
    assert len(spans) == 1, "bias scratch assumes one key span for all branches"
    span = spans.pop()
    assert t % DIL_SUPER == 0 and DIL_SUPER // DIL_QBLK % DIL_BLOCKS_PER_ITER == 0 and DIL_SUPER % DIL_MERGE_ROWS == 0
    for window, dil in DILATED_BRANCHES:
        assert DIL_SUPER % (dil * DIL_QBLK) == 0 and t // dil >= span and window // (2 * dil) <= DIL_QBLK
    slopes = 2.0 ** (-8.0 * jnp.arange(1, n_heads + 1, dtype=F32) / n_heads)
    kv_blk = lambda off: pl.BlockSpec((None, t, HEAD_DIM), lambda bi, h, s: (bi, 0, off + h))
    blocks = (2 * _nbytes((t, HEAD_DIM), F32) + _nbytes((DIL_SUPER, HEAD_DIM), F32)
              + _nbytes((DIL_SUPER, HEAD_DIM), BF16) + side_bytes)
    stats = pltpu.VMEM((n_br, DIL_SUPER, HEAD_DIM), F32)
    scratch = 3 * _nbytes((n_br, DIL_SUPER, HEAD_DIM), F32) + _nbytes((3 * n_br, DIL_QBLK, span), F32)
    return pl.pallas_call(
        functools.partial(_dilated_kernel, seq_len=t),
        grid=(b, n_heads, t // DIL_SUPER),
        in_specs=[
            pl.BlockSpec(memory_space=pltpu.SMEM),
            pl.BlockSpec((None, DIL_SUPER, HEAD_DIM), lambda bi, h, s: (bi, s, h)),
            kv_blk(n_heads), kv_blk(2 * n_heads), side_in,
        ],
        out_specs=[pl.BlockSpec((None, DIL_SUPER, HEAD_DIM), lambda bi, h, s: (bi, s, h)), side_out],
        out_shape=[jax.ShapeDtypeStruct((b, t, n_heads * HEAD_DIM), BF16), side_shape],
        scratch_shapes=[pltpu.VMEM((3 * n_br, DIL_QBLK, span), F32), stats, stats, stats],
        compiler_params=pltpu.CompilerParams(
            dimension_semantics=("parallel", "parallel", "arbitrary"),
            vmem_limit_bytes=_vmem_limit(blocks, scratch, 8 << 20),
        ),
        name=name,
    )(slopes, proj, proj, proj, side_w)


SWA_QBLK = 256
SWA_BLOCKS_PER_STEP = 8


def _swa_kernel(slope_ref, sink_ref, q_ref, k_ref, v_ref, o_ref, bias_ref, *, seq_len, group):
    kvh = pl.program_id(1)
    span = SWA_QBLK + 2 * C_RADIUS

    @pl.when(pl.program_id(2) == 0)
    def _():
        for g in range(group):
            slope2 = slope_ref[kvh * group + g] * LOG2E
            for var in range(3):
                bias_ref[3 * g + var] = _band_bias(SWA_QBLK, span, var * C_RADIUS, C_RADIUS, slope2)

    for blk in range(SWA_BLOCKS_PER_STEP):
        rows = slice(blk * SWA_QBLK, (blk + 1) * SWA_QBLK)
        s0 = (pl.program_id(2) * SWA_BLOCKS_PER_STEP + blk) * SWA_QBLK
        ks = pl.multiple_of(jnp.clip(s0 - C_RADIUS, 0, seq_len - span), C_RADIUS)
        var = (s0 - ks) // C_RADIUS
        k = k_ref[pl.ds(ks, span), :]
        v = v_ref[pl.ds(ks, span), :]
        for g in range(group):
            cols = slice(g * HEAD_DIM, (g + 1) * HEAD_DIM)
            s = _qk_scores(q_ref[rows, cols], k) + bias_ref[3 * g + var]
            sink2 = sink_ref[kvh * group + g] * LOG2E
            m = jnp.maximum(jnp.max(s, axis=-1, keepdims=True), sink2)
            p = jnp.exp2(s - m).astype(BF16)
            num, den = _pv_with_denominator(p, v)
            o = num / (den + jnp.exp2(sink2 - m))
            o_ref[rows, cols] = o.astype(o_ref.dtype)


def _swa_attention(proj, sink, n_heads, n_kv, *, name):
    b, t, _ = proj.shape
    group = n_heads // n_kv
    span = SWA_QBLK + 2 * C_RADIUS
    step_rows = SWA_QBLK * SWA_BLOCKS_PER_STEP
    assert t % step_rows == 0 and t >= span and C_RADIUS <= SWA_QBLK and SWA_QBLK % C_RADIUS == 0
    slopes = 2.0 ** (-8.0 * jnp.arange(1, n_heads + 1, dtype=F32) / n_heads)
    kv_blk = lambda off: pl.BlockSpec((None, t, HEAD_DIM), lambda bi, kh, s: (bi, 0, off + kh))
    q_blk = pl.BlockSpec((None, step_rows, group * HEAD_DIM), lambda bi, kh, s: (bi, s, kh))
    blocks = 2 * _nbytes((t, HEAD_DIM), BF16) + 2 * _nbytes((step_rows, group * HEAD_DIM), BF16)
    bias_bytes = _nbytes((3 * group, SWA_QBLK, span), F32)
    return pl.pallas_call(
        functools.partial(_swa_kernel, seq_len=t, group=group),
        grid=(b, n_kv, t // step_rows),
        in_specs=[
            pl.BlockSpec(memory_space=pltpu.SMEM),
            pl.BlockSpec(memory_space=pltpu.SMEM),
            q_blk, kv_blk(n_heads), kv_blk(n_heads + n_kv),
        ],
        out_specs=q_blk,
        out_shape=jax.ShapeDtypeStruct((b, t, n_heads * HEAD_DIM), BF16),
        scratch_shapes=[pltpu.VMEM((3 * group, SWA_QBLK, span), F32)],
        compiler_params=pltpu.CompilerParams(
            dimension_semantics=("parallel", "parallel", "arbitrary"),
            vmem_limit_bytes=_vmem_limit(blocks, bias_bytes, 8 << 20),
        ),
        name=name,
    )(slopes, sink.astype(F32), proj, proj, proj)


def _q_scale_row(n_cols, n_q_cols):
    col = jnp.arange(n_cols)
    return jnp.where(col < n_q_cols, LOG2E * HEAD_DIM ** -0.5, 1.0).astype(F32)[None, :]


def kernel(x, attn_norm, mlp_norm, w_mlp_in, w_mlp_out, even_w_in, even_rpb, even_w_out,
           odd_w_qkv, odd_sink, odd_w_out, final_norm):
    b, t, d = x.shape
    depth = attn_norm.shape[0]
    n_heads_a = even_rpb.shape[1]
    wa = n_heads_a * HEAD_DIM
    n_heads_b = (even_w_in.shape[2] - 3 * wa) // (3 * HEAD_DIM)
    n_heads_c = odd_sink.shape[1]
    n_kv_c = (odd_w_qkv.shape[2] // HEAD_DIM - n_heads_c) // 2
    xf = x.reshape(b * t, d)
    w1_bf16 = w2_bf16 = None
    w_qkv_bf16 = w_out_odd_bf16 = None

    for i in range(depth):
        j = i // 2
        if i % 2 == 0:
            w_in = even_w_in[j]
            wb_cols = w_in.shape[1] - 3 * wa
            assert wb_cols == 3 * wa, "the mixer-B column block is addressed as block 1 of two equal halves"
            w_in_a = (w_in[:, :3 * wa] * _q_scale_row(3 * wa, wa)).astype(BF16)
            side = [(w_in, (wb_cols, 1), _q_scale_row(wb_cols, n_heads_b * HEAD_DIM)), (even_w_out[j], None, None)]
            if i + 1 < depth:
                jo = (i + 1) // 2
                side += [(odd_w_qkv[jo], None, _q_scale_row(odd_w_qkv.shape[2], n_heads_c * HEAD_DIM)),
                         (odd_w_out[jo], None, None)]
            proj_a, casts = _norm_matmul(xf, attn_norm[i], w_in_a, BF16, side, name=f"l{i}_proj_a")
            w_in_b, w_out_bf16 = casts[:2]
            if i + 1 < depth:
                w_qkv_bf16, w_out_odd_bf16 = casts[2:]
            w1_pair, w2_pair = w_mlp_in[i:i + 2], w_mlp_out[i:i + 2]
            o_a, proj_b, w1_bf16 = _na_attention_and_proj(
                proj_a.reshape(b, t, -1), _na_bias_table(even_rpb[j]), n_heads_a, xf, attn_norm[i], w_in_b,
                w1_pair.reshape(-1, w1_pair.shape[-1]), name=f"l{i}_na_proj_b")
            o_b, w2_bf16 = _dilated_attention(proj_b.reshape(b, t, -1), n_heads_b,
                                              w2_pair.reshape(-1, w2_pair.shape[-1]), name=f"l{i}_dilated")
            w1_bf16, w2_bf16 = w1_bf16.reshape(w1_pair.shape), w2_bf16.reshape(w2_pair.shape)
            xf = _proj_residual(o_a.reshape(b * t, -1), 0, o_b.reshape(b * t, -1), 0, w_out_bf16, xf, name=f"l{i}_out")
        else:
            proj_c, _ = _norm_matmul(xf, attn_norm[i], w_qkv_bf16, BF16, name=f"l{i}_proj_c")
            o_c = _swa_attention(proj_c.reshape(b, t, -1), odd_sink[j], n_heads_c, n_kv_c, name=f"l{i}_swa")
            o_c = o_c.reshape(b * t, -1)
            xf = _proj_residual(o_c, 0, o_c, 1, w_out_odd_bf16, xf, name=f"l{i}_out")
        g_final = final_norm if i == depth - 1 else None
        xf = _mlp(xf, mlp_norm[i], w1_bf16, w2_bf16, i % 2, g_final, name=f"l{i}_mlp")
    return xf.reshape(b, t, d)
```

```python
import functools
import math

import jax
import jax.numpy as jnp
from jax import lax
from jax.experimental import pallas as pl
from jax.experimental.pallas import tpu as pltpu

HEAD_DIM = 128
GRID_W = 64
NA_ROW_WIN = 8
NA_COL_WIN = 16
DILATED_BRANCHES = ((128, 1), (512, 4), (2048, 16))
C_RADIUS = 128
NORM_EPS = 1e-6
NEG_INF = -1e30
LOG2E = math.log2(math.e)

V7X_VMEM_LIMIT_CAP = 56 * 1024 * 1024
BF16 = jnp.bfloat16
F32 = jnp.float32


def _vmem_limit(block_bytes, scratch_bytes, temp_bytes):
    need = 2 * block_bytes + scratch_bytes + temp_bytes + (2 << 20)
    return int(min(max(need, 16 << 20), V7X_VMEM_LIMIT_CAP))


def _nbytes(shape, dtype):
    n = 1
    for s in shape:
        n *= s
    return n * jnp.dtype(dtype).itemsize


def _rms_norm_f32(x, g):
    y = x * lax.rsqrt(jnp.mean(x * x, axis=-1, keepdims=True) + NORM_EPS)
    return y * g


def _qk_scores(q, k):
    return lax.dot_general(q, k, (((1,), (1,)), ((), ())), preferred_element_type=F32)


def _pv_with_denominator(p, v):
    v_ext = jnp.concatenate([v, jnp.ones_like(v)], axis=1)
    pv = jnp.dot(p, v_ext, preferred_element_type=F32)
    return pv[:, :HEAD_DIM], pv[:, HEAD_DIM:]


def _side_cast_specs(w, n_steps, step_index, col_block=None):
    rows = w.shape[0]
    width, col = (w.shape[1], 0) if col_block is None else col_block
    assert rows % n_steps == 0 and (rows // n_steps) % 16 == 0 and w.shape[1] % width == 0
    rb = rows // n_steps
    in_spec = pl.BlockSpec((rb, width), lambda *ids: (step_index(*ids), col))
    out_spec = pl.BlockSpec((rb, width), lambda *ids: (step_index(*ids), 0))
    return in_spec, out_spec, jax.ShapeDtypeStruct((rows, width), BF16), _nbytes((rb, width), F32) + _nbytes((rb, width), BF16)


def _side_casts(in_refs, out_refs, scaled):
    it = iter(in_refs)
    for o_ref, has_scale in zip(out_refs, scaled):
        w = next(it)[...]
        if has_scale:
            w = w * next(it)[...]
        o_ref[...] = w.astype(o_ref.dtype)


def _norm_matmul_kernel(x_ref, g_ref, w_ref, *rest, side_scaled):
    n_side_in = len(side_scaled) + sum(side_scaled)
    o_ref = rest[n_side_in]
    _side_casts(rest[:n_side_in], rest[n_side_in + 1:], side_scaled)
    h = _rms_norm_f32(x_ref[...], g_ref[...]).astype(BF16)
    o_ref[...] = jnp.dot(h, w_ref[...], preferred_element_type=F32).astype(o_ref.dtype)


def _norm_matmul(x, g, w, out_dtype, side_items=(), *, tm=512, name):
    m, d = x.shape
    n = w.shape[1]
    assert m % tm == 0
    n_steps = m // tm
    side_in_specs, side_args, side_out_specs, side_shapes, side_bytes = [], [], [], [], 0
    for w_side, col_block, scale in side_items:
        in_spec, out_spec, shape, nbytes = _side_cast_specs(w_side, n_steps, lambda i: i, col_block)
        side_in_specs.append(in_spec)
        side_args.append(w_side)
        if scale is not None:
            side_in_specs.append(pl.BlockSpec((1, scale.shape[1]), lambda i: (0, 0)))
            side_args.append(scale)
        side_out_specs.append(out_spec)
        side_shapes.append(shape)
        side_bytes += nbytes
    blocks = _nbytes((tm, d), F32) + _nbytes((tm, n), out_dtype) + side_bytes
    temps = _nbytes((tm, d), BF16) + _nbytes((tm, n), F32)
    outs = pl.pallas_call(
        functools.partial(_norm_matmul_kernel, side_scaled=tuple(scale is not None for _, _, scale in side_items)),
        grid=(n_steps,),
        in_specs=[
            pl.BlockSpec((tm, d), lambda i: (i, 0)),
            pl.BlockSpec((1, d), lambda i: (0, 0)),
            pl.BlockSpec((d, n), lambda i: (0, 0), pipeline_mode=pl.Buffered(1)),
        ] + side_in_specs,
        out_specs=[pl.BlockSpec((tm, n), lambda i: (i, 0))] + side_out_specs,
        out_shape=[jax.ShapeDtypeStruct((m, n), out_dtype)] + side_shapes,
        compiler_params=pltpu.CompilerParams(
            dimension_semantics=("parallel",),
            vmem_limit_bytes=_vmem_limit(blocks, _nbytes((d, n), BF16), temps),
        ),
        name=name,
    )(x, g.reshape(1, d), w, *side_args)
    return outs[0], list(outs[1:])


def _out_mlp_kernel(a_ref, b_ref, wo_ref, x_ref, g_ref, w1_ref, w2_ref, *rest, final_norm):
    gf_ref = rest[0] if final_norm else None
    o_ref, h_ref = rest[-2:]
    f = pl.program_id(1)
    kh = a_ref.shape[1]

    @pl.when(f == 0)
    def _():
        x1 = x_ref[...] + jnp.dot(a_ref[...], wo_ref[:kh, :], preferred_element_type=F32)
        x1 = x1 + jnp.dot(b_ref[...], wo_ref[kh:, :], preferred_element_type=F32)
        h_ref[...] = _rms_norm_f32(x1, g_ref[...]).astype(h_ref.dtype)
        o_ref[...] = x1

    a = jnp.maximum(jnp.dot(h_ref[...], w1_ref[...], preferred_element_type=F32), 0.0)
    o_ref[...] += jnp.dot((a * a).astype(BF16), w2_ref[...], preferred_element_type=F32)

    if final_norm:
        @pl.when(f == pl.num_programs(1) - 1)
        def _():
            o_ref[...] = _rms_norm_f32(o_ref[...], gf_ref[...])


def _out_mlp(a, a_col, b, b_col, w_out, x, g, w1, w2, layer, g_final=None, *, tm=512, tf=1024, name):
    m, d = x.shape
    ff = w1.shape[2]
    kh = w_out.shape[0] // 2
    assert m % tm == 0 and ff % tf == 0
    final_norm = g_final is not None
    blocks = (2 * _nbytes((tm, d), F32) + _nbytes((d, tf), BF16) + _nbytes((tf, d), BF16)
              + 2 * _nbytes((tm, kh), BF16))
    resident = _nbytes((tm, d), BF16) + _nbytes(w_out.shape, BF16)
    temps = _nbytes((tm, tf), F32) + _nbytes((tm, tf), BF16) + _nbytes((tm, d), F32)
    gain_spec = pl.BlockSpec((1, d), lambda i, f: (0, 0))
    extra_specs, extra_args = ([gain_spec], [g_final.reshape(1, d)]) if final_norm else ([], [])
    return pl.pallas_call(
        functools.partial(_out_mlp_kernel, final_norm=final_norm),
        grid=(m // tm, ff // tf),
        in_specs=[
            pl.BlockSpec((tm, kh), lambda i, f: (i, a_col)),
            pl.BlockSpec((tm, kh), lambda i, f: (i, b_col)),
            pl.BlockSpec(w_out.shape, lambda i, f: (0, 0), pipeline_mode=pl.Buffered(1)),
            pl.BlockSpec((tm, d), lambda i, f: (i, 0)),
            gain_spec,
            pl.BlockSpec((None, d, tf), lambda i, f: (layer, 0, f)),
            pl.BlockSpec((None, tf, d), lambda i, f: (layer, f, 0)),
        ] + extra_specs,
        out_specs=pl.BlockSpec((tm, d), lambda i, f: (i, 0)),
        out_shape=jax.ShapeDtypeStruct((m, d), F32),
        scratch_shapes=[pltpu.VMEM((tm, d), BF16)],
        compiler_params=pltpu.CompilerParams(
            dimension_semantics=("parallel", "arbitrary"),
            vmem_limit_bytes=_vmem_limit(blocks, resident, temps),
        ),
        name=name,
    )(a, b, w_out, x, g.reshape(1, d), w1, w2, *extra_args)


NA_ROWS_PER_STEP = 32


def _na_bias_table(rpb):
    n_heads = rpb.shape[0]
    col = jnp.arange(GRID_W)
    col_start = jnp.clip(col - NA_COL_WIN // 2, 0, GRID_W - NA_COL_WIN)
    kc = jnp.arange(GRID_W)[None, :]
    valid = (kc >= col_start[:, None]) & (kc < col_start[:, None] + NA_COL_WIN)
    dc = kc - col[:, None] + (NA_COL_WIN - 1)
    onehot = (dc[:, :, None] == jnp.arange(2 * NA_COL_WIN - 1)[None, None, :]).astype(F32)
    by_col = jnp.einsum("hrd,ckd->hcrk", rpb.astype(F32) * LOG2E, onehot, precision=lax.Precision.HIGHEST)
    by_col = jnp.where(valid[None, :, None, :], by_col, NEG_INF)
    per_delta = [by_col[:, :, NA_ROW_WIN - 1 - dl: 2 * NA_ROW_WIN - 1 - dl].reshape(n_heads, GRID_W, -1)
                 for dl in range(NA_ROW_WIN)]
    return jnp.stack(per_delta, axis=1)


def _na_kernel(q_ref, k_ref, v_ref, bias_ref, x_ref, g_ref, wb_ref, w_ref, o_ref, pb_ref, wo_ref, *, rows):
    span = NA_ROW_WIN * GRID_W
    h = _rms_norm_f32(x_ref[...], g_ref[...]).astype(BF16)
    pb_ref[...] = jnp.dot(h, wb_ref[...], preferred_element_type=F32)
    _side_casts([w_ref], [wo_ref], [False])

    def one_row(r):
        r0 = jnp.clip(r - NA_ROW_WIN // 2, 0, rows - NA_ROW_WIN)
        q_start = pl.multiple_of(r * GRID_W, GRID_W)
        k_start = pl.multiple_of(r0 * GRID_W, GRID_W)
        q = q_ref[pl.ds(q_start, GRID_W), :]
        k = k_ref[pl.ds(k_start, span), :]
        v = v_ref[pl.ds(k_start, span), :]
        s = _qk_scores(q, k) + bias_ref[r - r0]
        m = jnp.max(s, axis=-1, keepdims=True)
        p = jnp.exp2(s - m).astype(BF16)
        num, den = _pv_with_denominator(p, v)
        o_ref[pl.ds(q_start, GRID_W), :] = (num / den).astype(o_ref.dtype)

    for u in range(NA_ROWS_PER_STEP):
        one_row(pl.program_id(2) * NA_ROWS_PER_STEP + u)


def _na_attention_and_proj(proj, bias_tbl, n_heads, x, g, w_b, side_w, *, name):
    b, t, _ = proj.shape
    rows = t // GRID_W
    assert rows >= NA_ROW_WIN and rows % NA_ROWS_PER_STEP == 0
    n_trips = rows // NA_ROWS_PER_STEP
    n_steps = b * n_heads * n_trips
    m, d = x.shape
    n = w_b.shape[1]
    assert m % n_steps == 0 and (m // n_steps) % 16 == 0
    rb = m // n_steps
    span = NA_ROW_WIN * GRID_W
    step_row = lambda bi, h, tr: (bi * n_heads + h) * n_trips + tr
    step_id = lambda bi, h, tr: (step_row(bi, h, tr), 0)
    side_in, side_out, side_shape, side_bytes = _side_cast_specs(side_w, n_steps, step_row)
    head_blk = lambda off: pl.BlockSpec((None, t, HEAD_DIM), lambda bi, h, tr: (bi, 0, off + h))
    blocks = (4 * _nbytes((t, HEAD_DIM), BF16) + _nbytes((NA_ROW_WIN, GRID_W, span), F32)
              + _nbytes((rb, d), F32) + _nbytes((rb, n), F32) + side_bytes)
    return pl.pallas_call(
        functools.partial(_na_kernel, rows=rows),
        grid=(b, n_heads, n_trips),
        in_specs=[
            head_blk(0), head_blk(n_heads), head_blk(2 * n_heads),
            pl.BlockSpec((None, NA_ROW_WIN, GRID_W, span), lambda bi, h, tr: (h, 0, 0, 0)),
            pl.BlockSpec((rb, d), step_id),
            pl.BlockSpec((1, d), lambda bi, h, tr: (0, 0)),
            pl.BlockSpec((d, n), lambda bi, h, tr: (0, 0), pipeline_mode=pl.Buffered(1)),
            side_in,
        ],
        out_specs=[pl.BlockSpec((None, t, HEAD_DIM), lambda bi, h, tr: (bi, 0, h)), pl.BlockSpec((rb, n), step_id),
                   side_out],
        out_shape=[jax.ShapeDtypeStruct((b, t, n_heads * HEAD_DIM), BF16), jax.ShapeDtypeStruct((m, n), F32),
                   side_shape],
        compiler_params=pltpu.CompilerParams(
            dimension_semantics=("parallel", "parallel", "arbitrary"),
            vmem_limit_bytes=_vmem_limit(blocks, _nbytes((d, n), BF16), 8 << 20),
        ),
        name=name,
    )(proj, proj, proj, bias_tbl, x, g.reshape(1, d), w_b, side_w)


DIL_SUPER = 2048
DIL_QBLK = 128
DIL_BLOCKS_PER_ITER = 16
DIL_MERGE_ROWS = 256


def _band_bias(n_q, span, offset, radius, slope2):
    rel = (lax.broadcasted_iota(jnp.int32, (n_q, 1), 0)
           - lax.broadcasted_iota(jnp.int32, (1, span), 1)) + offset
    dist = jnp.abs(rel)
    return jnp.where(dist <= radius, -slope2 * dist.astype(F32), NEG_INF)


def _dilated_kernel(slope_ref, q_ref, k_ref, v_ref, w_ref, o_ref, wo_ref, bias_ref, acc_ref, m_ref, l_ref, *,
                    seq_len):
    h = pl.program_id(1)
    sb = pl.program_id(2)
    _side_casts([w_ref], [wo_ref], [False])

    @pl.when(sb == 0)
    def _():
        slope2 = slope_ref[h] * LOG2E
        for br, (window, dil) in enumerate(DILATED_BRANCHES):
            radius = window // (2 * dil)
            for var in range(3):
                bias_ref[3 * br + var] = _band_bias(DIL_QBLK, DIL_QBLK + 2 * radius, var * radius, radius, slope2 * dil)

    for br, (window, dil) in enumerate(DILATED_BRANCHES):
        radius = window // (2 * dil)
        span = DIL_QBLK + 2 * radius
        sub = seq_len // dil
        blocks_per_res = DIL_SUPER // dil // DIL_QBLK

        def one_block(idx, br=br, dil=dil, radius=radius, span=span, sub=sub, blocks_per_res=blocks_per_res):
            res = idx // blocks_per_res
            c = idx % blocks_per_res
            q_row = res + dil * DIL_QBLK * c
            u0 = sb * (DIL_SUPER // dil) + DIL_QBLK * c
            ws = jnp.clip(u0 - radius, 0, sub - span)
            k_row = res + dil * ws
            if dil == 1:
                q_idx = pl.ds(q_row, DIL_QBLK)
                k_idx = pl.ds(k_row, span)
            else:
                q_idx = pl.ds(q_row, DIL_QBLK, stride=dil)
                k_idx = pl.ds(k_row, span, stride=dil)
            q = q_ref[q_idx, :].astype(BF16)
            k = k_ref[k_idx, :].astype(BF16)
            v = v_ref[k_idx, :].astype(BF16)
            s = _qk_scores(q, k) + bias_ref[3 * br + (u0 - ws) // radius]
            m = jnp.max(s, axis=-1, keepdims=True)
            p = jnp.exp2(s - m).astype(BF16)
            num, den = _pv_with_denominator(p, v)
            acc_ref[br, q_idx, :] = num
            l_ref[br, q_idx, :] = den
            m_ref[br, q_idx, :] = jnp.broadcast_to(m, (DIL_QBLK, HEAD_DIM))

        def body(it, carry, one_block=one_block):
            for u in range(DIL_BLOCKS_PER_ITER):
                one_block(it * DIL_BLOCKS_PER_ITER + u)
            return carry

        lax.fori_loop(0, DIL_SUPER // DIL_QBLK // DIL_BLOCKS_PER_ITER, body, 0)

    n_br = len(DILATED_BRANCHES)

    def merge(it, carry):
        rows = pl.ds(pl.multiple_of(it * DIL_MERGE_ROWS, DIL_MERGE_ROWS), DIL_MERGE_ROWS)
        ms = [m_ref[br, rows, :] for br in range(n_br)]
        m_all = functools.reduce(jnp.maximum, ms)
        ws = [jnp.exp2(m - m_all) for m in ms]
        num = sum(w * acc_ref[br, rows, :] for br, w in enumerate(ws))
        den = sum(w * l_ref[br, rows, :] for br, w in enumerate(ws))
        o_ref[rows, :] = (num / den).astype(o_ref.dtype)
        return carry

    lax.fori_loop(0, DIL_SUPER // DIL_MERGE_ROWS, merge, 0)


def _dilated_attention(proj, n_heads, side_w, *, name):
    b, t, _ = proj.shape
    n_sb = t // DIL_SUPER
    side_in, side_out, side_shape, side_bytes = _side_cast_specs(
        side_w, b * n_heads * n_sb, lambda bi, h, s: (bi * n_heads + h) * n_sb + s)
    n_br = len(DILATED_BRANCHES)
    spans = {DIL_QBLK + window // dil for window, dil in DILATED_BRANCHES}
    assert len(spans) == 1, "bias scratch assumes one key span for all branches"
    span = spans.pop()
    assert t % DIL_SUPER == 0 and DIL_SUPER // DIL_QBLK % DIL_BLOCKS_PER_ITER == 0 and DIL_SUPER % DIL_MERGE_ROWS == 0
    for window, dil in DILATED_BRANCHES:
        assert DIL_SUPER % (dil * DIL_QBLK) == 0 and t // dil >= span and window // (2 * dil) <= DIL_QBLK
    slopes = 2.0 ** (-8.0 * jnp.arange(1, n_heads + 1, dtype=F32) / n_heads)
    kv_blk = lambda off: pl.BlockSpec((None, t, HEAD_DIM), lambda bi, h, s: (bi, 0, off + h))
    blocks = (2 * _nbytes((t, HEAD_DIM), F32) + _nbytes((DIL_SUPER, HEAD_DIM), F32)
              + _nbytes((DIL_SUPER, HEAD_DIM), BF16) + side_bytes)
    stats = pltpu.VMEM((n_br, DIL_SUPER, HEAD_DIM), F32)
    scratch = 3 * _nbytes((n_br, DIL_SUPER, HEAD_DIM), F32) + _nbytes((3 * n_br, DIL_QBLK, span), F32)
    return pl.pallas_call(
        functools.partial(_dilated_kernel, seq_len=t),
        grid=(b, n_heads, t // DIL_SUPER),
        in_specs=[
            pl.BlockSpec(memory_space=pltpu.SMEM),
            pl.BlockSpec((None, DIL_SUPER, HEAD_DIM), lambda bi, h, s: (bi, s, h)),
            kv_blk(n_heads), kv_blk(2 * n_heads), side_in,
        ],
        out_specs=[pl.BlockSpec((None, DIL_SUPER, HEAD_DIM), lambda bi, h, s: (bi, s, h)), side_out],
        out_shape=[jax.ShapeDtypeStruct((b, t, n_heads * HEAD_DIM), BF16), side_shape],
        scratch_shapes=[pltpu.VMEM((3 * n_br, DIL_QBLK, span), F32), stats, stats, stats],
        compiler_params=pltpu.CompilerParams(
            dimension_semantics=("parallel", "parallel", "arbitrary"),
            vmem_limit_bytes=_vmem_limit(blocks, scratch, 8 << 20),
        ),
        name=name,
    )(slopes, proj, proj, proj, side_w)


SWA_QBLK = 256
SWA_BLOCKS_PER_STEP = 8


def _swa_kernel(slope_ref, sink_ref, q_ref, k_ref, v_ref, o_ref, bias_ref, *, seq_len, group):
    kvh = pl.program_id(1)
    span = SWA_QBLK + 2 * C_RADIUS

    @pl.when(pl.program_id(2) == 0)
    def _():
        for g in range(group):
            slope2 = slope_ref[kvh * group + g] * LOG2E
            for var in range(3):
                bias_ref[3 * g + var] = _band_bias(SWA_QBLK, span, var * C_RADIUS, C_RADIUS, slope2)

    for blk in range(SWA_BLOCKS_PER_STEP):
        rows = slice(blk * SWA_QBLK, (blk + 1) * SWA_QBLK)
        s0 = (pl.program_id(2) * SWA_BLOCKS_PER_STEP + blk) * SWA_QBLK
        ks = pl.multiple_of(jnp.clip(s0 - C_RADIUS, 0, seq_len - span), C_RADIUS)
        var = (s0 - ks) // C_RADIUS
        k = k_ref[pl.ds(ks, span), :]
        v = v_ref[pl.ds(ks, span), :]
        for g in range(group):
            cols = slice(g * HEAD_DIM, (g + 1) * HEAD_DIM)
            s = _qk_scores(q_ref[rows, cols], k) + bias_ref[3 * g + var]
            sink2 = sink_ref[kvh * group + g] * LOG2E
            m = jnp.maximum(jnp.max(s, axis=-1, keepdims=True), sink2)
            p = jnp.exp2(s - m).astype(BF16)
            num, den = _pv_with_denominator(p, v)
            o = num / (den + jnp.exp2(sink2 - m))
            o_ref[rows, cols] = o.astype(o_ref.dtype)


def _swa_attention(proj, sink, n_heads, n_kv, *, name):
    b, t, _ = proj.shape
    group = n_heads // n_kv
    span = SWA_QBLK + 2 * C_RADIUS
    step_rows = SWA_QBLK * SWA_BLOCKS_PER_STEP
    assert t % step_rows == 0 and t >= span and C_RADIUS <= SWA_QBLK and SWA_QBLK % C_RADIUS == 0
    slopes = 2.0 ** (-8.0 * jnp.arange(1, n_heads + 1, dtype=F32) / n_heads)
    kv_blk = lambda off: pl.BlockSpec((None, t, HEAD_DIM), lambda bi, kh, s: (bi, 0, off + kh))
    q_blk = pl.BlockSpec((None, step_rows, group * HEAD_DIM), lambda bi, kh, s: (bi, s, kh))
    blocks = 2 * _nbytes((t, HEAD_DIM), BF16) + 2 * _nbytes((step_rows, group * HEAD_DIM), BF16)
    bias_bytes = _nbytes((3 * group, SWA_QBLK, span), F32)
    return pl.pallas_call(
        functools.partial(_swa_kernel, seq_len=t, group=group),
        grid=(b, n_kv, t // step_rows),
        in_specs=[
            pl.BlockSpec(memory_space=pltpu.SMEM),
            pl.BlockSpec(memory_space=pltpu.SMEM),
            q_blk, kv_blk(n_heads), kv_blk(n_heads + n_kv),
        ],
        out_specs=q_blk,
        out_shape=jax.ShapeDtypeStruct((b, t, n_heads * HEAD_DIM), BF16),
        scratch_shapes=[pltpu.VMEM((3 * group, SWA_QBLK, span), F32)],
        compiler_params=pltpu.CompilerParams(
            dimension_semantics=("parallel", "parallel", "arbitrary"),
            vmem_limit_bytes=_vmem_limit(blocks, bias_bytes, 8 << 20),
        ),
        name=name,
    )(slopes, sink.astype(F32), proj, proj, proj)


def _q_scale_row(n_cols, n_q_cols):
    col = jnp.arange(n_cols)
    return jnp.where(col < n_q_cols, LOG2E * HEAD_DIM ** -0.5, 1.0).astype(F32)[None, :]


def kernel(x, attn_norm, mlp_norm, w_mlp_in, w_mlp_out, even_w_in, even_rpb, even_w_out,
           odd_w_qkv, odd_sink, odd_w_out, final_norm):
    b, t, d = x.shape
    depth = attn_norm.shape[0]
    n_heads_a = even_rpb.shape[1]
    wa = n_heads_a * HEAD_DIM
    n_heads_b = (even_w_in.shape[2] - 3 * wa) // (3 * HEAD_DIM)
    n_heads_c = odd_sink.shape[1]
    n_kv_c = (odd_w_qkv.shape[2] // HEAD_DIM - n_heads_c) // 2
    xf = x.reshape(b * t, d)
    w1_bf16 = w2_bf16 = None
    w_qkv_bf16 = w_out_odd_bf16 = None

    for i in range(depth):
        j = i // 2
        if i % 2 == 0:
            w_in = even_w_in[j]
            wb_cols = w_in.shape[1] - 3 * wa
            assert wb_cols == 3 * wa, "the mixer-B column block is addressed as block 1 of two equal halves"
            w_in_a = (w_in[:, :3 * wa] * _q_scale_row(3 * wa, wa)).astype(BF16)
            side = [(w_in, (wb_cols, 1), _q_scale_row(wb_cols, n_heads_b * HEAD_DIM)), (even_w_out[j], None, None)]
            if i + 1 < depth:
                jo = (i + 1) // 2
                side += [(odd_w_qkv[jo], None, _q_scale_row(odd_w_qkv.shape[2], n_heads_c * HEAD_DIM)),
                         (odd_w_out[jo], None, None)]
            proj_a, casts = _norm_matmul(xf, attn_norm[i], w_in_a, BF16, side, name=f"l{i}_proj_a")
            w_in_b, w_out_bf16 = casts[:2]
            if i + 1 < depth:
                w_qkv_bf16, w_out_odd_bf16 = casts[2:]
            w1_pair, w2_pair = w_mlp_in[i:i + 2], w_mlp_out[i:i + 2]
            o_a, proj_b, w1_bf16 = _na_attention_and_proj(
                proj_a.reshape(b, t, -1), _na_bias_table(even_rpb[j]), n_heads_a, xf, attn_norm[i], w_in_b,
                w1_pair.reshape(-1, w1_pair.shape[-1]), name=f"l{i}_na_proj_b")
            o_b, w2_bf16 = _dilated_attention(proj_b.reshape(b, t, -1), n_heads_b,
                                              w2_pair.reshape(-1, w2_pair.shape[-1]), name=f"l{i}_dilated")
            w1_bf16, w2_bf16 = w1_bf16.reshape(w1_pair.shape), w2_bf16.reshape(w2_pair.shape)
            attn_out = (o_a.reshape(b * t, -1), 0, o_b.reshape(b * t, -1), 0, w_out_bf16)
        else:
            proj_c, _ = _norm_matmul(xf, attn_norm[i], w_qkv_bf16, BF16, name=f"l{i}_proj_c")
            o_c = _swa_attention(proj_c.reshape(b, t, -1), odd_sink[j], n_heads_c, n_kv_c, name=f"l{i}_swa")
            o_c = o_c.reshape(b * t, -1)
            attn_out = (o_c, 0, o_c, 1, w_out_odd_bf16)
        g_final = final_norm if i == depth - 1 else None
        xf = _out_mlp(*attn_out, xf, mlp_norm[i], w1_bf16, w2_bf16, i % 2, g_final, name=f"l{i}_out_mlp")
    return xf.reshape(b, t, d)
```

```python
import functools
import math

import jax
import jax.numpy as jnp
from jax import lax
from jax.experimental import pallas as pl
from jax.experimental.pallas import tpu as pltpu

HEAD_DIM = 128
GRID_W = 64
NA_ROW_WIN = 8
NA_COL_WIN = 16
DILATED_BRANCHES = ((128, 1), (512, 4), (2048, 16))
C_RADIUS = 128
NORM_EPS = 1e-6
NEG_INF = -1e30
LOG2E = math.log2(math.e)

V7X_VMEM_LIMIT_CAP = 56 * 1024 * 1024
VMEM_INTERNAL_SCRATCH = 2 * 1024 * 1024
ATTN_TEMP_BYTES = 8 * 1024 * 1024
BF16_ROW_TILE = 16
BF16 = jnp.bfloat16
F32 = jnp.float32


def _vmem_limit(block_bytes, scratch_bytes, temp_bytes):
    need = 2 * block_bytes + scratch_bytes + temp_bytes + VMEM_INTERNAL_SCRATCH
    return int(min(need, V7X_VMEM_LIMIT_CAP))


def _nbytes(shape, dtype):
    n = 1
    for s in shape:
        n *= s
    return n * jnp.dtype(dtype).itemsize


def _rms_norm_f32(x, g):
    y = x * lax.rsqrt(jnp.mean(x * x, axis=-1, keepdims=True) + NORM_EPS)
    return y * g


def _qk_scores(q, k):
    return lax.dot_general(q, k, (((1,), (1,)), ((), ())), preferred_element_type=F32)


def _pv_with_denominator(p, v):
    v_ext = jnp.concatenate([v, jnp.ones_like(v)], axis=1)
    pv = jnp.dot(p, v_ext, preferred_element_type=F32)
    return pv[:, :HEAD_DIM], pv[:, HEAD_DIM:]


def _side_cast_specs(w, n_steps, step_index, col_block=None):
    rows = w.shape[0]
    width, col = (w.shape[1], 0) if col_block is None else col_block
    assert rows % n_steps == 0 and (rows // n_steps) % BF16_ROW_TILE == 0 and w.shape[1] % width == 0
    rb = rows // n_steps
    in_spec = pl.BlockSpec((rb, width), lambda *ids: (step_index(*ids), col))
    out_spec = pl.BlockSpec((rb, width), lambda *ids: (step_index(*ids), 0))
    return in_spec, out_spec, jax.ShapeDtypeStruct((rows, width), BF16), _nbytes((rb, width), F32) + _nbytes((rb, width), BF16)


def _side_casts(in_refs, out_refs, scaled):
    it = iter(in_refs)
    for o_ref, has_scale in zip(out_refs, scaled):
        w = next(it)[...]
        if has_scale:
            w = w * next(it)[...]
        o_ref[...] = w.astype(o_ref.dtype)


def _norm_matmul_kernel(x_ref, g_ref, w_ref, *rest, side_scaled):
    n_side_in = len(side_scaled) + sum(side_scaled)
    o_ref = rest[n_side_in]
    _side_casts(rest[:n_side_in], rest[n_side_in + 1:], side_scaled)
    h = _rms_norm_f32(x_ref[...], g_ref[...]).astype(BF16)
    o_ref[...] = jnp.dot(h, w_ref[...], preferred_element_type=F32).astype(o_ref.dtype)


def _norm_matmul(x, g, w, out_dtype, side_items=(), *, tm=512, name):
    m, d = x.shape
    n = w.shape[1]
    assert m % tm == 0
    n_steps = m // tm
    side_in_specs, side_args, side_out_specs, side_shapes, side_bytes = [], [], [], [], 0
    for w_side, col_block, scale in side_items:
        in_spec, out_spec, shape, nbytes = _side_cast_specs(w_side, n_steps, lambda i: i, col_block)
        side_in_specs.append(in_spec)
        side_args.append(w_side)
        if scale is not None:
            side_in_specs.append(pl.BlockSpec((1, scale.shape[1]), lambda i: (0, 0)))
            side_args.append(scale)
        side_out_specs.append(out_spec)
        side_shapes.append(shape)
        side_bytes += nbytes
    blocks = _nbytes((tm, d), F32) + _nbytes((tm, n), out_dtype) + side_bytes
    temps = _nbytes((tm, d), BF16) + _nbytes((tm, n), F32)
    outs = pl.pallas_call(
        functools.partial(_norm_matmul_kernel, side_scaled=tuple(scale is not None for _, _, scale in side_items)),
        grid=(n_steps,),
        in_specs=[
            pl.BlockSpec((tm, d), lambda i: (i, 0)),
            pl.BlockSpec((1, d), lambda i: (0, 0)),
            pl.BlockSpec((d, n), lambda i: (0, 0), pipeline_mode=pl.Buffered(1)),
        ] + side_in_specs,
        out_specs=[pl.BlockSpec((tm, n), lambda i: (i, 0))] + side_out_specs,
        out_shape=[jax.ShapeDtypeStruct((m, n), out_dtype)] + side_shapes,
        compiler_params=pltpu.CompilerParams(
            dimension_semantics=("parallel",),
            vmem_limit_bytes=_vmem_limit(blocks, _nbytes((d, n), BF16), temps),
        ),
        name=name,
    )(x, g.reshape(1, d), w, *side_args)
    return outs[0], list(outs[1:])


def _out_mlp_kernel(a_ref, b_ref, wo_ref, x_ref, g_ref, w1_ref, w2_ref, *rest, final_norm):
    gf_ref = rest[0] if final_norm else None
    o_ref, h_ref = rest[-2:]
    f = pl.program_id(1)
    kh = a_ref.shape[1]

    @pl.when(f == 0)
    def _():
        x1 = x_ref[...] + jnp.dot(a_ref[...], wo_ref[:kh, :], preferred_element_type=F32)
        x1 = x1 + jnp.dot(b_ref[...], wo_ref[kh:, :], preferred_element_type=F32)
        h_ref[...] = _rms_norm_f32(x1, g_ref[...]).astype(h_ref.dtype)
        o_ref[...] = x1

    a = jnp.maximum(jnp.dot(h_ref[...], w1_ref[...], preferred_element_type=F32), 0.0)
    o_ref[...] += jnp.dot((a * a).astype(BF16), w2_ref[...], preferred_element_type=F32)

    if final_norm:
        @pl.when(f == pl.num_programs(1) - 1)
        def _():
            o_ref[...] = _rms_norm_f32(o_ref[...], gf_ref[...])


def _out_mlp(a, a_col, b, b_col, w_out, x, g, w1, w2, layer, g_final=None, *, tm=512, tf=1024, name):
    m, d = x.shape
    ff = w1.shape[2]
    kh = w_out.shape[0] // 2
    assert m % tm == 0 and ff % tf == 0
    final_norm = g_final is not None
    blocks = (2 * _nbytes((tm, d), F32) + _nbytes((d, tf), BF16) + _nbytes((tf, d), BF16)
              + 2 * _nbytes((tm, kh), BF16))
    resident = _nbytes((tm, d), BF16) + _nbytes(w_out.shape, BF16)
    temps = _nbytes((tm, tf), F32) + _nbytes((tm, tf), BF16) + _nbytes((tm, d), F32)
    gain_spec = pl.BlockSpec((1, d), lambda i, f: (0, 0))
    extra_specs, extra_args = ([gain_spec], [g_final.reshape(1, d)]) if final_norm else ([], [])
    return pl.pallas_call(
        functools.partial(_out_mlp_kernel, final_norm=final_norm),
        grid=(m // tm, ff // tf),
        in_specs=[
            pl.BlockSpec((tm, kh), lambda i, f: (i, a_col)),
            pl.BlockSpec((tm, kh), lambda i, f: (i, b_col)),
            pl.BlockSpec(w_out.shape, lambda i, f: (0, 0), pipeline_mode=pl.Buffered(1)),
            pl.BlockSpec((tm, d), lambda i, f: (i, 0)),
            gain_spec,
            pl.BlockSpec((None, d, tf), lambda i, f: (layer, 0, f)),
            pl.BlockSpec((None, tf, d), lambda i, f: (layer, f, 0)),
        ] + extra_specs,
        out_specs=pl.BlockSpec((tm, d), lambda i, f: (i, 0)),
        out_shape=jax.ShapeDtypeStruct((m, d), F32),
        scratch_shapes=[pltpu.VMEM((tm, d), BF16)],
        compiler_params=pltpu.CompilerParams(
            dimension_semantics=("parallel", "arbitrary"),
            vmem_limit_bytes=_vmem_limit(blocks, resident, temps),
        ),
        name=name,
    )(a, b, w_out, x, g.reshape(1, d), w1, w2, *extra_args)


NA_ROWS_PER_STEP = 32


def _na_bias_table(rpb):
    n_heads = rpb.shape[0]
    col = jnp.arange(GRID_W)
    col_start = jnp.clip(col - NA_COL_WIN // 2, 0, GRID_W - NA_COL_WIN)
    kc = jnp.arange(GRID_W)[None, :]
    valid = (kc >= col_start[:, None]) & (kc < col_start[:, None] + NA_COL_WIN)
    dc = kc - col[:, None] + (NA_COL_WIN - 1)
    onehot = (dc[:, :, None] == jnp.arange(2 * NA_COL_WIN - 1)[None, None, :]).astype(F32)
    by_col = jnp.einsum("hrd,ckd->hcrk", rpb.astype(F32) * LOG2E, onehot, precision=lax.Precision.HIGHEST)
    by_col = jnp.where(valid[None, :, None, :], by_col, NEG_INF)
    per_delta = [by_col[:, :, NA_ROW_WIN - 1 - dl: 2 * NA_ROW_WIN - 1 - dl] for dl in range(NA_ROW_WIN)]
    tbl = jnp.stack(per_delta, axis=1)
    return tbl.reshape(n_heads, NA_ROW_WIN, GRID_W, NA_ROW_WIN * GRID_W)


def _na_kernel(q_ref, k_ref, v_ref, bias_ref, x_ref, g_ref, wb_ref, w_ref, o_ref, pb_ref, wo_ref, *, rows):
    span = NA_ROW_WIN * GRID_W
    h = _rms_norm_f32(x_ref[...], g_ref[...]).astype(BF16)
    pb_ref[...] = jnp.dot(h, wb_ref[...], preferred_element_type=F32)
    _side_casts([w_ref], [wo_ref], [False])

    def one_row(r):
        r0 = jnp.clip(r - NA_ROW_WIN // 2, 0, rows - NA_ROW_WIN)
        q_start = pl.multiple_of(r * GRID_W, GRID_W)
        k_start = pl.multiple_of(r0 * GRID_W, GRID_W)
        q = q_ref[pl.ds(q_start, GRID_W), :]
        k = k_ref[pl.ds(k_start, span), :]
        v = v_ref[pl.ds(k_start, span), :]
        s = _qk_scores(q, k) + bias_ref[r - r0]
        m = jnp.max(s, axis=-1, keepdims=True)
        p = jnp.exp2(s - m).astype(BF16)
        num, den = _pv_with_denominator(p, v)
        o_ref[pl.ds(q_start, GRID_W), :] = (num / den).astype(o_ref.dtype)

    for u in range(NA_ROWS_PER_STEP):
        one_row(pl.program_id(2) * NA_ROWS_PER_STEP + u)


def _na_attention_and_proj(proj, bias_tbl, n_heads, x, g, w_b, side_w, *, name):
    b, t, _ = proj.shape
    rows = t // GRID_W
    assert rows >= NA_ROW_WIN and rows % NA_ROWS_PER_STEP == 0
    n_trips = rows // NA_ROWS_PER_STEP
    n_steps = b * n_heads * n_trips
    m, d = x.shape
    n = w_b.shape[1]
    assert m % n_steps == 0 and (m // n_steps) % BF16_ROW_TILE == 0
    rb = m // n_steps
    span = NA_ROW_WIN * GRID_W
    step_row = lambda bi, h, tr: (bi * n_heads + h) * n_trips + tr
    step_id = lambda bi, h, tr: (step_row(bi, h, tr), 0)
    side_in, side_out, side_shape, side_bytes = _side_cast_specs(side_w, n_steps, step_row)
    head_blk = lambda off: pl.BlockSpec((None, t, HEAD_DIM), lambda bi, h, tr: (bi, 0, off + h))
    blocks = (4 * _nbytes((t, HEAD_DIM), BF16) + _nbytes((NA_ROW_WIN, GRID_W, span), F32)
              + _nbytes((rb, d), F32) + _nbytes((rb, n), F32) + side_bytes)
    return pl.pallas_call(
        functools.partial(_na_kernel, rows=rows),
        grid=(b, n_heads, n_trips),
        in_specs=[
            head_blk(0), head_blk(n_heads), head_blk(2 * n_heads),
            pl.BlockSpec((None, NA_ROW_WIN, GRID_W, span), lambda bi, h, tr: (h, 0, 0, 0)),
            pl.BlockSpec((rb, d), step_id),
            pl.BlockSpec((1, d), lambda bi, h, tr: (0, 0)),
            pl.BlockSpec((d, n), lambda bi, h, tr: (0, 0), pipeline_mode=pl.Buffered(1)),
            side_in,
        ],
        out_specs=[pl.BlockSpec((None, t, HEAD_DIM), lambda bi, h, tr: (bi, 0, h)), pl.BlockSpec((rb, n), step_id),
                   side_out],
        out_shape=[jax.ShapeDtypeStruct((b, t, n_heads * HEAD_DIM), BF16), jax.ShapeDtypeStruct((m, n), F32),
                   side_shape],
        compiler_params=pltpu.CompilerParams(
            dimension_semantics=("parallel", "parallel", "arbitrary"),
            vmem_limit_bytes=_vmem_limit(blocks, _nbytes((d, n), BF16), ATTN_TEMP_BYTES),
        ),
        name=name,
    )(proj, proj, proj, bias_tbl, x, g.reshape(1, d), w_b, side_w)


DIL_SUPER = 2048
DIL_QBLK = 128
DIL_BLOCKS_PER_ITER = 16
DIL_MERGE_ROWS = 256


def _band_bias(n_q, span, offset, radius, slope2):
    rel = (lax.broadcasted_iota(jnp.int32, (n_q, 1), 0)
           - lax.broadcasted_iota(jnp.int32, (1, span), 1)) + offset
    dist = jnp.abs(rel)
    return jnp.where(dist <= radius, -slope2 * dist.astype(F32), NEG_INF)


def _dilated_kernel(slope_ref, q_ref, k_ref, v_ref, w_ref, o_ref, wo_ref, bias_ref, acc_ref, m_ref, l_ref, *,
                    seq_len):
    h = pl.program_id(1)
    sb = pl.program_id(2)
    _side_casts([w_ref], [wo_ref], [False])

    @pl.when(sb == 0)
    def _():
        slope2 = slope_ref[h] * LOG2E
        for br, (window, dil) in enumerate(DILATED_BRANCHES):
            radius = window // (2 * dil)
            for var in range(3):
                bias_ref[3 * br + var] = _band_bias(DIL_QBLK, DIL_QBLK + 2 * radius, var * radius, radius, slope2 * dil)

    for br, (window, dil) in enumerate(DILATED_BRANCHES):
        radius = window // (2 * dil)
        span = DIL_QBLK + 2 * radius
        sub = seq_len // dil
        blocks_per_res = DIL_SUPER // dil // DIL_QBLK

        def one_block(idx, br=br, dil=dil, radius=radius, span=span, sub=sub, blocks_per_res=blocks_per_res):
            res = idx // blocks_per_res
            c = idx % blocks_per_res
            q_row = res + dil * DIL_QBLK * c
            u0 = sb * (DIL_SUPER // dil) + DIL_QBLK * c
            ws = jnp.clip(u0 - radius, 0, sub - span)
            k_row = res + dil * ws
            if dil == 1:
                q_idx = pl.ds(q_row, DIL_QBLK)
                k_idx = pl.ds(k_row, span)
            else:
                q_idx = pl.ds(q_row, DIL_QBLK, stride=dil)
                k_idx = pl.ds(k_row, span, stride=dil)
            q = q_ref[q_idx, :].astype(BF16)
            k = k_ref[k_idx, :].astype(BF16)
            v = v_ref[k_idx, :].astype(BF16)
            s = _qk_scores(q, k) + bias_ref[3 * br + (u0 - ws) // radius]
            m = jnp.max(s, axis=-1, keepdims=True)
            p = jnp.exp2(s - m).astype(BF16)
            num, den = _pv_with_denominator(p, v)
            acc_ref[br, q_idx, :] = num
            l_ref[br, q_idx, :] = den
            m_ref[br, q_idx, :] = jnp.broadcast_to(m, (DIL_QBLK, HEAD_DIM))

        def body(it, carry, one_block=one_block):
            for u in range(DIL_BLOCKS_PER_ITER):
                one_block(it * DIL_BLOCKS_PER_ITER + u)
            return carry

        lax.fori_loop(0, DIL_SUPER // DIL_QBLK // DIL_BLOCKS_PER_ITER, body, 0)

    n_br = len(DILATED_BRANCHES)

    def merge(it, carry):
        rows = pl.ds(pl.multiple_of(it * DIL_MERGE_ROWS, DIL_MERGE_ROWS), DIL_MERGE_ROWS)
        ms = [m_ref[br, rows, :] for br in range(n_br)]
        m_all = functools.reduce(jnp.maximum, ms)
        ws = [jnp.exp2(m - m_all) for m in ms]
        num = sum(w * acc_ref[br, rows, :] for br, w in enumerate(ws))
        den = sum(w * l_ref[br, rows, :] for br, w in enumerate(ws))
        o_ref[rows, :] = (num / den).astype(o_ref.dtype)
        return carry

    lax.fori_loop(0, DIL_SUPER // DIL_MERGE_ROWS, merge, 0)


def _dilated_attention(proj, n_heads, side_w, *, name):
    b, t, _ = proj.shape
    n_sb = t // DIL_SUPER
    side_in, side_out, side_shape, side_bytes = _side_cast_specs(
        side_w, b * n_heads * n_sb, lambda bi, h, s: (bi * n_heads + h) * n_sb + s)
    n_br = len(DILATED_BRANCHES)
    spans = {DIL_QBLK + window // dil for window, dil in DILATED_BRANCHES}
    assert len(spans) == 1, "bias scratch assumes one key span for all branches"
    span = spans.pop()
    assert t % DIL_SUPER == 0 and DIL_SUPER // DIL_QBLK % DIL_BLOCKS_PER_ITER == 0 and DIL_SUPER % DIL_MERGE_ROWS == 0
    for window, dil in DILATED_BRANCHES:
        assert DIL_SUPER % (dil * DIL_QBLK) == 0 and t // dil >= span and window // (2 * dil) <= DIL_QBLK
    slopes = 2.0 ** (-8.0 * jnp.arange(1, n_heads + 1, dtype=F32) / n_heads)
    kv_blk = lambda off: pl.BlockSpec((None, t, HEAD_DIM), lambda bi, h, s: (bi, 0, off + h))
    blocks = (2 * _nbytes((t, HEAD_DIM), F32) + _nbytes((DIL_SUPER, HEAD_DIM), F32)
              + _nbytes((DIL_SUPER, HEAD_DIM), BF16) + side_bytes)
    stats = pltpu.VMEM((n_br, DIL_SUPER, HEAD_DIM), F32)
    scratch = 3 * _nbytes((n_br, DIL_SUPER, HEAD_DIM), F32) + _nbytes((3 * n_br, DIL_QBLK, span), F32)
    return pl.pallas_call(
        functools.partial(_dilated_kernel, seq_len=t),
        grid=(b, n_heads, t // DIL_SUPER),
        in_specs=[
            pl.BlockSpec(memory_space=pltpu.SMEM),
            pl.BlockSpec((None, DIL_SUPER, HEAD_DIM), lambda bi, h, s: (bi, s, h)),
            kv_blk(n_heads), kv_blk(2 * n_heads), side_in,
        ],
        out_specs=[pl.BlockSpec((None, DIL_SUPER, HEAD_DIM), lambda bi, h, s: (bi, s, h)), side_out],
        out_shape=[jax.ShapeDtypeStruct((b, t, n_heads * HEAD_DIM), BF16), side_shape],
        scratch_shapes=[pltpu.VMEM((3 * n_br, DIL_QBLK, span), F32), stats, stats, stats],
        compiler_params=pltpu.CompilerParams(
            dimension_semantics=("parallel", "parallel", "arbitrary"),
            vmem_limit_bytes=_vmem_limit(blocks, scratch, ATTN_TEMP_BYTES),
        ),
        name=name,
    )(slopes, proj, proj, proj, side_w)


SWA_QBLK = 256
SWA_BLOCKS_PER_STEP = 8


def _swa_kernel(slope_ref, sink_ref, q_ref, k_ref, v_ref, o_ref, bias_ref, *, seq_len, group):
    kvh = pl.program_id(1)
    span = SWA_QBLK + 2 * C_RADIUS

    @pl.when(pl.program_id(2) == 0)
    def _():
        for g in range(group):
            slope2 = slope_ref[kvh * group + g] * LOG2E
            for var in range(3):
                bias_ref[3 * g + var] = _band_bias(SWA_QBLK, span, var * C_RADIUS, C_RADIUS, slope2)

    for blk in range(SWA_BLOCKS_PER_STEP):
        rows = slice(blk * SWA_QBLK, (blk + 1) * SWA_QBLK)
        s0 = (pl.program_id(2) * SWA_BLOCKS_PER_STEP + blk) * SWA_QBLK
        ks = pl.multiple_of(jnp.clip(s0 - C_RADIUS, 0, seq_len - span), C_RADIUS)
        var = (s0 - ks) // C_RADIUS
        k = k_ref[pl.ds(ks, span), :]
        v = v_ref[pl.ds(ks, span), :]
        for g in range(group):
            cols = slice(g * HEAD_DIM, (g + 1) * HEAD_DIM)
            s = _qk_scores(q_ref[rows, cols], k) + bias_ref[3 * g + var]
            sink2 = sink_ref[kvh * group + g] * LOG2E
            m = jnp.maximum(jnp.max(s, axis=-1, keepdims=True), sink2)
            p = jnp.exp2(s - m).astype(BF16)
            num, den = _pv_with_denominator(p, v)
            o = num / (den + jnp.exp2(sink2 - m))
            o_ref[rows, cols] = o.astype(o_ref.dtype)


def _swa_attention(proj, sink, n_heads, n_kv, *, name):
    b, t, _ = proj.shape
    group = n_heads // n_kv
    span = SWA_QBLK + 2 * C_RADIUS
    step_rows = SWA_QBLK * SWA_BLOCKS_PER_STEP
    assert t % step_rows == 0 and t >= span and C_RADIUS <= SWA_QBLK and SWA_QBLK % C_RADIUS == 0
    slopes = 2.0 ** (-8.0 * jnp.arange(1, n_heads + 1, dtype=F32) / n_heads)
    kv_blk = lambda off: pl.BlockSpec((None, t, HEAD_DIM), lambda bi, kh, s: (bi, 0, off + kh))
    q_blk = pl.BlockSpec((None, step_rows, group * HEAD_DIM), lambda bi, kh, s: (bi, s, kh))
    blocks = 2 * _nbytes((t, HEAD_DIM), BF16) + 2 * _nbytes((step_rows, group * HEAD_DIM), BF16)
    bias_bytes = _nbytes((3 * group, SWA_QBLK, span), F32)
    return pl.pallas_call(
        functools.partial(_swa_kernel, seq_len=t, group=group),
        grid=(b, n_kv, t // step_rows),
        in_specs=[
            pl.BlockSpec(memory_space=pltpu.SMEM),
            pl.BlockSpec(memory_space=pltpu.SMEM),
            q_blk, kv_blk(n_heads), kv_blk(n_heads + n_kv),
        ],
        out_specs=q_blk,
        out_shape=jax.ShapeDtypeStruct((b, t, n_heads * HEAD_DIM), BF16),
        scratch_shapes=[pltpu.VMEM((3 * group, SWA_QBLK, span), F32)],
        compiler_params=pltpu.CompilerParams(
            dimension_semantics=("parallel", "parallel", "arbitrary"),
            vmem_limit_bytes=_vmem_limit(blocks, bias_bytes, ATTN_TEMP_BYTES),
        ),
        name=name,
    )(slopes, sink.astype(F32), proj, proj, proj)


def _q_scale_row(n_cols, n_q_cols):
    col = jnp.arange(n_cols)
    return jnp.where(col < n_q_cols, LOG2E * HEAD_DIM ** -0.5, 1.0).astype(F32)[None, :]


def kernel(x, attn_norm, mlp_norm, w_mlp_in, w_mlp_out, even_w_in, even_rpb, even_w_out,
           odd_w_qkv, odd_sink, odd_w_out, final_norm):
    b, t, d = x.shape
    depth = attn_norm.shape[0]
    n_heads_a = even_rpb.shape[1]
    wa = n_heads_a * HEAD_DIM
    n_heads_b = (even_w_in.shape[2] - 3 * wa) // (3 * HEAD_DIM)
    n_heads_c = odd_sink.shape[1]
    n_kv_c = (odd_w_qkv.shape[2] // HEAD_DIM - n_heads_c) // 2
    xf = x.reshape(b * t, d)
    w1_bf16 = w2_bf16 = None
    w_qkv_bf16 = w_out_odd_bf16 = None

    for i in range(depth):
        j = i // 2
        if i % 2 == 0:
            w_in = even_w_in[j]
            wb_cols = w_in.shape[1] - 3 * wa
            assert wb_cols == 3 * wa, "the mixer-B column block is addressed as block 1 of two equal halves"
            w_in_a = (w_in[:, :3 * wa] * _q_scale_row(3 * wa, wa)).astype(BF16)
            side = [(w_in, (wb_cols, 1), _q_scale_row(wb_cols, n_heads_b * HEAD_DIM)), (even_w_out[j], None, None)]
            if i + 1 < depth:
                jo = (i + 1) // 2
                side += [(odd_w_qkv[jo], None, _q_scale_row(odd_w_qkv.shape[2], n_heads_c * HEAD_DIM)),
                         (odd_w_out[jo], None, None)]
            proj_a, casts = _norm_matmul(xf, attn_norm[i], w_in_a, BF16, side, name=f"l{i}_proj_a")
            w_in_b, w_out_bf16 = casts[:2]
            if i + 1 < depth:
                w_qkv_bf16, w_out_odd_bf16 = casts[2:]
            w1_pair, w2_pair = w_mlp_in[i:i + 2], w_mlp_out[i:i + 2]
            o_a, proj_b, w1_bf16 = _na_attention_and_proj(
                proj_a.reshape(b, t, -1), _na_bias_table(even_rpb[j]), n_heads_a, xf, attn_norm[i], w_in_b,
                w1_pair.reshape(-1, w1_pair.shape[-1]), name=f"l{i}_na_proj_b")
            o_b, w2_bf16 = _dilated_attention(proj_b.reshape(b, t, -1), n_heads_b,
                                              w2_pair.reshape(-1, w2_pair.shape[-1]), name=f"l{i}_dilated")
            w1_bf16, w2_bf16 = w1_bf16.reshape(w1_pair.shape), w2_bf16.reshape(w2_pair.shape)
            attn_out = (o_a.reshape(b * t, -1), 0, o_b.reshape(b * t, -1), 0, w_out_bf16)
        else:
            proj_c, _ = _norm_matmul(xf, attn_norm[i], w_qkv_bf16, BF16, name=f"l{i}_proj_c")
            o_c = _swa_attention(proj_c.reshape(b, t, -1), odd_sink[j], n_heads_c, n_kv_c, name=f"l{i}_swa")
            o_c = o_c.reshape(b * t, -1)
            attn_out = (o_c, 0, o_c, 1, w_out_odd_bf16)
        g_final = final_norm if i == depth - 1 else None
        xf = _out_mlp(*attn_out, xf, mlp_norm[i], w1_bf16, w2_bf16, i % 2, g_final, name=f"l{i}_out_mlp")
    return xf.reshape(b, t, d)
```

```python
import functools
import math

import jax
import jax.numpy as jnp
from jax import lax
from jax.experimental import pallas as pl
from jax.experimental.pallas import tpu as pltpu

HEAD_DIM = 128
GRID_W = 64
NA_ROW_WIN = 8
NA_COL_WIN = 16
DILATED_BRANCHES = ((128, 1), (512, 4), (2048, 16))
C_RADIUS = 128
NORM_EPS = 1e-6
NEG_INF = -1e30
LOG2E = math.log2(math.e)

V7X_VMEM_LIMIT_CAP = 56 * 1024 * 1024
VMEM_INTERNAL_SCRATCH = 2 * 1024 * 1024
ATTN_TEMP_BYTES = 8 * 1024 * 1024
BF16_ROW_TILE = 16
BF16 = jnp.bfloat16
F32 = jnp.float32


def _vmem_limit(block_bytes, scratch_bytes, temp_bytes):
    need = 2 * block_bytes + scratch_bytes + temp_bytes + VMEM_INTERNAL_SCRATCH
    return int(min(need, V7X_VMEM_LIMIT_CAP))


def _nbytes(shape, dtype):
    n = 1
    for s in shape:
        n *= s
    return n * jnp.dtype(dtype).itemsize


def _rms_norm_f32(x, g):
    y = x * lax.rsqrt(jnp.mean(x * x, axis=-1, keepdims=True) + NORM_EPS)
    return y * g


def _qk_scores(q, k):
    return lax.dot_general(q, k, (((1,), (1,)), ((), ())), preferred_element_type=F32)


def _pv_with_denominator(p, v):
    v_ext = jnp.concatenate([v, jnp.ones_like(v)], axis=1)
    pv = jnp.dot(p, v_ext, preferred_element_type=F32)
    return pv[:, :HEAD_DIM], pv[:, HEAD_DIM:]


def _side_cast_specs(w, n_steps, step_index, col_block=None):
    rows = w.shape[0]
    width, col = (w.shape[1], 0) if col_block is None else col_block
    assert rows % n_steps == 0 and (rows // n_steps) % BF16_ROW_TILE == 0 and w.shape[1] % width == 0
    rb = rows // n_steps
    in_spec = pl.BlockSpec((rb, width), lambda *ids: (step_index(*ids), col))
    out_spec = pl.BlockSpec((rb, width), lambda *ids: (step_index(*ids), 0))
    return in_spec, out_spec, jax.ShapeDtypeStruct((rows, width), BF16), _nbytes((rb, width), F32) + _nbytes((rb, width), BF16)


def _side_casts(in_refs, out_refs, scaled):
    it = iter(in_refs)
    for o_ref, has_scale in zip(out_refs, scaled):
        w = next(it)[...]
        if has_scale:
            w = w * next(it)[...]
        o_ref[...] = w.astype(o_ref.dtype)


def _norm_matmul_kernel(x_ref, g_ref, w_ref, *rest, side_scaled):
    n_side_in = len(side_scaled) + sum(side_scaled)
    o_ref = rest[n_side_in]
    _side_casts(rest[:n_side_in], rest[n_side_in + 1:], side_scaled)
    h = _rms_norm_f32(x_ref[...], g_ref[...]).astype(BF16)
    o_ref[...] = jnp.dot(h, w_ref[...], preferred_element_type=F32).astype(o_ref.dtype)


def _norm_matmul(x, g, w, out_dtype, side_items=(), *, tm=512, name):
    m, d = x.shape
    n = w.shape[1]
    assert m % tm == 0
    n_steps = m // tm
    side_in_specs, side_args, side_out_specs, side_shapes, side_bytes = [], [], [], [], 0
    for w_side, col_block, scale in side_items:
        in_spec, out_spec, shape, nbytes = _side_cast_specs(w_side, n_steps, lambda i: i, col_block)
        side_in_specs.append(in_spec)
        side_args.append(w_side)
        if scale is not None:
            side_in_specs.append(pl.BlockSpec((1, scale.shape[1]), lambda i: (0, 0)))
            side_args.append(scale)
        side_out_specs.append(out_spec)
        side_shapes.append(shape)
        side_bytes += nbytes
    blocks = _nbytes((tm, d), F32) + _nbytes((tm, n), out_dtype) + side_bytes
    temps = _nbytes((tm, d), BF16) + _nbytes((tm, n), F32)
    outs = pl.pallas_call(
        functools.partial(_norm_matmul_kernel, side_scaled=tuple(scale is not None for _, _, scale in side_items)),
        grid=(n_steps,),
        in_specs=[
            pl.BlockSpec((tm, d), lambda i: (i, 0)),
            pl.BlockSpec((1, d), lambda i: (0, 0)),
            pl.BlockSpec((d, n), lambda i: (0, 0), pipeline_mode=pl.Buffered(1)),
        ] + side_in_specs,
        out_specs=[pl.BlockSpec((tm, n), lambda i: (i, 0))] + side_out_specs,
        out_shape=[jax.ShapeDtypeStruct((m, n), out_dtype)] + side_shapes,
        compiler_params=pltpu.CompilerParams(
            dimension_semantics=("parallel",),
            vmem_limit_bytes=_vmem_limit(blocks, _nbytes((d, n), BF16), temps),
        ),
        name=name,
    )(x, g.reshape(1, d), w, *side_args)
    return outs[0], list(outs[1:])


def _out_mlp_kernel(a_ref, b_ref, wo_ref, x_ref, g_ref, w1_ref, w2_ref, *rest, final_norm):
    gf_ref = rest[0] if final_norm else None
    o_ref, h_ref = rest[-2:]
    f = pl.program_id(1)
    kh = a_ref.shape[1]

    @pl.when(f == 0)
    def _():
        x1 = x_ref[...] + jnp.dot(a_ref[...], wo_ref[:kh, :], preferred_element_type=F32)
        x1 = x1 + jnp.dot(b_ref[...], wo_ref[kh:, :], preferred_element_type=F32)
        h_ref[...] = _rms_norm_f32(x1, g_ref[...]).astype(h_ref.dtype)
        o_ref[...] = x1

    a = jnp.maximum(jnp.dot(h_ref[...], w1_ref[...], preferred_element_type=F32), 0.0)
    o_ref[...] += jnp.dot((a * a).astype(BF16), w2_ref[...], preferred_element_type=F32)

    if final_norm:
        @pl.when(f == pl.num_programs(1) - 1)
        def _():
            o_ref[...] = _rms_norm_f32(o_ref[...], gf_ref[...])


def _out_mlp(a, a_col, b, b_col, w_out, x, g, w1, w2, layer, g_final=None, *, tm=512, tf=1024, name):
    m, d = x.shape
    ff = w1.shape[2]
    kh = w_out.shape[0] // 2
    assert m % tm == 0 and ff % tf == 0
    final_norm = g_final is not None
    blocks = (2 * _nbytes((tm, d), F32) + _nbytes((d, tf), BF16) + _nbytes((tf, d), BF16)
              + 2 * _nbytes((tm, kh), BF16))
    resident = _nbytes((tm, d), BF16) + _nbytes(w_out.shape, BF16)
    temps = _nbytes((tm, tf), F32) + _nbytes((tm, tf), BF16) + _nbytes((tm, d), F32)
    gain_spec = pl.BlockSpec((1, d), lambda i, f: (0, 0))
    extra_specs, extra_args = ([gain_spec], [g_final.reshape(1, d)]) if final_norm else ([], [])
    return pl.pallas_call(
        functools.partial(_out_mlp_kernel, final_norm=final_norm),
        grid=(m // tm, ff // tf),
        in_specs=[
            pl.BlockSpec((tm, kh), lambda i, f: (i, a_col)),
            pl.BlockSpec((tm, kh), lambda i, f: (i, b_col)),
            pl.BlockSpec(w_out.shape, lambda i, f: (0, 0), pipeline_mode=pl.Buffered(1)),
            pl.BlockSpec((tm, d), lambda i, f: (i, 0)),
            gain_spec,
            pl.BlockSpec((None, d, tf), lambda i, f: (layer, 0, f)),
            pl.BlockSpec((None, tf, d), lambda i, f: (layer, f, 0)),
        ] + extra_specs,
        out_specs=pl.BlockSpec((tm, d), lambda i, f: (i, 0)),
        out_shape=jax.ShapeDtypeStruct((m, d), F32),
        scratch_shapes=[pltpu.VMEM((tm, d), BF16)],
        compiler_params=pltpu.CompilerParams(
            dimension_semantics=("parallel", "arbitrary"),
            vmem_limit_bytes=_vmem_limit(blocks, resident, temps),
        ),
        name=name,
    )(a, b, w_out, x, g.reshape(1, d), w1, w2, *extra_args)


NA_ROWS_PER_STEP = 32


def _na_bias_table(rpb):
    n_heads = rpb.shape[0]
    col = jnp.arange(GRID_W)
    col_start = jnp.clip(col - NA_COL_WIN // 2, 0, GRID_W - NA_COL_WIN)
    kc = jnp.arange(GRID_W)[None, :]
    valid = (kc >= col_start[:, None]) & (kc < col_start[:, None] + NA_COL_WIN)
    dc = kc - col[:, None] + (NA_COL_WIN - 1)
    rel_rows = jnp.stack([rpb.astype(F32)[:, NA_ROW_WIN - 1 - dl: 2 * NA_ROW_WIN - 1 - dl]
                          for dl in range(NA_ROW_WIN)], axis=1) * LOG2E
    onehot = (dc[:, :, None] == jnp.arange(2 * NA_COL_WIN - 1)[None, None, :]).astype(F32)
    tbl = jnp.einsum("hpid,ckd->hpcik", rel_rows, onehot, precision=lax.Precision.HIGHEST)
    tbl = jnp.where(valid[None, None, :, None, :], tbl, NEG_INF)
    return tbl.reshape(n_heads, NA_ROW_WIN, GRID_W, NA_ROW_WIN * GRID_W)


def _na_kernel(q_ref, k_ref, v_ref, bias_ref, x_ref, g_ref, wb_ref, w_ref, o_ref, pb_ref, wo_ref, *, rows):
    span = NA_ROW_WIN * GRID_W
    h = _rms_norm_f32(x_ref[...], g_ref[...]).astype(BF16)
    pb_ref[...] = jnp.dot(h, wb_ref[...], preferred_element_type=F32)
    _side_casts([w_ref], [wo_ref], [False])

    def one_row(r):
        r0 = jnp.clip(r - NA_ROW_WIN // 2, 0, rows - NA_ROW_WIN)
        q_start = pl.multiple_of(r * GRID_W, GRID_W)
        k_start = pl.multiple_of(r0 * GRID_W, GRID_W)
        q = q_ref[pl.ds(q_start, GRID_W), :]
        k = k_ref[pl.ds(k_start, span), :]
        v = v_ref[pl.ds(k_start, span), :]
        s = _qk_scores(q, k) + bias_ref[r - r0]
        m = jnp.max(s, axis=-1, keepdims=True)
        p = jnp.exp2(s - m).astype(BF16)
        num, den = _pv_with_denominator(p, v)
        o_ref[pl.ds(q_start, GRID_W), :] = (num / den).astype(o_ref.dtype)

    for u in range(NA_ROWS_PER_STEP):
        one_row(pl.program_id(2) * NA_ROWS_PER_STEP + u)


def _na_attention_and_proj(proj, bias_tbl, n_heads, x, g, w_b, side_w, *, name):
    b, t, _ = proj.shape
    rows = t // GRID_W
    assert rows >= NA_ROW_WIN and rows % NA_ROWS_PER_STEP == 0
    n_trips = rows // NA_ROWS_PER_STEP
    n_steps = b * n_heads * n_trips
    m, d = x.shape
    n = w_b.shape[1]
    assert m % n_steps == 0 and (m // n_steps) % BF16_ROW_TILE == 0
    rb = m // n_steps
    span = NA_ROW_WIN * GRID_W
    step_row = lambda bi, h, tr: (bi * n_heads + h) * n_trips + tr
    step_id = lambda bi, h, tr: (step_row(bi, h, tr), 0)
    side_in, side_out, side_shape, side_bytes = _side_cast_specs(side_w, n_steps, step_row)
    head_blk = lambda off: pl.BlockSpec((None, t, HEAD_DIM), lambda bi, h, tr: (bi, 0, off + h))
    blocks = (4 * _nbytes((t, HEAD_DIM), BF16) + _nbytes((NA_ROW_WIN, GRID_W, span), F32)
              + _nbytes((rb, d), F32) + _nbytes((rb, n), F32) + side_bytes)
    return pl.pallas_call(
        functools.partial(_na_kernel, rows=rows),
        grid=(b, n_heads, n_trips),
        in_specs=[
            head_blk(0), head_blk(n_heads), head_blk(2 * n_heads),
            pl.BlockSpec((None, NA_ROW_WIN, GRID_W, span), lambda bi, h, tr: (h, 0, 0, 0)),
            pl.BlockSpec((rb, d), step_id),
            pl.BlockSpec((1, d), lambda bi, h, tr: (0, 0)),
            pl.BlockSpec((d, n), lambda bi, h, tr: (0, 0), pipeline_mode=pl.Buffered(1)),
            side_in,
        ],
        out_specs=[pl.BlockSpec((None, t, HEAD_DIM), lambda bi, h, tr: (bi, 0, h)), pl.BlockSpec((rb, n), step_id),
                   side_out],
        out_shape=[jax.ShapeDtypeStruct((b, t, n_heads * HEAD_DIM), BF16), jax.ShapeDtypeStruct((m, n), F32),
                   side_shape],
        compiler_params=pltpu.CompilerParams(
            dimension_semantics=("parallel", "parallel", "arbitrary"),
            vmem_limit_bytes=_vmem_limit(blocks, _nbytes((d, n), BF16), ATTN_TEMP_BYTES),
        ),
        name=name,
    )(proj, proj, proj, bias_tbl, x, g.reshape(1, d), w_b, side_w)


DIL_SUPER = 2048
DIL_QBLK = 128
DIL_BLOCKS_PER_ITER = 16
DIL_MERGE_ROWS = 256


def _band_bias(n_q, span, offset, radius, slope2):
    rel = (lax.broadcasted_iota(jnp.int32, (n_q, 1), 0)
           - lax.broadcasted_iota(jnp.int32, (1, span), 1)) + offset
    dist = jnp.abs(rel)
    return jnp.where(dist <= radius, -slope2 * dist.astype(F32), NEG_INF)


def _dilated_kernel(slope_ref, q_ref, k_ref, v_ref, w_ref, o_ref, wo_ref, bias_ref, acc_ref, m_ref, l_ref, *,
                    seq_len):
    h = pl.program_id(1)
    sb = pl.program_id(2)
    _side_casts([w_ref], [wo_ref], [False])

    @pl.when(sb == 0)
    def _():
        slope2 = slope_ref[h] * LOG2E
        for br, (window, dil) in enumerate(DILATED_BRANCHES):
            radius = window // (2 * dil)
            for var in range(3):
                bias_ref[3 * br + var] = _band_bias(DIL_QBLK, DIL_QBLK + 2 * radius, var * radius, radius, slope2 * dil)

    for br, (window, dil) in enumerate(DILATED_BRANCHES):
        radius = window // (2 * dil)
        span = DIL_QBLK + 2 * radius
        sub = seq_len // dil
        blocks_per_res = DIL_SUPER // dil // DIL_QBLK

        def one_block(idx, br=br, dil=dil, radius=radius, span=span, sub=sub, blocks_per_res=blocks_per_res):
            res = idx // blocks_per_res
            c = idx % blocks_per_res
            q_row = res + dil * DIL_QBLK * c
            u0 = sb * (DIL_SUPER // dil) + DIL_QBLK * c
            ws = jnp.clip(u0 - radius, 0, sub - span)
            k_row = res + dil * ws
            if dil == 1:
                q_idx = pl.ds(q_row, DIL_QBLK)
                k_idx = pl.ds(k_row, span)
            else:
                q_idx = pl.ds(q_row, DIL_QBLK, stride=dil)
                k_idx = pl.ds(k_row, span, stride=dil)
            q = q_ref[q_idx, :].astype(BF16)
            k = k_ref[k_idx, :].astype(BF16)
            v = v_ref[k_idx, :].astype(BF16)
            s = _qk_scores(q, k) + bias_ref[3 * br + (u0 - ws) // radius]
            m = jnp.max(s, axis=-1, keepdims=True)
            p = jnp.exp2(s - m).astype(BF16)
            num, den = _pv_with_denominator(p, v)
            acc_ref[br, q_idx, :] = num
            l_ref[br, q_idx, :] = den
            m_ref[br, q_idx, :] = jnp.broadcast_to(m, (DIL_QBLK, HEAD_DIM))

        def body(it, carry, one_block=one_block):
            for u in range(DIL_BLOCKS_PER_ITER):
                one_block(it * DIL_BLOCKS_PER_ITER + u)
            return carry

        lax.fori_loop(0, DIL_SUPER // DIL_QBLK // DIL_BLOCKS_PER_ITER, body, 0)

    n_br = len(DILATED_BRANCHES)

    def merge(it, carry):
        rows = pl.ds(pl.multiple_of(it * DIL_MERGE_ROWS, DIL_MERGE_ROWS), DIL_MERGE_ROWS)
        ms = [m_ref[br, rows, :] for br in range(n_br)]
        m_all = functools.reduce(jnp.maximum, ms)
        ws = [jnp.exp2(m - m_all) for m in ms]
        num = sum(w * acc_ref[br, rows, :] for br, w in enumerate(ws))
        den = sum(w * l_ref[br, rows, :] for br, w in enumerate(ws))
        o_ref[rows, :] = (num / den).astype(o_ref.dtype)
        return carry

    lax.fori_loop(0, DIL_SUPER // DIL_MERGE_ROWS, merge, 0)


def _dilated_attention(proj, n_heads, side_w, *, name):
    b, t, _ = proj.shape
    n_sb = t // DIL_SUPER
    side_in, side_out, side_shape, side_bytes = _side_cast_specs(
        side_w, b * n_heads * n_sb, lambda bi, h, s: (bi * n_heads + h) * n_sb + s)
    n_br = len(DILATED_BRANCHES)
    spans = {DIL_QBLK + window // dil for window, dil in DILATED_BRANCHES}
    assert len(spans) == 1, "bias scratch assumes one key span for all branches"
    span = spans.pop()
    assert t % DIL_SUPER == 0 and DIL_SUPER // DIL_QBLK % DIL_BLOCKS_PER_ITER == 0 and DIL_SUPER % DIL_MERGE_ROWS == 0
    for window, dil in DILATED_BRANCHES:
        assert DIL_SUPER % (dil * DIL_QBLK) == 0 and t // dil >= span and window // (2 * dil) <= DIL_QBLK
    slopes = 2.0 ** (-8.0 * jnp.arange(1, n_heads + 1, dtype=F32) / n_heads)
    kv_blk = lambda off: pl.BlockSpec((None, t, HEAD_DIM), lambda bi, h, s: (bi, 0, off + h))
    blocks = (2 * _nbytes((t, HEAD_DIM), F32) + _nbytes((DIL_SUPER, HEAD_DIM), F32)
              + _nbytes((DIL_SUPER, HEAD_DIM), BF16) + side_bytes)
    stats = pltpu.VMEM((n_br, DIL_SUPER, HEAD_DIM), F32)
    scratch = 3 * _nbytes((n_br, DIL_SUPER, HEAD_DIM), F32) + _nbytes((3 * n_br, DIL_QBLK, span), F32)
    return pl.pallas_call(
        functools.partial(_dilated_kernel, seq_len=t),
        grid=(b, n_heads, t // DIL_SUPER),
        in_specs=[
            pl.BlockSpec(memory_space=pltpu.SMEM),
            pl.BlockSpec((None, DIL_SUPER, HEAD_DIM), lambda bi, h, s: (bi, s, h)),
            kv_blk(n_heads), kv_blk(2 * n_heads), side_in,
        ],
        out_specs=[pl.BlockSpec((None, DIL_SUPER, HEAD_DIM), lambda bi, h, s: (bi, s, h)), side_out],
        out_shape=[jax.ShapeDtypeStruct((b, t, n_heads * HEAD_DIM), BF16), side_shape],
        scratch_shapes=[pltpu.VMEM((3 * n_br, DIL_QBLK, span), F32), stats, stats, stats],
        compiler_params=pltpu.CompilerParams(
            dimension_semantics=("parallel", "parallel", "arbitrary"),
            vmem_limit_bytes=_vmem_limit(blocks, scratch, ATTN_TEMP_BYTES),
        ),
        name=name,
    )(slopes, proj, proj, proj, side_w)


SWA_QBLK = 256
SWA_BLOCKS_PER_STEP = 16


def _swa_kernel(slope_ref, sink_ref, q_ref, k_ref, v_ref, o_ref, bias_ref, *, seq_len, group):
    kvh = pl.program_id(1)
    span = SWA_QBLK + 2 * C_RADIUS

    @pl.when(pl.program_id(2) == 0)
    def _():
        for g in range(group):
            slope2 = slope_ref[kvh * group + g] * LOG2E
            for var in range(3):
                bias_ref[3 * g + var] = _band_bias(SWA_QBLK, span, var * C_RADIUS, C_RADIUS, slope2)

    for blk in range(SWA_BLOCKS_PER_STEP):
        rows = slice(blk * SWA_QBLK, (blk + 1) * SWA_QBLK)
        s0 = (pl.program_id(2) * SWA_BLOCKS_PER_STEP + blk) * SWA_QBLK
        ks = pl.multiple_of(jnp.clip(s0 - C_RADIUS, 0, seq_len - span), C_RADIUS)
        var = (s0 - ks) // C_RADIUS
        k = k_ref[pl.ds(ks, span), :]
        v = v_ref[pl.ds(ks, span), :]
        for g in range(group):
            cols = slice(g * HEAD_DIM, (g + 1) * HEAD_DIM)
            s = _qk_scores(q_ref[rows, cols], k) + bias_ref[3 * g + var]
            sink2 = sink_ref[kvh * group + g] * LOG2E
            m = jnp.maximum(jnp.max(s, axis=-1, keepdims=True), sink2)
            p = jnp.exp2(s - m).astype(BF16)
            num, den = _pv_with_denominator(p, v)
            o = num / (den + jnp.exp2(sink2 - m))
            o_ref[rows, cols] = o.astype(o_ref.dtype)


def _swa_attention(proj, sink, n_heads, n_kv, *, name):
    b, t, _ = proj.shape
    group = n_heads // n_kv
    span = SWA_QBLK + 2 * C_RADIUS
    step_rows = SWA_QBLK * SWA_BLOCKS_PER_STEP
    assert t % step_rows == 0 and t >= span and C_RADIUS <= SWA_QBLK and SWA_QBLK % C_RADIUS == 0
    slopes = 2.0 ** (-8.0 * jnp.arange(1, n_heads + 1, dtype=F32) / n_heads)
    kv_blk = lambda off: pl.BlockSpec((None, t, HEAD_DIM), lambda bi, kh, s: (bi, 0, off + kh))
    q_blk = pl.BlockSpec((None, step_rows, group * HEAD_DIM), lambda bi, kh, s: (bi, s, kh))
    blocks = 2 * _nbytes((t, HEAD_DIM), BF16) + 2 * _nbytes((step_rows, group * HEAD_DIM), BF16)
    bias_bytes = _nbytes((3 * group, SWA_QBLK, span), F32)
    return pl.pallas_call(
        functools.partial(_swa_kernel, seq_len=t, group=group),
        grid=(b, n_kv, t // step_rows),
        in_specs=[
            pl.BlockSpec(memory_space=pltpu.SMEM),
            pl.BlockSpec(memory_space=pltpu.SMEM),
            q_blk, kv_blk(n_heads), kv_blk(n_heads + n_kv),
        ],
        out_specs=q_blk,
        out_shape=jax.ShapeDtypeStruct((b, t, n_heads * HEAD_DIM), BF16),
        scratch_shapes=[pltpu.VMEM((3 * group, SWA_QBLK, span), F32)],
        compiler_params=pltpu.CompilerParams(
            dimension_semantics=("parallel", "parallel", "arbitrary"),
            vmem_limit_bytes=_vmem_limit(blocks, bias_bytes, ATTN_TEMP_BYTES),
        ),
        name=name,
    )(slopes, sink.astype(F32), proj, proj, proj)


def _q_scale_row(n_cols, n_q_cols):
    col = jnp.arange(n_cols)
    return jnp.where(col < n_q_cols, LOG2E * HEAD_DIM ** -0.5, 1.0).astype(F32)[None, :]


def kernel(x, attn_norm, mlp_norm, w_mlp_in, w_mlp_out, even_w_in, even_rpb, even_w_out,
           odd_w_qkv, odd_sink, odd_w_out, final_norm):
    b, t, d = x.shape
    depth = attn_norm.shape[0]
    n_heads_a = even_rpb.shape[1]
    wa = n_heads_a * HEAD_DIM
    n_heads_b = (even_w_in.shape[2] - 3 * wa) // (3 * HEAD_DIM)
    n_heads_c = odd_sink.shape[1]
    n_kv_c = (odd_w_qkv.shape[2] // HEAD_DIM - n_heads_c) // 2
    xf = x.reshape(b * t, d)
    w1_bf16 = w2_bf16 = None
    w_qkv_bf16 = w_out_odd_bf16 = None

    for i in range(depth):
        j = i // 2
        if i % 2 == 0:
            w_in = even_w_in[j]
            wb_cols = w_in.shape[1] - 3 * wa
            assert wb_cols == 3 * wa, "the mixer-B column block is addressed as block 1 of two equal halves"
            w_in_a = (w_in[:, :3 * wa] * _q_scale_row(3 * wa, wa)).astype(BF16)
            side = [(w_in, (wb_cols, 1), _q_scale_row(wb_cols, n_heads_b * HEAD_DIM)), (even_w_out[j], None, None)]
            if i + 1 < depth:
                jo = (i + 1) // 2
                side += [(odd_w_qkv[jo], None, _q_scale_row(odd_w_qkv.shape[2], n_heads_c * HEAD_DIM)),
                         (odd_w_out[jo], None, None)]
            proj_a, casts = _norm_matmul(xf, attn_norm[i], w_in_a, BF16, side, name=f"l{i}_proj_a")
            w_in_b, w_out_bf16 = casts[:2]
            if i + 1 < depth:
                w_qkv_bf16, w_out_odd_bf16 = casts[2:]
            w1_pair, w2_pair = w_mlp_in[i:i + 2], w_mlp_out[i:i + 2]
            o_a, proj_b, w1_bf16 = _na_attention_and_proj(
                proj_a.reshape(b, t, -1), _na_bias_table(even_rpb[j]), n_heads_a, xf, attn_norm[i], w_in_b,
                w1_pair.reshape(-1, w1_pair.shape[-1]), name=f"l{i}_na_proj_b")
            o_b, w2_bf16 = _dilated_attention(proj_b.reshape(b, t, -1), n_heads_b,
                                              w2_pair.reshape(-1, w2_pair.shape[-1]), name=f"l{i}_dilated")
            w1_bf16, w2_bf16 = w1_bf16.reshape(w1_pair.shape), w2_bf16.reshape(w2_pair.shape)
            attn_out = (o_a.reshape(b * t, -1), 0, o_b.reshape(b * t, -1), 0, w_out_bf16)
        else:
            proj_c, _ = _norm_matmul(xf, attn_norm[i], w_qkv_bf16, BF16, name=f"l{i}_proj_c")
            o_c = _swa_attention(proj_c.reshape(b, t, -1), odd_sink[j], n_heads_c, n_kv_c, name=f"l{i}_swa")
            o_c = o_c.reshape(b * t, -1)
            attn_out = (o_c, 0, o_c, 1, w_out_odd_bf16)
        g_final = final_norm if i == depth - 1 else None
        xf = _out_mlp(*attn_out, xf, mlp_norm[i], w1_bf16, w2_bf16, i % 2, g_final, name=f"l{i}_out_mlp")
    return xf.reshape(b, t, d)
```

```python
import functools
import math

import jax
import jax.numpy as jnp
from jax import lax
from jax.experimental import pallas as pl
from jax.experimental.pallas import tpu as pltpu

HEAD_DIM = 128
GRID_W = 64
NA_ROW_WIN = 8
NA_COL_WIN = 16
DILATED_BRANCHES = ((128, 1), (512, 4), (2048, 16))
C_RADIUS = 128
NORM_EPS = 1e-6
NEG_INF = -1e30
LOG2E = math.log2(math.e)

V7X_VMEM_LIMIT_CAP = 56 * 1024 * 1024
VMEM_INTERNAL_SCRATCH = 2 * 1024 * 1024
ATTN_TEMP_BYTES = 8 * 1024 * 1024
BF16_ROW_TILE = 16
BF16 = jnp.bfloat16
F32 = jnp.float32


def _vmem_limit(block_bytes, scratch_bytes, temp_bytes):
    need = 2 * block_bytes + scratch_bytes + temp_bytes + VMEM_INTERNAL_SCRATCH
    return int(min(need, V7X_VMEM_LIMIT_CAP))


def _nbytes(shape, dtype):
    n = 1
    for s in shape:
        n *= s
    return n * jnp.dtype(dtype).itemsize


def _rms_norm_f32(x, g):
    y = x * lax.rsqrt(jnp.mean(x * x, axis=-1, keepdims=True) + NORM_EPS)
    return y * g


def _qk_scores(q, k):
    return lax.dot_general(q, k, (((1,), (1,)), ((), ())), preferred_element_type=F32)


def _pv_with_denominator(p, v):
    v_ext = jnp.concatenate([v, jnp.ones_like(v)], axis=1)
    pv = jnp.dot(p, v_ext, preferred_element_type=F32)
    return pv[:, :HEAD_DIM], pv[:, HEAD_DIM:]


def _side_cast_specs(w, n_steps, step_index, col_block=None):
    rows = w.shape[0]
    width, col = (w.shape[1], 0) if col_block is None else col_block
    assert rows % n_steps == 0 and (rows // n_steps) % BF16_ROW_TILE == 0 and w.shape[1] % width == 0
    rb = rows // n_steps
    in_spec = pl.BlockSpec((rb, width), lambda *ids: (step_index(*ids), col))
    out_spec = pl.BlockSpec((rb, width), lambda *ids: (step_index(*ids), 0))
    return in_spec, out_spec, jax.ShapeDtypeStruct((rows, width), BF16), _nbytes((rb, width), F32) + _nbytes((rb, width), BF16)


def _side_casts(in_refs, out_refs, scaled):
    it = iter(in_refs)
    for o_ref, has_scale in zip(out_refs, scaled):
        w = next(it)[...]
        if has_scale:
            w = w * next(it)[...]
        o_ref[...] = w.astype(o_ref.dtype)


def _norm_matmul_kernel(x_ref, g_ref, w_ref, *rest, side_scaled):
    n_side_in = len(side_scaled) + sum(side_scaled)
    o_ref = rest[n_side_in]
    _side_casts(rest[:n_side_in], rest[n_side_in + 1:], side_scaled)
    h = _rms_norm_f32(x_ref[...], g_ref[...]).astype(BF16)
    o_ref[...] = jnp.dot(h, w_ref[...], preferred_element_type=F32).astype(o_ref.dtype)


def _norm_matmul(x, g, w, out_dtype, side_items=(), *, tm=512, name):
    m, d = x.shape
    n = w.shape[1]
    assert m % tm == 0
    n_steps = m // tm
    side_in_specs, side_args, side_out_specs, side_shapes, side_bytes = [], [], [], [], 0
    for w_side, col_block, scale in side_items:
        in_spec, out_spec, shape, nbytes = _side_cast_specs(w_side, n_steps, lambda i: i, col_block)
        side_in_specs.append(in_spec)
        side_args.append(w_side)
        if scale is not None:
            side_in_specs.append(pl.BlockSpec((1, scale.shape[1]), lambda i: (0, 0)))
            side_args.append(scale)
        side_out_specs.append(out_spec)
        side_shapes.append(shape)
        side_bytes += nbytes
    blocks = _nbytes((tm, d), F32) + _nbytes((tm, n), out_dtype) + side_bytes
    temps = _nbytes((tm, d), BF16) + _nbytes((tm, n), F32)
    outs = pl.pallas_call(
        functools.partial(_norm_matmul_kernel, side_scaled=tuple(scale is not None for _, _, scale in side_items)),
        grid=(n_steps,),
        in_specs=[
            pl.BlockSpec((tm, d), lambda i: (i, 0)),
            pl.BlockSpec((1, d), lambda i: (0, 0)),
            pl.BlockSpec((d, n), lambda i: (0, 0), pipeline_mode=pl.Buffered(1)),
        ] + side_in_specs,
        out_specs=[pl.BlockSpec((tm, n), lambda i: (i, 0))] + side_out_specs,
        out_shape=[jax.ShapeDtypeStruct((m, n), out_dtype)] + side_shapes,
        compiler_params=pltpu.CompilerParams(
            dimension_semantics=("parallel",),
            vmem_limit_bytes=_vmem_limit(blocks, _nbytes((d, n), BF16), temps),
        ),
        name=name,
    )(x, g.reshape(1, d), w, *side_args)
    return outs[0], list(outs[1:])


def _out_mlp_kernel(a_ref, b_ref, wo_ref, x_ref, g_ref, w1_ref, w2_ref, *rest, final_norm):
    gf_ref = rest[0] if final_norm else None
    o_ref, h_ref = rest[-2:]
    f = pl.program_id(1)
    kh = a_ref.shape[1]

    @pl.when(f == 0)
    def _():
        x1 = x_ref[...] + jnp.dot(a_ref[...], wo_ref[:kh, :], preferred_element_type=F32)
        x1 = x1 + jnp.dot(b_ref[...], wo_ref[kh:, :], preferred_element_type=F32)
        h_ref[...] = _rms_norm_f32(x1, g_ref[...]).astype(h_ref.dtype)
        o_ref[...] = x1

    def accumulated():
        a = jnp.maximum(jnp.dot(h_ref[...], w1_ref[...], preferred_element_type=F32), 0.0)
        return o_ref[...] + jnp.dot((a * a).astype(BF16), w2_ref[...], preferred_element_type=F32)

    if final_norm:
        last = pl.num_programs(1) - 1

        @pl.when(f < last)
        def _():
            o_ref[...] = accumulated()

        @pl.when(f == last)
        def _():
            o_ref[...] = _rms_norm_f32(accumulated(), gf_ref[...])
    else:
        o_ref[...] = accumulated()


def _out_mlp(a, a_col, b, b_col, w_out, x, g, w1, w2, layer, g_final=None, *, tm=512, tf=1024, name):
    m, d = x.shape
    ff = w1.shape[2]
    kh = w_out.shape[0] // 2
    assert m % tm == 0 and ff % tf == 0
    final_norm = g_final is not None
    blocks = (2 * _nbytes((tm, d), F32) + _nbytes((d, tf), BF16) + _nbytes((tf, d), BF16)
              + 2 * _nbytes((tm, kh), BF16))
    resident = _nbytes((tm, d), BF16) + _nbytes(w_out.shape, BF16)
    temps = _nbytes((tm, tf), F32) + _nbytes((tm, tf), BF16) + _nbytes((tm, d), F32)
    gain_spec = pl.BlockSpec((1, d), lambda i, f: (0, 0))
    extra_specs, extra_args = ([gain_spec], [g_final.reshape(1, d)]) if final_norm else ([], [])
    return pl.pallas_call(
        functools.partial(_out_mlp_kernel, final_norm=final_norm),
        grid=(m // tm, ff // tf),
        in_specs=[
            pl.BlockSpec((tm, kh), lambda i, f: (i, a_col)),
            pl.BlockSpec((tm, kh), lambda i, f: (i, b_col)),
            pl.BlockSpec(w_out.shape, lambda i, f: (0, 0), pipeline_mode=pl.Buffered(1)),
            pl.BlockSpec((tm, d), lambda i, f: (i, 0)),
            gain_spec,
            pl.BlockSpec((None, d, tf), lambda i, f: (layer, 0, f)),
            pl.BlockSpec((None, tf, d), lambda i, f: (layer, f, 0)),
        ] + extra_specs,
        out_specs=pl.BlockSpec((tm, d), lambda i, f: (i, 0)),
        out_shape=jax.ShapeDtypeStruct((m, d), F32),
        scratch_shapes=[pltpu.VMEM((tm, d), BF16)],
        compiler_params=pltpu.CompilerParams(
            dimension_semantics=("parallel", "arbitrary"),
            vmem_limit_bytes=_vmem_limit(blocks, resident, temps),
        ),
        name=name,
    )(a, b, w_out, x, g.reshape(1, d), w1, w2, *extra_args)


NA_ROWS_PER_STEP = 32


def _na_bias_table(rpb):
    n_heads = rpb.shape[0]
    col = jnp.arange(GRID_W)
    col_start = jnp.clip(col - NA_COL_WIN // 2, 0, GRID_W - NA_COL_WIN)
    kc = jnp.arange(GRID_W)[None, :]
    valid = (kc >= col_start[:, None]) & (kc < col_start[:, None] + NA_COL_WIN)
    dc = kc - col[:, None] + (NA_COL_WIN - 1)
    rel_rows = jnp.stack([rpb.astype(F32)[:, NA_ROW_WIN - 1 - dl: 2 * NA_ROW_WIN - 1 - dl]
                          for dl in range(NA_ROW_WIN)], axis=1) * LOG2E
    onehot = (dc[:, :, None] == jnp.arange(2 * NA_COL_WIN - 1)[None, None, :]).astype(F32)
    tbl = jnp.einsum("hpid,ckd->hpcik", rel_rows, onehot, precision=lax.Precision.HIGHEST)
    tbl = jnp.where(valid[None, None, :, None, :], tbl, NEG_INF)
    return tbl.reshape(n_heads, NA_ROW_WIN, GRID_W, NA_ROW_WIN * GRID_W)


def _na_kernel(q_ref, k_ref, v_ref, bias_ref, x_ref, g_ref, wb_ref, w_ref, o_ref, pb_ref, wo_ref, *, rows):
    span = NA_ROW_WIN * GRID_W
    h = _rms_norm_f32(x_ref[...], g_ref[...]).astype(BF16)
    pb_ref[...] = jnp.dot(h, wb_ref[...], preferred_element_type=F32)
    _side_casts([w_ref], [wo_ref], [False])

    def one_row(r):
        r0 = jnp.clip(r - NA_ROW_WIN // 2, 0, rows - NA_ROW_WIN)
        q_start = pl.multiple_of(r * GRID_W, GRID_W)
        k_start = pl.multiple_of(r0 * GRID_W, GRID_W)
        q = q_ref[pl.ds(q_start, GRID_W), :]
        k = k_ref[pl.ds(k_start, span), :]
        v = v_ref[pl.ds(k_start, span), :]
        s = _qk_scores(q, k) + bias_ref[r - r0]
        m = jnp.max(s, axis=-1, keepdims=True)
        p = jnp.exp2(s - m).astype(BF16)
        num, den = _pv_with_denominator(p, v)
        o_ref[pl.ds(q_start, GRID_W), :] = (num / den).astype(o_ref.dtype)

    for u in range(NA_ROWS_PER_STEP):
        one_row(pl.program_id(2) * NA_ROWS_PER_STEP + u)


def _na_attention_and_proj(proj, bias_tbl, n_heads, x, g, w_b, side_w, *, name):
    b, t, _ = proj.shape
    rows = t // GRID_W
    assert rows >= NA_ROW_WIN and rows % NA_ROWS_PER_STEP == 0
    n_trips = rows // NA_ROWS_PER_STEP
    n_steps = b * n_heads * n_trips
    m, d = x.shape
    n = w_b.shape[1]
    assert m % n_steps == 0 and (m // n_steps) % BF16_ROW_TILE == 0
    rb = m // n_steps
    span = NA_ROW_WIN * GRID_W
    step_row = lambda bi, h, tr: (bi * n_heads + h) * n_trips + tr
    step_id = lambda bi, h, tr: (step_row(bi, h, tr), 0)
    side_in, side_out, side_shape, side_bytes = _side_cast_specs(side_w, n_steps, step_row)
    head_blk = lambda off: pl.BlockSpec((None, t, HEAD_DIM), lambda bi, h, tr: (bi, 0, off + h))
    blocks = (4 * _nbytes((t, HEAD_DIM), BF16) + _nbytes((NA_ROW_WIN, GRID_W, span), F32)
              + _nbytes((rb, d), F32) + _nbytes((rb, n), F32) + side_bytes)
    return pl.pallas_call(
        functools.partial(_na_kernel, rows=rows),
        grid=(b, n_heads, n_trips),
        in_specs=[
            head_blk(0), head_blk(n_heads), head_blk(2 * n_heads),
            pl.BlockSpec((None, NA_ROW_WIN, GRID_W, span), lambda bi, h, tr: (h, 0, 0, 0)),
            pl.BlockSpec((rb, d), step_id),
            pl.BlockSpec((1, d), lambda bi, h, tr: (0, 0)),
            pl.BlockSpec((d, n), lambda bi, h, tr: (0, 0), pipeline_mode=pl.Buffered(1)),
            side_in,
        ],
        out_specs=[pl.BlockSpec((None, t, HEAD_DIM), lambda bi, h, tr: (bi, 0, h)), pl.BlockSpec((rb, n), step_id),
                   side_out],
        out_shape=[jax.ShapeDtypeStruct((b, t, n_heads * HEAD_DIM), BF16), jax.ShapeDtypeStruct((m, n), F32),
                   side_shape],
        compiler_params=pltpu.CompilerParams(
            dimension_semantics=("parallel", "parallel", "arbitrary"),
            vmem_limit_bytes=_vmem_limit(blocks, _nbytes((d, n), BF16), ATTN_TEMP_BYTES),
        ),
        name=name,
    )(proj, proj, proj, bias_tbl, x, g.reshape(1, d), w_b, side_w)


DIL_SUPER = 2048
DIL_QBLK = 128
DIL_BLOCKS_PER_ITER = 16
DIL_MERGE_ROWS = 256


def _band_bias(n_q, span, offset, radius, slope2):
    rel = (lax.broadcasted_iota(jnp.int32, (n_q, 1), 0)
           - lax.broadcasted_iota(jnp.int32, (1, span), 1)) + offset
    dist = jnp.abs(rel)
    return jnp.where(dist <= radius, -slope2 * dist.astype(F32), NEG_INF)


def _dilated_kernel(slope_ref, q_ref, k_ref, v_ref, w_ref, o_ref, wo_ref, bias_ref, acc_ref, m_ref, l_ref, *,
                    seq_len):
    h = pl.program_id(1)
    sb = pl.program_id(2)
    _side_casts([w_ref], [wo_ref], [False])

    @pl.when(sb == 0)
    def _():
        slope2 = slope_ref[h] * LOG2E
        for br, (window, dil) in enumerate(DILATED_BRANCHES):
            radius = window // (2 * dil)
            for var in range(3):
                bias_ref[3 * br + var] = _band_bias(DIL_QBLK, DIL_QBLK + 2 * radius, var * radius, radius, slope2 * dil)

    for br, (window, dil) in enumerate(DILATED_BRANCHES):
        radius = window // (2 * dil)
        span = DIL_QBLK + 2 * radius
        sub = seq_len // dil
        blocks_per_res = DIL_SUPER // dil // DIL_QBLK

        def one_block(idx, br=br, dil=dil, radius=radius, span=span, sub=sub, blocks_per_res=blocks_per_res):
            res = idx // blocks_per_res
            c = idx % blocks_per_res
            q_row = res + dil * DIL_QBLK * c
            u0 = sb * (DIL_SUPER // dil) + DIL_QBLK * c
            ws = jnp.clip(u0 - radius, 0, sub - span)
            k_row = res + dil * ws
            if dil == 1:
                q_idx = pl.ds(q_row, DIL_QBLK)
                k_idx = pl.ds(k_row, span)
            else:
                q_idx = pl.ds(q_row, DIL_QBLK, stride=dil)
                k_idx = pl.ds(k_row, span, stride=dil)
            q = q_ref[q_idx, :].astype(BF16)
            k = k_ref[k_idx, :].astype(BF16)
            v = v_ref[k_idx, :].astype(BF16)
            s = _qk_scores(q, k) + bias_ref[3 * br + (u0 - ws) // radius]
            m = jnp.max(s, axis=-1, keepdims=True)
            p = jnp.exp2(s - m).astype(BF16)
            num, den = _pv_with_denominator(p, v)
            acc_ref[br, q_idx, :] = num
            l_ref[br, q_idx, :] = den
            m_ref[br, q_idx, :] = jnp.broadcast_to(m, (DIL_QBLK, HEAD_DIM))

        def body(it, carry, one_block=one_block):
            for u in range(DIL_BLOCKS_PER_ITER):
                one_block(it * DIL_BLOCKS_PER_ITER + u)
            return carry

        lax.fori_loop(0, DIL_SUPER // DIL_QBLK // DIL_BLOCKS_PER_ITER, body, 0)

    n_br = len(DILATED_BRANCHES)

    def merge(it, carry):
        rows = pl.ds(pl.multiple_of(it * DIL_MERGE_ROWS, DIL_MERGE_ROWS), DIL_MERGE_ROWS)
        ms = [m_ref[br, rows, :] for br in range(n_br)]
        m_all = functools.reduce(jnp.maximum, ms)
        ws = [jnp.exp2(m - m_all) for m in ms]
        num = sum(w * acc_ref[br, rows, :] for br, w in enumerate(ws))
        den = sum(w * l_ref[br, rows, :] for br, w in enumerate(ws))
        o_ref[rows, :] = (num / den).astype(o_ref.dtype)
        return carry

    lax.fori_loop(0, DIL_SUPER // DIL_MERGE_ROWS, merge, 0)


def _dilated_attention(proj, n_heads, side_w, *, name):
    b, t, _ = proj.shape
    n_sb = t // DIL_SUPER
    side_in, side_out, side_shape, side_bytes = _side_cast_specs(
        side_w, b * n_heads * n_sb, lambda bi, h, s: (bi * n_heads + h) * n_sb + s)
    n_br = len(DILATED_BRANCHES)
    spans = {DIL_QBLK + window // dil for window, dil in DILATED_BRANCHES}
    assert len(spans) == 1, "bias scratch assumes one key span for all branches"
    span = spans.pop()
    assert t % DIL_SUPER == 0 and DIL_SUPER // DIL_QBLK % DIL_BLOCKS_PER_ITER == 0 and DIL_SUPER % DIL_MERGE_ROWS == 0
    for window, dil in DILATED_BRANCHES:
        assert DIL_SUPER % (dil * DIL_QBLK) == 0 and t // dil >= span and window // (2 * dil) <= DIL_QBLK
    slopes = 2.0 ** (-8.0 * jnp.arange(1, n_heads + 1, dtype=F32) / n_heads)
    kv_blk = lambda off: pl.BlockSpec((None, t, HEAD_DIM), lambda bi, h, s: (bi, 0, off + h))
    blocks = (2 * _nbytes((t, HEAD_DIM), F32) + _nbytes((DIL_SUPER, HEAD_DIM), F32)
              + _nbytes((DIL_SUPER, HEAD_DIM), BF16) + side_bytes)
    stats = pltpu.VMEM((n_br, DIL_SUPER, HEAD_DIM), F32)
    scratch = 3 * _nbytes((n_br, DIL_SUPER, HEAD_DIM), F32) + _nbytes((3 * n_br, DIL_QBLK, span), F32)
    return pl.pallas_call(
        functools.partial(_dilated_kernel, seq_len=t),
        grid=(b, n_heads, t // DIL_SUPER),
        in_specs=[
            pl.BlockSpec(memory_space=pltpu.SMEM),
            pl.BlockSpec((None, DIL_SUPER, HEAD_DIM), lambda bi, h, s: (bi, s, h)),
            kv_blk(n_heads), kv_blk(2 * n_heads), side_in,
        ],
        out_specs=[pl.BlockSpec((None, DIL_SUPER, HEAD_DIM), lambda bi, h, s: (bi, s, h)), side_out],
        out_shape=[jax.ShapeDtypeStruct((b, t, n_heads * HEAD_DIM), BF16), side_shape],
        scratch_shapes=[pltpu.VMEM((3 * n_br, DIL_QBLK, span), F32), stats, stats, stats],
        compiler_params=pltpu.CompilerParams(
            dimension_semantics=("parallel", "parallel", "arbitrary"),
            vmem_limit_bytes=_vmem_limit(blocks, scratch, ATTN_TEMP_BYTES),
        ),
        name=name,
    )(slopes, proj, proj, proj, side_w)


SWA_QBLK = 256
SWA_BLOCKS_PER_STEP = 16


def _swa_kernel(slope_ref, sink_ref, q_ref, k_ref, v_ref, o_ref, bias_ref, *, seq_len, group):
    kvh = pl.program_id(1)
    span = SWA_QBLK + 2 * C_RADIUS

    @pl.when(pl.program_id(2) == 0)
    def _():
        for g in range(group):
            slope2 = slope_ref[kvh * group + g] * LOG2E
            for var in range(3):
                bias_ref[3 * g + var] = _band_bias(SWA_QBLK, span, var * C_RADIUS, C_RADIUS, slope2)

    for blk in range(SWA_BLOCKS_PER_STEP):
        rows = slice(blk * SWA_QBLK, (blk + 1) * SWA_QBLK)
        s0 = (pl.program_id(2) * SWA_BLOCKS_PER_STEP + blk) * SWA_QBLK
        ks = pl.multiple_of(jnp.clip(s0 - C_RADIUS, 0, seq_len - span), C_RADIUS)
        var = (s0 - ks) // C_RADIUS
        k = k_ref[pl.ds(ks, span), :]
        v = v_ref[pl.ds(ks, span), :]
        for g in range(group):
            cols = slice(g * HEAD_DIM, (g + 1) * HEAD_DIM)
            s = _qk_scores(q_ref[rows, cols], k) + bias_ref[3 * g + var]
            sink2 = sink_ref[kvh * group + g] * LOG2E
            m = jnp.maximum(jnp.max(s, axis=-1, keepdims=True), sink2)
            p = jnp.exp2(s - m).astype(BF16)
            num, den = _pv_with_denominator(p, v)
            o = num / (den + jnp.exp2(sink2 - m))
            o_ref[rows, cols] = o.astype(o_ref.dtype)


def _swa_attention(proj, sink, n_heads, n_kv, *, name):
    b, t, _ = proj.shape
    group = n_heads // n_kv
    span = SWA_QBLK + 2 * C_RADIUS
    step_rows = SWA_QBLK * SWA_BLOCKS_PER_STEP
    assert t % step_rows == 0 and t >= span and C_RADIUS <= SWA_QBLK and SWA_QBLK % C_RADIUS == 0
    slopes = 2.0 ** (-8.0 * jnp.arange(1, n_heads + 1, dtype=F32) / n_heads)
    kv_blk = lambda off: pl.BlockSpec((None, t, HEAD_DIM), lambda bi, kh, s: (bi, 0, off + kh))
    q_blk = pl.BlockSpec((None, step_rows, group * HEAD_DIM), lambda bi, kh, s: (bi, s, kh))
    blocks = 2 * _nbytes((t, HEAD_DIM), BF16) + 2 * _nbytes((step_rows, group * HEAD_DIM), BF16)
    bias_bytes = _nbytes((3 * group, SWA_QBLK, span), F32)
    return pl.pallas_call(
        functools.partial(_swa_kernel, seq_len=t, group=group),
        grid=(b, n_kv, t // step_rows),
        in_specs=[
            pl.BlockSpec(memory_space=pltpu.SMEM),
            pl.BlockSpec(memory_space=pltpu.SMEM),
            q_blk, kv_blk(n_heads), kv_blk(n_heads + n_kv),
        ],
        out_specs=q_blk,
        out_shape=jax.ShapeDtypeStruct((b, t, n_heads * HEAD_DIM), BF16),
        scratch_shapes=[pltpu.VMEM((3 * group, SWA_QBLK, span), F32)],
        compiler_params=pltpu.CompilerParams(
            dimension_semantics=("parallel", "parallel", "arbitrary"),
            vmem_limit_bytes=_vmem_limit(blocks, bias_bytes, ATTN_TEMP_BYTES),
        ),
        name=name,
    )(slopes, sink.astype(F32), proj, proj, proj)


def _q_scale_row(n_cols, n_q_cols):
    col = jnp.arange(n_cols)
    return jnp.where(col < n_q_cols, LOG2E * HEAD_DIM ** -0.5, 1.0).astype(F32)[None, :]


def kernel(x, attn_norm, mlp_norm, w_mlp_in, w_mlp_out, even_w_in, even_rpb, even_w_out,
           odd_w_qkv, odd_sink, odd_w_out, final_norm):
    b, t, d = x.shape
    depth = attn_norm.shape[0]
    n_heads_a = even_rpb.shape[1]
    wa = n_heads_a * HEAD_DIM
    n_heads_b = (even_w_in.shape[2] - 3 * wa) // (3 * HEAD_DIM)
    n_heads_c = odd_sink.shape[1]
    n_kv_c = (odd_w_qkv.shape[2] // HEAD_DIM - n_heads_c) // 2
    xf = x.reshape(b * t, d)
    w1_bf16 = w2_bf16 = None
    w_qkv_bf16 = w_out_odd_bf16 = None

    for i in range(depth):
        j = i // 2
        if i % 2 == 0:
            w_in = even_w_in[j]
            wb_cols = w_in.shape[1] - 3 * wa
            assert wb_cols == 3 * wa, "the mixer-B column block is addressed as block 1 of two equal halves"
            w_in_a = (w_in[:, :3 * wa] * _q_scale_row(3 * wa, wa)).astype(BF16)
            side = [(w_in, (wb_cols, 1), _q_scale_row(wb_cols, n_heads_b * HEAD_DIM)), (even_w_out[j], None, None)]
            if i + 1 < depth:
                jo = (i + 1) // 2
                side += [(odd_w_qkv[jo], None, _q_scale_row(odd_w_qkv.shape[2], n_heads_c * HEAD_DIM)),
                         (odd_w_out[jo], None, None)]
            proj_a, casts = _norm_matmul(xf, attn_norm[i], w_in_a, BF16, side, name=f"l{i}_proj_a")
            w_in_b, w_out_bf16 = casts[:2]
            if i + 1 < depth:
                w_qkv_bf16, w_out_odd_bf16 = casts[2:]
            w1_pair, w2_pair = w_mlp_in[i:i + 2], w_mlp_out[i:i + 2]
            o_a, proj_b, w1_bf16 = _na_attention_and_proj(
                proj_a.reshape(b, t, -1), _na_bias_table(even_rpb[j]), n_heads_a, xf, attn_norm[i], w_in_b,
                w1_pair.reshape(-1, w1_pair.shape[-1]), name=f"l{i}_na_proj_b")
            o_b, w2_bf16 = _dilated_attention(proj_b.reshape(b, t, -1), n_heads_b,
                                              w2_pair.reshape(-1, w2_pair.shape[-1]), name=f"l{i}_dilated")
            w1_bf16, w2_bf16 = w1_bf16.reshape(w1_pair.shape), w2_bf16.reshape(w2_pair.shape)
            attn_out = (o_a.reshape(b * t, -1), 0, o_b.reshape(b * t, -1), 0, w_out_bf16)
        else:
            proj_c, _ = _norm_matmul(xf, attn_norm[i], w_qkv_bf16, BF16, name=f"l{i}_proj_c")
            o_c = _swa_attention(proj_c.reshape(b, t, -1), odd_sink[j], n_heads_c, n_kv_c, name=f"l{i}_swa")
            o_c = o_c.reshape(b * t, -1)
            attn_out = (o_c, 0, o_c, 1, w_out_odd_bf16)
        g_final = final_norm if i == depth - 1 else None
        xf = _out_mlp(*attn_out, xf, mlp_norm[i], w1_bf16, w2_bf16, i % 2, g_final, name=f"l{i}_out_mlp")
    return xf.reshape(b, t, d)
```

```python
import functools
import math

import jax
import jax.numpy as jnp
from jax import lax
from jax.experimental import pallas as pl
from jax.experimental.pallas import tpu as pltpu

HEAD_DIM = 128
GRID_W = 64
NA_ROW_WIN = 8
NA_COL_WIN = 16
DILATED_BRANCHES = ((128, 1), (512, 4), (2048, 16))
C_RADIUS = 128
NORM_EPS = 1e-6
NEG_INF = -1e30
LOG2E = math.log2(math.e)

V7X_VMEM_LIMIT_CAP = 56 * 1024 * 1024
VMEM_INTERNAL_SCRATCH = 2 * 1024 * 1024
ATTN_TEMP_BYTES = 8 * 1024 * 1024
BF16_ROW_TILE = 16
BF16 = jnp.bfloat16
F32 = jnp.float32


def _vmem_limit(block_bytes, scratch_bytes, temp_bytes):
    need = 2 * block_bytes + scratch_bytes + temp_bytes + VMEM_INTERNAL_SCRATCH
    return int(min(need, V7X_VMEM_LIMIT_CAP))


def _nbytes(shape, dtype):
    n = 1
    for s in shape:
        n *= s
    return n * jnp.dtype(dtype).itemsize


def _rms_norm_f32(x, g):
    y = x * lax.rsqrt(jnp.mean(x * x, axis=-1, keepdims=True) + NORM_EPS)
    return y * g


def _qk_scores(q, k):
    return lax.dot_general(q, k, (((1,), (1,)), ((), ())), preferred_element_type=F32)


def _pv_with_denominator(p, v):
    v_ext = jnp.concatenate([v, jnp.ones_like(v)], axis=1)
    pv = jnp.dot(p, v_ext, preferred_element_type=F32)
    return pv[:, :HEAD_DIM], pv[:, HEAD_DIM:]


def _side_cast_specs(w, n_steps, step_index, col_block=None):
    rows = w.shape[0]
    width, col = (w.shape[1], 0) if col_block is None else col_block
    assert rows % n_steps == 0 and (rows // n_steps) % BF16_ROW_TILE == 0 and w.shape[1] % width == 0
    rb = rows // n_steps
    in_spec = pl.BlockSpec((rb, width), lambda *ids: (step_index(*ids), col))
    out_spec = pl.BlockSpec((rb, width), lambda *ids: (step_index(*ids), 0))
    return in_spec, out_spec, jax.ShapeDtypeStruct((rows, width), BF16), _nbytes((rb, width), F32) + _nbytes((rb, width), BF16)


def _side_casts(in_refs, out_refs, scaled):
    it = iter(in_refs)
    for o_ref, has_scale in zip(out_refs, scaled):
        w = next(it)[...]
        if has_scale:
            w = w * next(it)[...]
        o_ref[...] = w.astype(o_ref.dtype)


def _norm_matmul_kernel(x_ref, g_ref, w_ref, *rest, side_scaled):
    n_side_in = len(side_scaled) + sum(side_scaled)
    o_ref = rest[n_side_in]
    _side_casts(rest[:n_side_in], rest[n_side_in + 1:], side_scaled)
    h = _rms_norm_f32(x_ref[...], g_ref[...]).astype(BF16)
    o_ref[...] = jnp.dot(h, w_ref[...], preferred_element_type=F32).astype(o_ref.dtype)


def _norm_matmul(x, g, w, out_dtype, side_items=(), *, tm=512, name):
    m, d = x.shape
    n = w.shape[1]
    assert m % tm == 0
    n_steps = m // tm
    side_in_specs, side_args, side_out_specs, side_shapes, side_bytes = [], [], [], [], 0
    for w_side, col_block, scale in side_items:
        in_spec, out_spec, shape, nbytes = _side_cast_specs(w_side, n_steps, lambda i: i, col_block)
        side_in_specs.append(in_spec)
        side_args.append(w_side)
        if scale is not None:
            side_in_specs.append(pl.BlockSpec((1, scale.shape[1]), lambda i: (0, 0)))
            side_args.append(scale)
        side_out_specs.append(out_spec)
        side_shapes.append(shape)
        side_bytes += nbytes
    blocks = _nbytes((tm, d), F32) + _nbytes((tm, n), out_dtype) + side_bytes
    temps = _nbytes((tm, d), BF16) + _nbytes((tm, n), F32)
    outs = pl.pallas_call(
        functools.partial(_norm_matmul_kernel, side_scaled=tuple(scale is not None for _, _, scale in side_items)),
        grid=(n_steps,),
        in_specs=[
            pl.BlockSpec((tm, d), lambda i: (i, 0)),
            pl.BlockSpec((1, d), lambda i: (0, 0)),
            pl.BlockSpec((d, n), lambda i: (0, 0), pipeline_mode=pl.Buffered(1)),
        ] + side_in_specs,
        out_specs=[pl.BlockSpec((tm, n), lambda i: (i, 0))] + side_out_specs,
        out_shape=[jax.ShapeDtypeStruct((m, n), out_dtype)] + side_shapes,
        compiler_params=pltpu.CompilerParams(
            dimension_semantics=("parallel",),
            vmem_limit_bytes=_vmem_limit(blocks, _nbytes((d, n), BF16), temps),
        ),
        name=name,
    )(x, g.reshape(1, d), w, *side_args)
    return outs[0], list(outs[1:])


MLP_HIDDEN_CHUNK = 1024


def _out_mlp_kernel(a_ref, b_ref, wo_ref, x_ref, g_ref, w1_ref, w2_ref, *rest, final_norm):
    gf_ref = rest[0] if final_norm else None
    o_ref, h_ref = rest[-2:]
    f = pl.program_id(1)
    kh = a_ref.shape[1]

    @pl.when(f == 0)
    def _():
        x1 = x_ref[...] + jnp.dot(a_ref[...], wo_ref[:kh, :], preferred_element_type=F32)
        x1 = x1 + jnp.dot(b_ref[...], wo_ref[kh:, :], preferred_element_type=F32)
        h_ref[...] = _rms_norm_f32(x1, g_ref[...]).astype(h_ref.dtype)
        o_ref[...] = x1

    a = jnp.maximum(jnp.dot(h_ref[...], w1_ref[...], preferred_element_type=F32), 0.0)
    o_ref[...] += jnp.dot((a * a).astype(BF16), w2_ref[...], preferred_element_type=F32)

    if final_norm:
        @pl.when(f == pl.num_programs(1) - 1)
        def _():
            o_ref[...] = _rms_norm_f32(o_ref[...], gf_ref[...])


def _out_mlp(a, a_col, b, b_col, w_out, x, g, w1, w2, layer, g_final=None, *, tm=512, name):
    m, d = x.shape
    tf = w1.shape[3]
    ff = w1.shape[1] * tf
    kh = w_out.shape[0] // 2
    assert m % tm == 0 and w2.shape[1] == ff
    final_norm = g_final is not None
    blocks = (2 * _nbytes((tm, d), F32) + _nbytes((d, tf), BF16) + _nbytes((tf, d), BF16)
              + 2 * _nbytes((tm, kh), BF16))
    resident = _nbytes((tm, d), BF16) + _nbytes(w_out.shape, BF16)
    temps = _nbytes((tm, tf), F32) + _nbytes((tm, tf), BF16) + _nbytes((tm, d), F32)
    gain_spec = pl.BlockSpec((1, d), lambda i, f: (0, 0))
    extra_specs, extra_args = ([gain_spec], [g_final.reshape(1, d)]) if final_norm else ([], [])
    return pl.pallas_call(
        functools.partial(_out_mlp_kernel, final_norm=final_norm),
        grid=(m // tm, ff // tf),
        in_specs=[
            pl.BlockSpec((tm, kh), lambda i, f: (i, a_col)),
            pl.BlockSpec((tm, kh), lambda i, f: (i, b_col)),
            pl.BlockSpec(w_out.shape, lambda i, f: (0, 0), pipeline_mode=pl.Buffered(1)),
            pl.BlockSpec((tm, d), lambda i, f: (i, 0)),
            gain_spec,
            pl.BlockSpec((None, None, d, tf), lambda i, f: (layer, f, 0, 0)),
            pl.BlockSpec((None, tf, d), lambda i, f: (layer, f, 0)),
        ] + extra_specs,
        out_specs=pl.BlockSpec((tm, d), lambda i, f: (i, 0)),
        out_shape=jax.ShapeDtypeStruct((m, d), F32),
        scratch_shapes=[pltpu.VMEM((tm, d), BF16)],
        compiler_params=pltpu.CompilerParams(
            dimension_semantics=("parallel", "arbitrary"),
            vmem_limit_bytes=_vmem_limit(blocks, resident, temps),
        ),
        name=name,
    )(a, b, w_out, x, g.reshape(1, d), w1, w2, *extra_args)


NA_ROWS_PER_STEP = 32


def _na_bias_table(rpb):
    n_heads = rpb.shape[0]
    col = jnp.arange(GRID_W)
    col_start = jnp.clip(col - NA_COL_WIN // 2, 0, GRID_W - NA_COL_WIN)
    kc = jnp.arange(GRID_W)[None, :]
    valid = (kc >= col_start[:, None]) & (kc < col_start[:, None] + NA_COL_WIN)
    dc = kc - col[:, None] + (NA_COL_WIN - 1)
    rel_rows = jnp.stack([rpb.astype(F32)[:, NA_ROW_WIN - 1 - dl: 2 * NA_ROW_WIN - 1 - dl]
                          for dl in range(NA_ROW_WIN)], axis=1) * LOG2E
    onehot = (dc[:, :, None] == jnp.arange(2 * NA_COL_WIN - 1)[None, None, :]).astype(F32)
    tbl = jnp.einsum("hpid,ckd->hpcik", rel_rows, onehot, precision=lax.Precision.HIGHEST)
    tbl = jnp.where(valid[None, None, :, None, :], tbl, NEG_INF)
    return tbl.reshape(n_heads, NA_ROW_WIN, GRID_W, NA_ROW_WIN * GRID_W)


def _na_kernel(q_ref, k_ref, v_ref, bias_ref, x_ref, g_ref, wb_ref, w_ref, o_ref, pb_ref, wo_ref, *, rows):
    span = NA_ROW_WIN * GRID_W
    h = _rms_norm_f32(x_ref[...], g_ref[...]).astype(BF16)
    pb_ref[...] = jnp.dot(h, wb_ref[...], preferred_element_type=F32)
    w = w_ref[...]
    chunk = wo_ref.shape[2]
    for c in range(wo_ref.shape[0]):
        wo_ref[c] = w[:, c * chunk:(c + 1) * chunk].astype(wo_ref.dtype)

    def one_row(r):
        r0 = jnp.clip(r - NA_ROW_WIN // 2, 0, rows - NA_ROW_WIN)
        q_start = pl.multiple_of(r * GRID_W, GRID_W)
        k_start = pl.multiple_of(r0 * GRID_W, GRID_W)
        q = q_ref[pl.ds(q_start, GRID_W), :]
        k = k_ref[pl.ds(k_start, span), :]
        v = v_ref[pl.ds(k_start, span), :]
        s = _qk_scores(q, k) + bias_ref[r - r0]
        m = jnp.max(s, axis=-1, keepdims=True)
        p = jnp.exp2(s - m).astype(BF16)
        num, den = _pv_with_denominator(p, v)
        o_ref[pl.ds(q_start, GRID_W), :] = (num / den).astype(o_ref.dtype)

    for u in range(NA_ROWS_PER_STEP):
        one_row(pl.program_id(2) * NA_ROWS_PER_STEP + u)


def _na_attention_and_proj(proj, bias_tbl, n_heads, x, g, w_b, side_w, side_rows, side_chunk, *, name):
    b, t, _ = proj.shape
    rows = t // GRID_W
    assert rows >= NA_ROW_WIN and rows % NA_ROWS_PER_STEP == 0
    n_trips = rows // NA_ROWS_PER_STEP
    n_steps = b * n_heads * n_trips
    m, d = x.shape
    n = w_b.shape[1]
    assert m % n_steps == 0 and (m // n_steps) % BF16_ROW_TILE == 0
    rb = m // n_steps
    span = NA_ROW_WIN * GRID_W
    step_row = lambda bi, h, tr: (bi * n_heads + h) * n_trips + tr
    step_id = lambda bi, h, tr: (step_row(bi, h, tr), 0)
    side_in, _, _, side_bytes = _side_cast_specs(side_w, n_steps, step_row)
    srb = side_w.shape[0] // n_steps
    n_chunks = side_w.shape[1] // side_chunk
    steps_per_layer = side_rows // srb
    assert side_w.shape[0] % side_rows == 0 and side_rows % srb == 0 and side_w.shape[1] % side_chunk == 0
    side_out = pl.BlockSpec((None, n_chunks, srb, side_chunk),
                            lambda bi, h, tr: (step_row(bi, h, tr) // steps_per_layer, 0,
                                               step_row(bi, h, tr) % steps_per_layer, 0))
    side_shape = jax.ShapeDtypeStruct((side_w.shape[0] // side_rows, n_chunks, side_rows, side_chunk), BF16)
    head_blk = lambda off: pl.BlockSpec((None, t, HEAD_DIM), lambda bi, h, tr: (bi, 0, off + h))
    blocks = (4 * _nbytes((t, HEAD_DIM), BF16) + _nbytes((NA_ROW_WIN, GRID_W, span), F32)
              + _nbytes((rb, d), F32) + _nbytes((rb, n), F32) + side_bytes)
    return pl.pallas_call(
        functools.partial(_na_kernel, rows=rows),
        grid=(b, n_heads, n_trips),
        in_specs=[
            head_blk(0), head_blk(n_heads), head_blk(2 * n_heads),
            pl.BlockSpec((None, NA_ROW_WIN, GRID_W, span), lambda bi, h, tr: (h, 0, 0, 0)),
            pl.BlockSpec((rb, d), step_id),
            pl.BlockSpec((1, d), lambda bi, h, tr: (0, 0)),
            pl.BlockSpec((d, n), lambda bi, h, tr: (0, 0), pipeline_mode=pl.Buffered(1)),
            side_in,
        ],
        out_specs=[pl.BlockSpec((None, t, HEAD_DIM), lambda bi, h, tr: (bi, 0, h)), pl.BlockSpec((rb, n), step_id),
                   side_out],
        out_shape=[jax.ShapeDtypeStruct((b, t, n_heads * HEAD_DIM), BF16), jax.ShapeDtypeStruct((m, n), F32),
                   side_shape],
        compiler_params=pltpu.CompilerParams(
            dimension_semantics=("parallel", "parallel", "arbitrary"),
            vmem_limit_bytes=_vmem_limit(blocks, _nbytes((d, n), BF16), ATTN_TEMP_BYTES),
        ),
        name=name,
    )(proj, proj, proj, bias_tbl, x, g.reshape(1, d), w_b, side_w)


DIL_SUPER = 2048
DIL_QBLK = 128
DIL_BLOCKS_PER_ITER = 16
DIL_MERGE_ROWS = 256


def _band_bias(n_q, span, offset, radius, slope2):
    rel = (lax.broadcasted_iota(jnp.int32, (n_q, 1), 0)
           - lax.broadcasted_iota(jnp.int32, (1, span), 1)) + offset
    dist = jnp.abs(rel)
    return jnp.where(dist <= radius, -slope2 * dist.astype(F32), NEG_INF)


def _dilated_kernel(slope_ref, q_ref, k_ref, v_ref, w_ref, o_ref, wo_ref, bias_ref, acc_ref, m_ref, l_ref, *,
                    seq_len):
    h = pl.program_id(1)
    sb = pl.program_id(2)
    _side_casts([w_ref], [wo_ref], [False])

    @pl.when(sb == 0)
    def _():
        slope2 = slope_ref[h] * LOG2E
        for br, (window, dil) in enumerate(DILATED_BRANCHES):
            radius = window // (2 * dil)
            for var in range(3):
                bias_ref[3 * br + var] = _band_bias(DIL_QBLK, DIL_QBLK + 2 * radius, var * radius, radius, slope2 * dil)

    for br, (window, dil) in enumerate(DILATED_BRANCHES):
        radius = window // (2 * dil)
        span = DIL_QBLK + 2 * radius
        sub = seq_len // dil
        blocks_per_res = DIL_SUPER // dil // DIL_QBLK

        def one_block(idx, br=br, dil=dil, radius=radius, span=span, sub=sub, blocks_per_res=blocks_per_res):
            res = idx // blocks_per_res
            c = idx % blocks_per_res
            q_row = res + dil * DIL_QBLK * c
            u0 = sb * (DIL_SUPER // dil) + DIL_QBLK * c
            ws = jnp.clip(u0 - radius, 0, sub - span)
            k_row = res + dil * ws
            if dil == 1:
                q_idx = pl.ds(q_row, DIL_QBLK)
                k_idx = pl.ds(k_row, span)
            else:
                q_idx = pl.ds(q_row, DIL_QBLK, stride=dil)
                k_idx = pl.ds(k_row, span, stride=dil)
            q = q_ref[q_idx, :].astype(BF16)
            k = k_ref[k_idx, :].astype(BF16)
            v = v_ref[k_idx, :].astype(BF16)
            s = _qk_scores(q, k) + bias_ref[3 * br + (u0 - ws) // radius]
            m = jnp.max(s, axis=-1, keepdims=True)
            p = jnp.exp2(s - m).astype(BF16)
            num, den = _pv_with_denominator(p, v)
            acc_ref[br, q_idx, :] = num
            l_ref[br, q_idx, :] = den
            m_ref[br, q_idx, :] = jnp.broadcast_to(m, (DIL_QBLK, HEAD_DIM))

        def body(it, carry, one_block=one_block):
            for u in range(DIL_BLOCKS_PER_ITER):
                one_block(it * DIL_BLOCKS_PER_ITER + u)
            return carry

        lax.fori_loop(0, DIL_SUPER // DIL_QBLK // DIL_BLOCKS_PER_ITER, body, 0)

    n_br = len(DILATED_BRANCHES)

    def merge(it, carry):
        rows = pl.ds(pl.multiple_of(it * DIL_MERGE_ROWS, DIL_MERGE_ROWS), DIL_MERGE_ROWS)
        ms = [m_ref[br, rows, :] for br in range(n_br)]
        m_all = functools.reduce(jnp.maximum, ms)
        ws = [jnp.exp2(m - m_all) for m in ms]
        num = sum(w * acc_ref[br, rows, :] for br, w in enumerate(ws))
        den = sum(w * l_ref[br, rows, :] for br, w in enumerate(ws))
        o_ref[rows, :] = (num / den).astype(o_ref.dtype)
        return carry

    lax.fori_loop(0, DIL_SUPER // DIL_MERGE_ROWS, merge, 0)


def _dilated_attention(proj, n_heads, side_w, *, name):
    b, t, _ = proj.shape
    n_sb = t // DIL_SUPER
    side_in, side_out, side_shape, side_bytes = _side_cast_specs(
        side_w, b * n_heads * n_sb, lambda bi, h, s: (bi * n_heads + h) * n_sb + s)
    n_br = len(DILATED_BRANCHES)
    spans = {DIL_QBLK + window // dil for window, dil in DILATED_BRANCHES}
    assert len(spans) == 1, "bias scratch assumes one key span for all branches"
    span = spans.pop()
    assert t % DIL_SUPER == 0 and DIL_SUPER // DIL_QBLK % DIL_BLOCKS_PER_ITER == 0 and DIL_SUPER % DIL_MERGE_ROWS == 0
    for window, dil in DILATED_BRANCHES:
        assert DIL_SUPER % (dil * DIL_QBLK) == 0 and t // dil >= span and window // (2 * dil) <= DIL_QBLK
    slopes = 2.0 ** (-8.0 * jnp.arange(1, n_heads + 1, dtype=F32) / n_heads)
    kv_blk = lambda off: pl.BlockSpec((None, t, HEAD_DIM), lambda bi, h, s: (bi, 0, off + h))
    blocks = (2 * _nbytes((t, HEAD_DIM), F32) + _nbytes((DIL_SUPER, HEAD_DIM), F32)
              + _nbytes((DIL_SUPER, HEAD_DIM), BF16) + side_bytes)
    stats = pltpu.VMEM((n_br, DIL_SUPER, HEAD_DIM), F32)
    scratch = 3 * _nbytes((n_br, DIL_SUPER, HEAD_DIM), F32) + _nbytes((3 * n_br, DIL_QBLK, span), F32)
    return pl.pallas_call(
        functools.partial(_dilated_kernel, seq_len=t),
        grid=(b, n_heads, t // DIL_SUPER),
        in_specs=[
            pl.BlockSpec(memory_space=pltpu.SMEM),
            pl.BlockSpec((None, DIL_SUPER, HEAD_DIM), lambda bi, h, s: (bi, s, h)),
            kv_blk(n_heads), kv_blk(2 * n_heads), side_in,
        ],
        out_specs=[pl.BlockSpec((None, DIL_SUPER, HEAD_DIM), lambda bi, h, s: (bi, s, h)), side_out],
        out_shape=[jax.ShapeDtypeStruct((b, t, n_heads * HEAD_DIM), BF16), side_shape],
        scratch_shapes=[pltpu.VMEM((3 * n_br, DIL_QBLK, span), F32), stats, stats, stats],
        compiler_params=pltpu.CompilerParams(
            dimension_semantics=("parallel", "parallel", "arbitrary"),
            vmem_limit_bytes=_vmem_limit(blocks, scratch, ATTN_TEMP_BYTES),
        ),
        name=name,
    )(slopes, proj, proj, proj, side_w)


SWA_QBLK = 256
SWA_BLOCKS_PER_STEP = 16


def _swa_kernel(slope_ref, sink_ref, q_ref, k_ref, v_ref, o_ref, bias_ref, *, seq_len, group):
    kvh = pl.program_id(1)
    span = SWA_QBLK + 2 * C_RADIUS

    @pl.when(pl.program_id(2) == 0)
    def _():
        for g in range(group):
            slope2 = slope_ref[kvh * group + g] * LOG2E
            for var in range(3):
                bias_ref[3 * g + var] = _band_bias(SWA_QBLK, span, var * C_RADIUS, C_RADIUS, slope2)

    for blk in range(SWA_BLOCKS_PER_STEP):
        rows = slice(blk * SWA_QBLK, (blk + 1) * SWA_QBLK)
        s0 = (pl.program_id(2) * SWA_BLOCKS_PER_STEP + blk) * SWA_QBLK
        ks = pl.multiple_of(jnp.clip(s0 - C_RADIUS, 0, seq_len - span), C_RADIUS)
        var = (s0 - ks) // C_RADIUS
        k = k_ref[pl.ds(ks, span), :]
        v = v_ref[pl.ds(ks, span), :]
        for g in range(group):
            cols = slice(g * HEAD_DIM, (g + 1) * HEAD_DIM)
            s = _qk_scores(q_ref[rows, cols], k) + bias_ref[3 * g + var]
            sink2 = sink_ref[kvh * group + g] * LOG2E
            m = jnp.maximum(jnp.max(s, axis=-1, keepdims=True), sink2)
            p = jnp.exp2(s - m).astype(BF16)
            num, den = _pv_with_denominator(p, v)
            o = num / (den + jnp.exp2(sink2 - m))
            o_ref[rows, cols] = o.astype(o_ref.dtype)


def _swa_attention(proj, sink, n_heads, n_kv, *, name):
    b, t, _ = proj.shape
    group = n_heads // n_kv
    span = SWA_QBLK + 2 * C_RADIUS
    step_rows = SWA_QBLK * SWA_BLOCKS_PER_STEP
    assert t % step_rows == 0 and t >= span and C_RADIUS <= SWA_QBLK and SWA_QBLK % C_RADIUS == 0
    slopes = 2.0 ** (-8.0 * jnp.arange(1, n_heads + 1, dtype=F32) / n_heads)
    kv_blk = lambda off: pl.BlockSpec((None, t, HEAD_DIM), lambda bi, kh, s: (bi, 0, off + kh))
    q_blk = pl.BlockSpec((None, step_rows, group * HEAD_DIM), lambda bi, kh, s: (bi, s, kh))
    blocks = 2 * _nbytes((t, HEAD_DIM), BF16) + 2 * _nbytes((step_rows, group * HEAD_DIM), BF16)
    bias_bytes = _nbytes((3 * group, SWA_QBLK, span), F32)
    return pl.pallas_call(
        functools.partial(_swa_kernel, seq_len=t, group=group),
        grid=(b, n_kv, t // step_rows),
        in_specs=[
            pl.BlockSpec(memory_space=pltpu.SMEM),
            pl.BlockSpec(memory_space=pltpu.SMEM),
            q_blk, kv_blk(n_heads), kv_blk(n_heads + n_kv),
        ],
        out_specs=q_blk,
        out_shape=jax.ShapeDtypeStruct((b, t, n_heads * HEAD_DIM), BF16),
        scratch_shapes=[pltpu.VMEM((3 * group, SWA_QBLK, span), F32)],
        compiler_params=pltpu.CompilerParams(
            dimension_semantics=("parallel", "parallel", "arbitrary"),
            vmem_limit_bytes=_vmem_limit(blocks, bias_bytes, ATTN_TEMP_BYTES),
        ),
        name=name,
    )(slopes, sink.astype(F32), proj, proj, proj)


def _q_scale_row(n_cols, n_q_cols):
    col = jnp.arange(n_cols)
    return jnp.where(col < n_q_cols, LOG2E * HEAD_DIM ** -0.5, 1.0).astype(F32)[None, :]


def kernel(x, attn_norm, mlp_norm, w_mlp_in, w_mlp_out, even_w_in, even_rpb, even_w_out,
           odd_w_qkv, odd_sink, odd_w_out, final_norm):
    b, t, d = x.shape
    depth = attn_norm.shape[0]
    n_heads_a = even_rpb.shape[1]
    wa = n_heads_a * HEAD_DIM
    n_heads_b = (even_w_in.shape[2] - 3 * wa) // (3 * HEAD_DIM)
    n_heads_c = odd_sink.shape[1]
    n_kv_c = (odd_w_qkv.shape[2] // HEAD_DIM - n_heads_c) // 2
    xf = x.reshape(b * t, d)
    w1_bf16 = w2_bf16 = None
    w_qkv_bf16 = w_out_odd_bf16 = None

    for i in range(depth):
        j = i // 2
        if i % 2 == 0:
            w_in = even_w_in[j]
            wb_cols = w_in.shape[1] - 3 * wa
            assert wb_cols == 3 * wa, "the mixer-B column block is addressed as block 1 of two equal halves"
            w_in_a = (w_in[:, :3 * wa] * _q_scale_row(3 * wa, wa)).astype(BF16)
            side = [(w_in, (wb_cols, 1), _q_scale_row(wb_cols, n_heads_b * HEAD_DIM)), (even_w_out[j], None, None)]
            if i + 1 < depth:
                jo = (i + 1) // 2
                side += [(odd_w_qkv[jo], None, _q_scale_row(odd_w_qkv.shape[2], n_heads_c * HEAD_DIM)),
                         (odd_w_out[jo], None, None)]
            proj_a, casts = _norm_matmul(xf, attn_norm[i], w_in_a, BF16, side, name=f"l{i}_proj_a")
            w_in_b, w_out_bf16 = casts[:2]
            if i + 1 < depth:
                w_qkv_bf16, w_out_odd_bf16 = casts[2:]
            w1_pair, w2_pair = w_mlp_in[i:i + 2], w_mlp_out[i:i + 2]
            o_a, proj_b, w1_bf16 = _na_attention_and_proj(
                proj_a.reshape(b, t, -1), _na_bias_table(even_rpb[j]), n_heads_a, xf, attn_norm[i], w_in_b,
                w1_pair.reshape(-1, w1_pair.shape[-1]), w1_pair.shape[1], MLP_HIDDEN_CHUNK, name=f"l{i}_na_proj_b")
            o_b, w2_bf16 = _dilated_attention(proj_b.reshape(b, t, -1), n_heads_b,
                                              w2_pair.reshape(-1, w2_pair.shape[-1]), name=f"l{i}_dilated")
            w2_bf16 = w2_bf16.reshape(w2_pair.shape)
            attn_out = (o_a.reshape(b * t, -1), 0, o_b.reshape(b * t, -1), 0, w_out_bf16)
        else:
            proj_c, _ = _norm_matmul(xf, attn_norm[i], w_qkv_bf16, BF16, name=f"l{i}_proj_c")
            o_c = _swa_attention(proj_c.reshape(b, t, -1), odd_sink[j], n_heads_c, n_kv_c, name=f"l{i}_swa")
            o_c = o_c.reshape(b * t, -1)
            attn_out = (o_c, 0, o_c, 1, w_out_odd_bf16)
        g_final = final_norm if i == depth - 1 else None
        xf = _out_mlp(*attn_out, xf, mlp_norm[i], w1_bf16, w2_bf16, i % 2, g_final, name=f"l{i}_out_mlp")
    return xf.reshape(b, t, d)
```
